```python
import jax
import jax.numpy as jnp
from jax import lax
import numpy as np


D_MODEL = 1024
BATCH = 8
SEQ = 4096
DEPTH = 4

GRID_W = 64
CTX_LEN = 256
HEAD_DIM = 64
ROPE_THETA = 10000.0
Q_BLOCK = 128
EPS = 1e-6

GLA_HEADS = 4
GLA_DK = 32
GLA_DV = 64
GLA_WIDTH = GLA_HEADS * GLA_DV
GLA_GATE_RANK = 16
GLA_GATE_NORM = 16.0
GLA_CHUNK = 64

MLA_HEADS = 6
MLA_NOPE = 64
MLA_ROPE = 32
MLA_V = 64
MLA_Q_LORA = 256
MLA_KV_LORA = 256
MLA_WIDTH = MLA_HEADS * MLA_V

GQA_HEADS = 6
GQA_KV_HEADS = 2
GQA_GROUP = GQA_HEADS // GQA_KV_HEADS
GQA_WIDTH = GQA_HEADS * HEAD_DIM

D_MIX = GLA_WIDTH + MLA_WIDTH + GQA_WIDTH
GLA_COLS = (GLA_HEADS * GLA_DK, GLA_HEADS * GLA_DK, GLA_WIDTH, GLA_WIDTH, 2 * GLA_GATE_RANK)
MLA_COLS = (MLA_Q_LORA, MLA_KV_LORA, MLA_ROPE, MLA_WIDTH)
GQA_COLS = (GQA_HEADS * HEAD_DIM, GQA_KV_HEADS * HEAD_DIM, GQA_KV_HEADS * HEAD_DIM, GQA_WIDTH)
IN_COLS = GLA_COLS + MLA_COLS + GQA_COLS
D_IN = sum(IN_COLS)

kernel_name = 'hybrid_gla_mla_gqa_diffusion_trunk'


def rms_norm(x, w):
    xf = x.astype(jnp.float32)
    y = xf * lax.rsqrt(jnp.mean(xf * xf, axis=-1, keepdims=True) + EPS)
    return (y * w.astype(jnp.float32)).astype(x.dtype)


def split_cols(u, sizes):
    idx = [int(i) for i in np.cumsum(sizes)[:-1]]
    return jnp.split(u, idx, axis=-1)


def rope_2d_tables(rows, d_rot):
    quarter = d_rot // 4
    row = jnp.repeat(jnp.arange(rows), GRID_W).astype(jnp.float32)
    col = jnp.tile(jnp.arange(GRID_W), rows).astype(jnp.float32)
    freqs = ROPE_THETA ** (-jnp.arange(quarter, dtype=jnp.float32) / quarter)
    ang = jnp.stack([row[:, None] * freqs, col[:, None] * freqs], axis=1)
    return jnp.cos(ang), jnp.sin(ang)


def apply_rope_2d(x, cos, sin):
    n, d = x.shape[1], x.shape[-1]
    q = d // 4
    xr = x.astype(jnp.float32).reshape(x.shape[:-1] + (2, 2, q))
    shape = (1, n) + (1,) * (x.ndim - 3) + (2, q)
    cos = cos.reshape(shape)
    sin = sin.reshape(shape)
    x1, x2 = xr[..., 0, :], xr[..., 1, :]
    out = jnp.stack([x1 * cos - x2 * sin, x2 * cos + x1 * sin], axis=-2)
    return out.reshape(x.shape).astype(x.dtype)


def attend(q, keys, vals, scale):
    s = jnp.einsum('bqhgd,bkhd->bhgqk', q, keys) * scale
    p = jax.nn.softmax(s.astype(jnp.float32), axis=-1).astype(vals.dtype)
    return jnp.einsum('bhgqk,bkhe->bqhge', p, vals)


def latent_attention(q, k, v, k_c, v_c, scale):
    keys = jnp.concatenate([k_c, k], axis=1)
    vals = jnp.concatenate([v_c, v], axis=1)
    b, n, hk, g, dq = q.shape
    nb = n // Q_BLOCK
    qb = jnp.moveaxis(q.reshape(b, nb, Q_BLOCK, hk, g, dq), 1, 0)
    o = lax.map(lambda qblk: attend(qblk, keys, vals, scale), qb)
    return jnp.moveaxis(o, 0, 1).reshape(b, n, hk, g, vals.shape[-1])


def gla_chunk_states(k, v, g, s0):
    bsz, t, h, dk = k.shape
    nc = t // GLA_CHUNK
    kc = k.reshape(bsz, nc, GLA_CHUNK, h, dk).astype(jnp.float32)
    vc = v.reshape(bsz, nc, GLA_CHUNK, h, v.shape[-1]).astype(jnp.float32)
    cum = jnp.cumsum(g.reshape(bsz, nc, GLA_CHUNK, h, dk), axis=2)
    last = cum[:, :, -1]
    ds = jnp.einsum('bnchk,bnchv->bnhkv', kc * jnp.exp(last[:, :, None] - cum), vc)

    def step(s, inp):
        decay, d = inp
        return jnp.exp(decay)[..., None] * s + d, s

    s_final, s_prev = lax.scan(step, s0, (jnp.moveaxis(last, 1, 0), jnp.moveaxis(ds, 1, 0)))
    return cum, jnp.moveaxis(s_prev, 0, 1), s_final


def gla_chunk_outputs(q, k, v, cum, s_prev):
    bsz, t, h, dk = q.shape
    dv = v.shape[-1]
    nc = t // GLA_CHUNK
    qc = q.reshape(bsz, nc, GLA_CHUNK, h, dk).astype(jnp.float32)
    kc = k.reshape(bsz, nc, GLA_CHUNK, h, dk).astype(jnp.float32)
    vc = v.reshape(bsz, nc, GLA_CHUNK, h, dv).astype(jnp.float32)
    q_dec = qc * jnp.exp(cum)
    k_inv = kc * jnp.exp(-cum)
    o_inter = jnp.einsum('bnchk,bnhkv->bnchv', q_dec, s_prev)
    a = jnp.einsum('bnihk,bnjhk->bnhij', q_dec, k_inv)
    mask = jnp.tril(jnp.ones((GLA_CHUNK, GLA_CHUNK), dtype=bool))
    a = jnp.where(mask, a, 0.0)
    o_intra = jnp.einsum('bnhij,bnjhv->bnihv', a, vc)
    return (o_inter + o_intra).reshape(bsz, t, h, dv).astype(v.dtype)


def gla_direction(lat, cx, want_ctx):
    q, k, v, g = lat
    qc, kc, vc, gc = cx
    s0 = jnp.zeros((kc.shape[0], GLA_HEADS, GLA_DK, GLA_DV), jnp.float32)
    cum_c, sp_c, sf_c = gla_chunk_states(kc, vc, gc, s0)
    cum, sp, _ = gla_chunk_states(k, v, g, sf_c)
    o = gla_chunk_outputs(q, k, v, cum, sp)
    oc = gla_chunk_outputs(qc, kc, vc, cum_c, sp_c) if want_ctx else None
    return o, oc


def gla_inputs(parts, w_a_f, b_a_f, w_a_b, b_a_b):
    q, k, v, _, a = parts
    bsz, t = q.shape[:2]
    hk = (bsz, t, GLA_HEADS, GLA_DK)
    q = q.reshape(hk) * GLA_DK ** -0.5
    k = k.reshape(hk)
    v = v.reshape(bsz, t, GLA_HEADS, GLA_DV)
    a_f, a_b = jnp.split(a, 2, axis=-1)
    g_f = (jax.nn.log_sigmoid((a_f @ w_a_f + b_a_f).astype(jnp.float32)) / GLA_GATE_NORM).reshape(hk)
    g_b = (jax.nn.log_sigmoid((a_b @ w_a_b + b_a_b).astype(jnp.float32)) / GLA_GATE_NORM).reshape(hk)
    return (q, k, v, g_f), (q, k, v, g_b)


def flip_t(ts):
    return tuple(jnp.flip(t, axis=1) for t in ts)


def gla_branch(parts, parts_c, w_a_f, b_a_f, w_a_b, b_a_b, norm_w, want_ctx):
    lat_f, lat_b = gla_inputs(parts, w_a_f, b_a_f, w_a_b, b_a_b)
    cx_f, cx_b = gla_inputs(parts_c, w_a_f, b_a_f, w_a_b, b_a_b)
    o_f, oc_f = gla_direction(lat_f, cx_f, want_ctx)
    o_b, oc_b = gla_direction(flip_t(lat_b), flip_t(cx_b), want_ctx)

    def finish(o_fwd, o_bwd, z):
        o = rms_norm(o_fwd + jnp.flip(o_bwd, axis=1), norm_w)
        return o.reshape(z.shape) * jax.nn.silu(z)

    y = finish(o_f, o_b, parts[3])
    yc = finish(oc_f, oc_b, parts_c[3]) if want_ctx else None
    return y, yc


def mla_project(parts, q_norm_w, w_uq, kv_norm_w, w_ukv, cos, sin, rotate):
    cq, ckv, kr, _ = parts
    bsz, t = cq.shape[:2]
    q = (rms_norm(cq, q_norm_w) @ w_uq).reshape(bsz, t, MLA_HEADS, MLA_NOPE + MLA_ROPE)
    kv = (rms_norm(ckv, kv_norm_w) @ w_ukv).reshape(bsz, t, MLA_HEADS, MLA_NOPE + MLA_V)
    q_nope, q_rope = q[..., :MLA_NOPE], q[..., MLA_NOPE:]
    k_nope, v = kv[..., :MLA_NOPE], kv[..., MLA_NOPE:]
    k_rope = kr[:, :, None, :]
    if rotate:
        q_rope = apply_rope_2d(q_rope, cos, sin)
        k_rope = apply_rope_2d(k_rope, cos, sin)
    q = jnp.concatenate([q_nope, q_rope], axis=-1)
    k = jnp.concatenate([k_nope, jnp.broadcast_to(k_rope, (bsz, t, MLA_HEADS, MLA_ROPE))], axis=-1)
    return q, k, v


def mla_branch(parts, parts_c, q_norm_w, w_uq, kv_norm_w, w_ukv, cos, sin, want_ctx):
    scale = (MLA_NOPE + MLA_ROPE) ** -0.5
    q, k, v = mla_project(parts, q_norm_w, w_uq, kv_norm_w, w_ukv, cos, sin, True)
    qc, kc, vc = mla_project(parts_c, q_norm_w, w_uq, kv_norm_w, w_ukv, None, None, False)
    z, zc = parts[3], parts_c[3]
    y = latent_attention(q[:, :, :, None], k, v, kc, vc, scale).reshape(z.shape) * jax.nn.silu(z)
    yc = attend(qc[:, :, :, None], kc, vc, scale).reshape(zc.shape) * jax.nn.silu(zc) if want_ctx else None
    return y, yc


def gqa_project(parts, q_norm_w, k_norm_w, cos, sin, rotate):
    q, k, v, _ = parts
    bsz, t = q.shape[:2]
    q = rms_norm(q.reshape(bsz, t, GQA_HEADS, HEAD_DIM), q_norm_w)
    k = rms_norm(k.reshape(bsz, t, GQA_KV_HEADS, HEAD_DIM), k_norm_w)
    v = v.reshape(bsz, t, GQA_KV_HEADS, HEAD_DIM)
    if rotate:
        q = apply_rope_2d(q, cos, sin)
        k = apply_rope_2d(k, cos, sin)
    return q.reshape(bsz, t, GQA_KV_HEADS, GQA_GROUP, HEAD_DIM), k, v


def gqa_branch(parts, parts_c, q_norm_w, k_norm_w, cos, sin, want_ctx):
    scale = HEAD_DIM ** -0.5
    q, k, v = gqa_project(parts, q_norm_w, k_norm_w, cos, sin, True)
    qc, kc, vc = gqa_project(parts_c, q_norm_w, k_norm_w, None, None, False)
    z, zc = parts[3], parts_c[3]
    y = latent_attention(q, k, v, kc, vc, scale).reshape(z.shape) * jax.nn.silu(z)
    yc = attend(qc, kc, vc, scale).reshape(zc.shape) * jax.nn.silu(zc) if want_ctx else None
    return y, yc


def setup_inputs(seed: int = 0) -> dict:
    key = jax.random.key(seed)
    ks = jax.random.split(key, 24)
    f32 = jnp.float32
    L = DEPTH

    def nrm(k, shape, s):
        return jax.random.normal(k, shape, f32) * s

    def gain(k, shape):
        return 1.0 + 0.02 * jax.random.normal(k, shape, f32)

    return {
        'x': nrm(ks[0], (BATCH, SEQ, D_MODEL), 1.0),
        'c': nrm(ks[1], (BATCH, D_MODEL), 1.0),
        'ctx': nrm(ks[2], (BATCH, CTX_LEN, D_MODEL), 1.0),
        'c_ctx': nrm(ks[3], (D_MODEL,), 1.0),
        'norm_w': gain(ks[4], (L, D_MODEL)),
        'w_mod': nrm(ks[5], (L, D_MODEL, 3 * D_MODEL), D_MODEL ** -0.5),
        'b_mod': nrm(ks[6], (L, 3 * D_MODEL), 0.01),
        'w_in': nrm(ks[7], (L, D_MODEL, D_IN), D_MODEL ** -0.5),
        'gla_w_a_fwd': nrm(ks[8], (L, GLA_GATE_RANK, GLA_HEADS * GLA_DK), GLA_GATE_RANK ** -0.5),
        'gla_b_a_fwd': nrm(ks[9], (L, GLA_HEADS * GLA_DK), 0.1),
        'gla_w_a_bwd': nrm(ks[10], (L, GLA_GATE_RANK, GLA_HEADS * GLA_DK), GLA_GATE_RANK ** -0.5),
        'gla_b_a_bwd': nrm(ks[11], (L, GLA_HEADS * GLA_DK), 0.1),
        'gla_norm_w': gain(ks[12], (L, GLA_DV)),
        'mla_q_norm_w': gain(ks[13], (L, MLA_Q_LORA)),
        'mla_w_uq': nrm(ks[14], (L, MLA_Q_LORA, MLA_HEADS * (MLA_NOPE + MLA_ROPE)), MLA_Q_LORA ** -0.5),
        'mla_kv_norm_w': gain(ks[15], (L, MLA_KV_LORA)),
        'mla_w_ukv': nrm(ks[16], (L, MLA_KV_LORA, MLA_HEADS * (MLA_NOPE + MLA_V)), MLA_KV_LORA ** -0.5),
        'gqa_q_norm_w': gain(ks[17], (L, HEAD_DIM)),
        'gqa_k_norm_w': gain(ks[18], (L, HEAD_DIM)),
        'w_out': nrm(ks[19], (L, D_MIX, D_MODEL), D_MIX ** -0.5),
        'final_norm_w': gain(ks[20], (D_MODEL,)),
    }


def reference(x, c, ctx, c_ctx, norm_w, w_mod, b_mod, w_in, gla_w_a_fwd, gla_b_a_fwd, gla_w_a_bwd,
              gla_b_a_bwd, gla_norm_w, mla_q_norm_w, mla_w_uq, mla_kv_norm_w, mla_w_ukv,
              gqa_q_norm_w, gqa_k_norm_w, w_out, final_norm_w):
    n = x.shape[1]
    rows = n // GRID_W
    cos_m, sin_m = rope_2d_tables(rows, MLA_ROPE)
    cos_g, sin_g = rope_2d_tables(rows, HEAD_DIM)
    n_a = len(GLA_COLS)
    n_b = n_a + len(MLA_COLS)
    for l in range(DEPTH):
        want_ctx = l < DEPTH - 1
        shift, scale, gate = jnp.split(jax.nn.silu(c) @ w_mod[l] + b_mod[l], 3, axis=-1)
        shift_c, scale_c, gate_c = jnp.split(jax.nn.silu(c_ctx) @ w_mod[l] + b_mod[l], 3, axis=-1)
        h = rms_norm(x, norm_w[l]) * (1.0 + scale[:, None, :]) + shift[:, None, :]
        hc = rms_norm(ctx, norm_w[l]) * (1.0 + scale_c) + shift_c
        u = split_cols(h @ w_in[l], IN_COLS)
        uc = split_cols(hc @ w_in[l], IN_COLS)
        y_a, yc_a = gla_branch(u[:n_a], uc[:n_a], gla_w_a_fwd[l], gla_b_a_fwd[l], gla_w_a_bwd[l],
                               gla_b_a_bwd[l], gla_norm_w[l], want_ctx)
        y_b, yc_b = mla_branch(u[n_a:n_b], uc[n_a:n_b], mla_q_norm_w[l], mla_w_uq[l], mla_kv_norm_w[l],
                               mla_w_ukv[l], cos_m, sin_m, want_ctx)
        y_c, yc_c = gqa_branch(u[n_b:], uc[n_b:], gqa_q_norm_w[l], gqa_k_norm_w[l], cos_g, sin_g, want_ctx)
        x = x + gate[:, None, :] * (jnp.concatenate([y_a, y_b, y_c], axis=-1) @ w_out[l])
        if want_ctx:
            ctx = ctx + gate_c * (jnp.concatenate([yc_a, yc_b, yc_c], axis=-1) @ w_out[l])
    return rms_norm(x, final_norm_w)
```

```python
import functools

import numpy as np
import jax
import jax.numpy as jnp
from jax import lax
from jax.experimental import pallas as pl
from jax.experimental.pallas import tpu as pltpu

D_MODEL = 1024
GRID_W = 64
CTX_LEN = 256
HEAD_DIM = 64
ROPE_THETA = 10000.0
EPS = 1e-6

GLA_HEADS = 4
GLA_DK = 32
GLA_DV = 64
GLA_WIDTH = GLA_HEADS * GLA_DV
GLA_QK = GLA_HEADS * GLA_DK
GLA_GATE_RANK = 16
GLA_GATE_NORM = 16.0
GLA_CHUNK = 64

MLA_HEADS = 6
MLA_NOPE = 64
MLA_ROPE = 32
MLA_V = 64
MLA_Q_LORA = 256
MLA_KV_LORA = 256
MLA_WIDTH = MLA_HEADS * MLA_V

GQA_HEADS = 6
GQA_KV_HEADS = 2
GQA_GROUP = GQA_HEADS // GQA_KV_HEADS
GQA_WIDTH = GQA_HEADS * HEAD_DIM

D_MIX = GLA_WIDTH + MLA_WIDTH + GQA_WIDTH

LANES = 128
TM = CTX_LEN
CHUNKS = TM // GLA_CHUNK
SEG = 640
W_IN_PAD = 3 * SEG + D_MIX
VMEM_LIMIT = 48 * 1024 * 1024

F32 = jnp.float32
BF16 = jnp.bfloat16


def _cparams(sem):
    return pltpu.CompilerParams(dimension_semantics=sem, vmem_limit_bytes=VMEM_LIMIT)


def _dot(a, b):
    return jnp.dot(a, b, preferred_element_type=F32)


def _dot_nt(a, b):
    return lax.dot_general(a, b, (((1,), (1,)), ((), ())), preferred_element_type=F32)


def _dot_tn(a, b):
    return lax.dot_general(a, b, (((0,), (0,)), ((), ())), preferred_element_type=F32)


def _silu(v):
    return v / (1.0 + jnp.exp(-v))


def _mod_kernel(c_ref, w_ref, b_ref, o_ref):
    s = _silu(c_ref[...]).astype(BF16)
    o_ref[...] = _dot(s, w_ref[...].astype(BF16)) + b_ref[...]


def _modulation(cs, w_mod, b_mod):
    depth = w_mod.shape[0]
    rows = cs.shape[0]
    return pl.pallas_call(
        _mod_kernel,
        grid=(depth, 3),
        in_specs=[
            pl.BlockSpec((rows, D_MODEL), lambda l, j: (0, 0)),
            pl.BlockSpec((None, D_MODEL, D_MODEL), lambda l, j: (l, 0, j)),
            pl.BlockSpec((None, 1, D_MODEL), lambda l, j: (l, 0, j)),
        ],
        out_specs=pl.BlockSpec((None, rows, D_MODEL), lambda l, j: (l, 0, j)),
        out_shape=jax.ShapeDtypeStruct((depth, rows, 3 * D_MODEL), F32),
        compiler_params=_cparams(("arbitrary", "arbitrary")),
        name="modulation",
    )(cs, w_mod, b_mod.reshape(depth, 1, 3 * D_MODEL))


def _rope(t, tab, shift):
    c, sa, sb = tab[:, :LANES], tab[:, LANES:2 * LANES], tab[:, 2 * LANES:]
    return t * c + pltpu.roll(t, LANES - shift, 1) * sa + pltpu.roll(t, shift, 1) * sb


def _prep_kernel(x_ref, mod_ref, nw_ref, win_ref, wa_ref, ba_ref, mqn_ref, wuq_ref, mkvn_ref,
                 wuk_ref, wuv_ref, gqn_ref, gkn_ref, rm_ref, rg_ref,
                 gq_ref, gk_ref, gv_ref, gg_ref, zg_ref, mq_ref, mk_ref, mvt_ref, cq_ref, ck_ref,
                 cvt_ref):
    x = x_ref[...]
    mod = mod_ref[...]
    shift, scale = mod[:, :D_MODEL], mod[:, D_MODEL:2 * D_MODEL]
    ms = jnp.mean(x * x, axis=-1, keepdims=True)
    h = (x * lax.rsqrt(ms + EPS) * nw_ref[...]) * (1.0 + scale) + shift
    hb = h.astype(BF16)

    uz = _dot(hb, win_ref[:, 3 * SEG:])
    zg_ref[...] = _silu(uz)

    ua = _dot(hb, win_ref[:, 0:SEG])
    gq_ref[...] = (ua[:, 0:GLA_QK] * GLA_DK ** -0.5).astype(BF16)
    gk_ref[...] = ua[:, GLA_QK:2 * GLA_QK].astype(BF16)
    gv_ref[...] = ua[:, 2 * GLA_QK:2 * GLA_QK + GLA_WIDTH].astype(BF16)
    a = ua[:, 2 * GLA_QK + GLA_WIDTH:SEG].astype(BF16)
    xg = _dot(a, wa_ref[...]) + ba_ref[...]
    log_sig = jnp.minimum(xg, 0.0) - jnp.log1p(jnp.exp(-jnp.abs(xg)))
    gg_ref[...] = log_sig * (1.0 / GLA_GATE_NORM)

    ub = _dot(hb, win_ref[:, SEG:2 * SEG])
    cq = ub[:, 0:MLA_Q_LORA]
    ckv = ub[:, MLA_Q_LORA:MLA_Q_LORA + MLA_KV_LORA]
    kr = ub[:, MLA_Q_LORA + MLA_KV_LORA:SEG]
    cqn = cq * lax.rsqrt(jnp.mean(cq * cq, axis=-1, keepdims=True) + EPS) * mqn_ref[...]
    ckvn = ckv * lax.rsqrt(jnp.mean(ckv * ckv, axis=-1, keepdims=True) + EPS) * mkvn_ref[...]
    qm = _dot(cqn.astype(BF16), wuq_ref[...])
    ckvb = ckvn.astype(BF16)
    kn = _dot(ckvb, wuk_ref[...])
    vm = _dot(ckvb, wuv_ref[...])
    rm = rm_ref[...]
    krr = _rope(kr, rm, MLA_ROPE // 4)
    mla_scale = (MLA_NOPE + MLA_ROPE) ** -0.5
    for hd in range(MLA_HEADS):
        sl = slice(hd * LANES, (hd + 1) * LANES)
        mq_ref[:, sl] = (_rope(qm[:, sl], rm, MLA_ROPE // 4) * mla_scale).astype(BF16)
        mk_ref[:, sl] = (kn[:, sl] + krr).astype(BF16)
    for p in range(MLA_HEADS // 2):
        mvt_ref[p] = vm[:, p * LANES:(p + 1) * LANES].T.astype(BF16)

    uc = _dot(hb, win_ref[:, 2 * SEG:3 * SEG])
    rg = rg_ref[...]
    lo = lax.broadcasted_iota(jnp.int32, (TM, LANES), 1) < HEAD_DIM
    gqn = gqn_ref[...]
    for hd in range(GQA_HEADS):
        blk = uc[:, (hd // 2) * LANES:(hd // 2 + 1) * LANES]
        if hd % 2 == 1:
            blk = pltpu.roll(blk, HEAD_DIM, 1)
        xh = jnp.where(lo, blk, 0.0)
        msq = jnp.sum(xh * xh, axis=-1, keepdims=True) * (1.0 / HEAD_DIM)
        xh = xh * lax.rsqrt(msq + EPS) * gqn
        xh = _rope(xh, rg, HEAD_DIM // 4) * HEAD_DIM ** -0.5
        if hd // GQA_GROUP == 1:
            xh = pltpu.roll(xh, HEAD_DIM, 1)
        cq_ref[:, hd * LANES:(hd + 1) * LANES] = xh.astype(BF16)
    k = uc[:, GQA_WIDTH:GQA_WIDTH + LANES]
    k2 = k * k
    ms_lo = jnp.sum(jnp.where(lo, k2, 0.0), axis=-1, keepdims=True) * (1.0 / HEAD_DIM)
    ms_hi = jnp.sum(jnp.where(lo, 0.0, k2), axis=-1, keepdims=True) * (1.0 / HEAD_DIM)
    inv = jnp.where(lo, lax.rsqrt(ms_lo + EPS), lax.rsqrt(ms_hi + EPS))
    kg = k * inv * gkn_ref[...]
    rg2 = jnp.concatenate([rg[:, j * LANES:(j + 1) * LANES]
                           + pltpu.roll(rg[:, j * LANES:(j + 1) * LANES], HEAD_DIM, 1)
                           for j in range(3)], axis=1)
    ck_ref[...] = _rope(kg, rg2, HEAD_DIM // 4).astype(BF16)
    cvt_ref[...] = uc[:, GQA_WIDTH + LANES:SEG].T.astype(BF16)


def _prep(xs, mod, lw, rope_m, rope_g):
    bsz, t, _ = xs.shape
    nt = t // TM
    tok = lambda w: pl.BlockSpec((None, TM, w), lambda b, i: (b, i, 0))
    full = lambda a: pl.BlockSpec(a.shape, lambda b, i: (0,) * a.ndim)
    tab = pl.BlockSpec((TM, 3 * LANES), lambda b, i: (i, 0))
    weights = [lw[n] for n in ("norm_w", "w_in", "w_a", "b_a", "mla_q_norm", "w_uq", "mla_kv_norm",
                               "w_uk", "w_uv", "gqa_q_norm", "gqa_k_norm")]
    out_shape = [
        jax.ShapeDtypeStruct((bsz, t, GLA_QK), BF16),
        jax.ShapeDtypeStruct((bsz, t, GLA_QK), BF16),
        jax.ShapeDtypeStruct((bsz, t, GLA_WIDTH), BF16),
        jax.ShapeDtypeStruct((bsz, t, 2 * GLA_QK), F32),
        jax.ShapeDtypeStruct((bsz, t, D_MIX), F32),
        jax.ShapeDtypeStruct((bsz, t, MLA_HEADS * LANES), BF16),
        jax.ShapeDtypeStruct((bsz, t, MLA_HEADS * LANES), BF16),
        jax.ShapeDtypeStruct((bsz, MLA_HEADS // 2, LANES, t), BF16),
        jax.ShapeDtypeStruct((bsz, t, GQA_HEADS * LANES), BF16),
        jax.ShapeDtypeStruct((bsz, t, LANES), BF16),
        jax.ShapeDtypeStruct((bsz, LANES, t), BF16),
    ]
    out_specs = [
        tok(GLA_QK), tok(GLA_QK), tok(GLA_WIDTH), tok(2 * GLA_QK), tok(D_MIX),
        tok(MLA_HEADS * LANES), tok(MLA_HEADS * LANES),
        pl.BlockSpec((None, MLA_HEADS // 2, LANES, TM), lambda b, i: (b, 0, 0, i)),
        tok(GQA_HEADS * LANES), tok(LANES),
        pl.BlockSpec((None, LANES, TM), lambda b, i: (b, 0, i)),
    ]
    return pl.pallas_call(
        _prep_kernel,
        grid=(bsz, nt),
        in_specs=[tok(D_MODEL),
                  pl.BlockSpec((None, 1, 3 * D_MODEL), lambda b, i: (2 * b + jnp.minimum(i, 1), 0, 0))]
                 + [full(w) for w in weights] + [tab, tab],
        out_specs=out_specs,
        out_shape=out_shape,
        compiler_params=_cparams(("parallel", "arbitrary")),
        name="prep",
    )(xs, mod, *weights, rope_m, rope_g)


def _gla_direction(q_ref, k_ref, v_ref, g_ref, l_ref, cmask_ref, qmask, omask, smask, s_ref, o_ref,
                   reverse):
    q = q_ref[...].astype(F32)
    k = k_ref[...].astype(F32)
    g = g_ref[...]
    g_hi = g.astype(BF16)
    r1 = g - g_hi.astype(F32)
    g_mid = r1.astype(BF16)
    g_lo = (r1 - g_mid.astype(F32)).astype(BF16)
    tri = l_ref[...]
    cum = _dot(tri, g_hi) + _dot(tri, g_mid) + _dot(tri, g_lo)
    cmask = cmask_ref[...]
    order = range(CHUNKS - 1, -1, -1) if reverse else range(CHUNKS)
    for c in order:
        sl = slice(c * GLA_CHUNK, (c + 1) * GLA_CHUNK)
        cum_c = cum[sl]
        last = cum_c[0:1] if reverse else cum_c[GLA_CHUNK - 1:GLA_CHUNK]
        qd = q[sl] * jnp.exp(cum_c)
        ki = k[sl] * jnp.exp(-cum_c)
        kd = k[sl] * jnp.exp(last - cum_c)
        vc = v_ref[sl, :]
        qh = jnp.concatenate([qd] * GLA_HEADS, axis=0) * qmask
        att = _dot_nt(qh.astype(BF16), ki.astype(BF16)) * cmask
        o_all = _dot(att.astype(BF16), vc) * omask
        o_intra = o_all[0:GLA_CHUNK]
        for hd in range(1, GLA_HEADS):
            o_intra = o_intra + o_all[hd * GLA_CHUNK:(hd + 1) * GLA_CHUNK]
        st = s_ref[...]
        o_inter = _dot_nt(qd.astype(BF16), st.astype(BF16))
        o_ref[sl, :] = o_inter + o_intra
        ds = _dot_tn(vc, kd.astype(BF16)) * smask
        s_ref[...] = st * jnp.exp(last) + ds


def _gla_kernel(qf_ref, kf_ref, vf_ref, gf_ref, qb_ref, kb_ref, vb_ref, gb_ref, lf_ref, lb_ref,
                cmf_ref, cmb_ref, qmask_ref, omask_ref, smask_ref, of_ref, ob_ref, sf_ref, sb_ref):
    @pl.when(pl.program_id(1) == 0)
    def _():
        sf_ref[...] = jnp.zeros_like(sf_ref)
        sb_ref[...] = jnp.zeros_like(sb_ref)

    qmask, omask, smask = qmask_ref[...], omask_ref[...], smask_ref[...]
    _gla_direction(qf_ref, kf_ref, vf_ref, gf_ref, lf_ref, cmf_ref, qmask, omask, smask, sf_ref,
                   of_ref, False)
    _gla_direction(qb_ref, kb_ref, vb_ref, gb_ref, lb_ref, cmb_ref, qmask, omask, smask, sb_ref,
                   ob_ref, True)


def _gla_consts():
    r = np.arange(TM)
    same = (r[:, None] // GLA_CHUNK) == (r[None, :] // GLA_CHUNK)
    lf = (same & (r[None, :] <= r[:, None])).astype(np.float32)
    lb = (same & (r[None, :] >= r[:, None])).astype(np.float32)
    rows = np.arange(GLA_HEADS * GLA_CHUNK)
    j = np.arange(GLA_CHUNK)
    cmf = ((rows[:, None] % GLA_CHUNK) >= j[None, :]).astype(np.float32)
    cmb = ((rows[:, None] % GLA_CHUNK) <= j[None, :]).astype(np.float32)
    qmask = ((rows[:, None] // GLA_CHUNK) == (np.arange(GLA_QK)[None, :] // GLA_DK)).astype(np.float32)
    omask = ((rows[:, None] // GLA_CHUNK) == (np.arange(GLA_WIDTH)[None, :] // GLA_DV)).astype(np.float32)
    smask = ((np.arange(GLA_WIDTH)[:, None] // GLA_DV) == (np.arange(GLA_QK)[None, :] // GLA_DK)).astype(np.float32)
    return (jnp.asarray(lf, BF16), jnp.asarray(lb, BF16), jnp.asarray(cmf), jnp.asarray(cmb),
            jnp.asarray(qmask), jnp.asarray(omask), jnp.asarray(smask))


def _gla(gq, gk, gv, gg):
    bsz, t, _ = gq.shape
    nt = t // TM
    consts = _gla_consts()
    fwd = lambda b, i: (b, i, 0)
    rev_blk = lambda i: jnp.where(i == 0, 0, nt - i)
    bwd = lambda b, i: (b, rev_blk(i), 0)
    bwd_g = lambda b, i: (b, rev_blk(i), 1)
    spec = lambda w, im: pl.BlockSpec((None, TM, w), im)
    full = lambda a: pl.BlockSpec(a.shape, lambda b, i: (0,) * a.ndim)
    return pl.pallas_call(
        _gla_kernel,
        grid=(bsz, nt),
        in_specs=[spec(GLA_QK, fwd), spec(GLA_QK, fwd), spec(GLA_WIDTH, fwd), spec(GLA_QK, fwd),
                  spec(GLA_QK, bwd), spec(GLA_QK, bwd), spec(GLA_WIDTH, bwd), spec(GLA_QK, bwd_g)]
                 + [full(a) for a in consts],
        out_specs=[spec(GLA_WIDTH, fwd), spec(GLA_WIDTH, bwd)],
        out_shape=[jax.ShapeDtypeStruct((bsz, t, GLA_WIDTH), F32)] * 2,
        scratch_shapes=[pltpu.VMEM((GLA_WIDTH, GLA_QK), F32)] * 2,
        compiler_params=_cparams(("parallel", "arbitrary")),
        name="gla",
    )(gq, gk, gv, gg, gq, gk, gv, gg, *consts)


def _attn_kernel(q_ref, k_ref, vt_ref, o_ref, *, shared_kv):
    pair = pl.program_id(1)
    lo = lax.broadcasted_iota(jnp.int32, (TM, LANES), 1) < HEAD_DIM

    def body(nk):
        outs = []
        for e in range(2):
            qh = q_ref[:, e * LANES:(e + 1) * LANES]
            kh = k_ref[0:nk, :] if shared_kv else k_ref[0:nk, e * LANES:(e + 1) * LANES]
            st = _dot_nt(kh, qh)
            m = jnp.max(st, axis=0, keepdims=True)
            p = jnp.exp(st - m)
            l = jnp.sum(p, axis=0, keepdims=True)
            ot = _dot(vt_ref[:, 0:nk], p.astype(BF16)) / l
            outs.append(ot.T)
        o0, o1 = outs
        if shared_kv:
            o0 = jnp.where(pair == 2, pltpu.roll(o0, HEAD_DIM, 1), o0)
            o1 = jnp.where(pair == 0, pltpu.roll(o1, HEAD_DIM, 1), o1)
        o_ref[...] = jnp.where(lo, o0, o1)

    @pl.when(pl.program_id(2) == 0)
    def _():
        body(CTX_LEN)

    @pl.when(pl.program_id(2) > 0)
    def _():
        body(k_ref.shape[0])


def _attention(q, k, vt, shared_kv):
    bsz, t, _ = q.shape
    nt = t // TM
    if shared_kv:
        k_spec = pl.BlockSpec((None, t, LANES), lambda b, p, i: (b, 0, 0))
        v_spec = pl.BlockSpec((None, LANES, t), lambda b, p, i: (b, 0, 0))
    else:
        k_spec = pl.BlockSpec((None, t, 2 * LANES), lambda b, p, i: (b, 0, p))
        v_spec = pl.BlockSpec((None, None, LANES, t), lambda b, p, i: (b, p, 0, 0))
    return pl.pallas_call(
        functools.partial(_attn_kernel, shared_kv=shared_kv),
        grid=(bsz, 3, nt),
        in_specs=[pl.BlockSpec((None, TM, 2 * LANES), lambda b, p, i: (b, i, p)), k_spec, v_spec],
        out_specs=pl.BlockSpec((None, TM, LANES), lambda b, p, i: (b, i, p)),
        out_shape=jax.ShapeDtypeStruct((bsz, t, 3 * LANES), F32),
        compiler_params=_cparams(("parallel", "arbitrary", "arbitrary")),
        name="attn_gqa" if shared_kv else "attn_mla",
    )(q, k, vt)


def _outproj_kernel(x_ref, of_ref, ob_ref, yb_ref, yc_ref, zg_ref, gnw_ref, mod_ref, wout_ref,
                    fnw_ref, o_ref, *, final):
    lo = lax.broadcasted_iota(jnp.int32, (TM, LANES), 1) < GLA_DV
    o = of_ref[...] + ob_ref[...]
    gnw = gnw_ref[...]
    zg = zg_ref[...]
    parts = []
    for j in range(GLA_WIDTH // LANES):
        sl = slice(j * LANES, (j + 1) * LANES)
        oj = o[:, sl]
        o2 = oj * oj
        ms_lo = jnp.sum(jnp.where(lo, o2, 0.0), axis=-1, keepdims=True) * (1.0 / GLA_DV)
        ms_hi = jnp.sum(jnp.where(lo, 0.0, o2), axis=-1, keepdims=True) * (1.0 / GLA_DV)
        inv = jnp.where(lo, lax.rsqrt(ms_lo + EPS), lax.rsqrt(ms_hi + EPS))
        parts.append(oj * inv * gnw[:, sl] * zg[:, sl])
    parts.append(yb_ref[...] * zg[:, GLA_WIDTH:GLA_WIDTH + MLA_WIDTH])
    parts.append(yc_ref[...] * zg[:, GLA_WIDTH + MLA_WIDTH:])
    y = jnp.concatenate(parts, axis=-1).astype(BF16)
    gate = mod_ref[...]
    xn = x_ref[...] + gate * _dot(y, wout_ref[...])
    if final:
        xn = xn * lax.rsqrt(jnp.mean(xn * xn, axis=-1, keepdims=True) + EPS) * fnw_ref[...]
    o_ref[...] = xn


def _outproj(xs, o_f, o_b, y_b, y_c, zg, mod, lw, final_norm_w, final):
    bsz, t, _ = xs.shape
    off = 1 if final else 0
    nt = t // TM - off
    tok = lambda w: pl.BlockSpec((None, TM, w), lambda b, i: (b, i + off, 0))
    full = lambda a: pl.BlockSpec(a.shape, lambda b, i: (0,) * a.ndim)
    return pl.pallas_call(
        functools.partial(_outproj_kernel, final=final),
        grid=(bsz, nt),
        in_specs=[tok(D_MODEL), tok(GLA_WIDTH), tok(GLA_WIDTH), tok(MLA_WIDTH), tok(GQA_WIDTH),
                  tok(D_MIX), full(lw["gla_norm"]),
                  pl.BlockSpec((None, 1, D_MODEL),
                               lambda b, i: (2 * b + jnp.minimum(i + off, 1), 0, 2)),
                  full(lw["w_out"]), full(final_norm_w)],
        out_specs=pl.BlockSpec((None, TM, D_MODEL), lambda b, i: (b, i, 0)),
        out_shape=jax.ShapeDtypeStruct((bsz, nt * TM, D_MODEL), F32),
        compiler_params=_cparams(("parallel", "arbitrary")),
        name="outproj",
    )(xs, o_f, o_b, y_b, y_c, zg, lw["gla_norm"], mod, lw["w_out"], final_norm_w)


def _rope_tables(rows, d_rot, lane0):
    quarter = d_rot // 4
    n = rows * GRID_W
    row = jnp.repeat(jnp.arange(rows), GRID_W).astype(F32)
    col = jnp.tile(jnp.arange(GRID_W), rows).astype(F32)
    freqs = ROPE_THETA ** (-jnp.arange(quarter, dtype=F32) / quarter)
    ang = jnp.stack([row[:, None] * freqs, col[:, None] * freqs], axis=1)
    cos, sin = jnp.cos(ang), jnp.sin(ang)
    zero = jnp.zeros_like(sin)
    cos_l = jnp.stack([cos, cos], axis=2).reshape(n, d_rot)
    sa_l = jnp.stack([-sin, zero], axis=2).reshape(n, d_rot)
    sb_l = jnp.stack([zero, sin], axis=2).reshape(n, d_rot)

    def place(v, fill):
        out = jnp.full((n, LANES), fill, F32)
        return out.at[:, lane0:lane0 + d_rot].set(v)

    c_fill = 1.0 if lane0 > 0 else 0.0
    lat = jnp.concatenate([place(cos_l, c_fill), place(sa_l, 0.0), place(sb_l, 0.0)], axis=1)
    ident = jnp.concatenate([place(jnp.ones((n, d_rot), F32), c_fill)[:CTX_LEN],
                             jnp.zeros((CTX_LEN, 2 * LANES), F32)], axis=1)
    return jnp.concatenate([ident, lat], axis=0)


def _layer_weights(l, norm_w, w_in, gla_w_a_fwd, gla_b_a_fwd, gla_w_a_bwd, gla_b_a_bwd, gla_norm_w,
                   mla_q_norm_w, mla_w_uq, mla_kv_norm_w, mla_w_ukv, gqa_q_norm_w, gqa_k_norm_w,
                   w_out):
    w = w_in[l]
    zc = lambda n: jnp.zeros((D_MODEL, n), F32)
    c0 = 2 * GLA_QK + 2 * GLA_WIDTH
    m0 = c0 + 2 * GLA_GATE_RANK
    g0 = m0 + MLA_Q_LORA + MLA_KV_LORA + MLA_ROPE + MLA_WIDTH
    seg_a = [w[:, 0:2 * GLA_QK + GLA_WIDTH], w[:, c0:m0], zc(LANES - 2 * GLA_GATE_RANK)]
    kr0 = m0 + MLA_Q_LORA + MLA_KV_LORA
    seg_b = [w[:, m0:kr0], zc(MLA_NOPE), w[:, kr0:kr0 + MLA_ROPE], zc(LANES - MLA_NOPE - MLA_ROPE)]
    seg_c = [w[:, g0:g0 + GQA_WIDTH + 2 * LANES]]
    seg_z = [w[:, 2 * GLA_QK + GLA_WIDTH:c0], w[:, kr0 + MLA_ROPE:g0], w[:, g0 + GQA_WIDTH + 2 * LANES:]]
    w_pad = jnp.concatenate(seg_a + seg_b + seg_c + seg_z, axis=1).astype(BF16)

    w_a = jnp.zeros((LANES, 2 * GLA_QK), F32)
    w_a = w_a.at[0:GLA_GATE_RANK, 0:GLA_QK].set(gla_w_a_fwd[l])
    w_a = w_a.at[GLA_GATE_RANK:2 * GLA_GATE_RANK, GLA_QK:].set(gla_w_a_bwd[l])
    b_a = jnp.concatenate([gla_b_a_fwd[l], gla_b_a_bwd[l]])[None, :]

    dq = MLA_NOPE + MLA_ROPE
    uq = mla_w_uq[l].reshape(MLA_Q_LORA, MLA_HEADS, dq)
    uq = jnp.pad(uq, ((0, 0), (0, 0), (0, LANES - dq))).reshape(MLA_Q_LORA, MLA_HEADS * LANES)
    ukv = mla_w_ukv[l].reshape(MLA_KV_LORA, MLA_HEADS, MLA_NOPE + MLA_V)
    uk = jnp.pad(ukv[:, :, :MLA_NOPE], ((0, 0), (0, 0), (0, LANES - MLA_NOPE)))
    uk = uk.reshape(MLA_KV_LORA, MLA_HEADS * LANES)
    uv = ukv[:, :, MLA_NOPE:].reshape(MLA_KV_LORA, MLA_WIDTH)

    pad_lane = lambda v: jnp.concatenate([v, jnp.zeros((LANES - v.shape[0],), F32)])[None, :]
    return {
        "norm_w": norm_w[l][None, :],
        "w_in": w_pad,
        "w_a": w_a.astype(BF16),
        "b_a": b_a,
        "mla_q_norm": mla_q_norm_w[l][None, :],
        "w_uq": uq.astype(BF16),
        "mla_kv_norm": mla_kv_norm_w[l][None, :],
        "w_uk": uk.astype(BF16),
        "w_uv": uv.astype(BF16),
        "gqa_q_norm": pad_lane(gqa_q_norm_w[l]),
        "gqa_k_norm": jnp.tile(gqa_k_norm_w[l], GQA_KV_HEADS)[None, :],
        "gla_norm": jnp.tile(gla_norm_w[l], GLA_HEADS)[None, :],
        "w_out": w_out[l].astype(BF16),
    }


def kernel(x, c, ctx, c_ctx, norm_w, w_mod, b_mod, w_in, gla_w_a_fwd, gla_b_a_fwd, gla_w_a_bwd,
           gla_b_a_bwd, gla_norm_w, mla_q_norm_w, mla_w_uq, mla_kv_norm_w, mla_w_ukv,
           gqa_q_norm_w, gqa_k_norm_w, w_out, final_norm_w):
    bsz, n, _ = x.shape
    depth = w_in.shape[0]
    assert ctx.shape[1] == CTX_LEN and n % TM == 0 and n % GRID_W == 0
    rows = n // GRID_W
    rope_m = _rope_tables(rows, MLA_ROPE, MLA_NOPE)
    rope_g = _rope_tables(rows, HEAD_DIM, 0)

    mod_rows = -(-(bsz + 1) // 8) * 8
    cs = jnp.zeros((mod_rows, D_MODEL), F32).at[:bsz].set(c).at[bsz].set(c_ctx)
    mod_all = _modulation(cs, w_mod, b_mod)
    fnw = final_norm_w[None, :]

    xs = jnp.concatenate([ctx, x], axis=1)
    for l in range(depth):
        lw = _layer_weights(l, norm_w, w_in, gla_w_a_fwd, gla_b_a_fwd, gla_w_a_bwd, gla_b_a_bwd,
                            gla_norm_w, mla_q_norm_w, mla_w_uq, mla_kv_norm_w, mla_w_ukv,
                            gqa_q_norm_w, gqa_k_norm_w, w_out)
        m = mod_all[l]
        mod = jnp.stack([jnp.broadcast_to(m[bsz], (bsz, 3 * D_MODEL)), m[:bsz]], axis=1)
        mod = mod.reshape(2 * bsz, 1, 3 * D_MODEL)
        gq, gk, gv, gg, zg, mq, mk, mvt, cq, ck, cvt = _prep(xs, mod, lw, rope_m, rope_g)
        o_f, o_b = _gla(gq, gk, gv, gg)
        y_b = _attention(mq, mk, mvt, shared_kv=False)
        y_c = _attention(cq, ck, cvt, shared_kv=True)
        xs = _outproj(xs, o_f, o_b, y_b, y_c, zg, mod, lw, fnw, final=(l == depth - 1))
    return xs
```

```python
import functools

import numpy as np
import jax
import jax.numpy as jnp
from jax import lax
from jax.experimental import pallas as pl
from jax.experimental.pallas import tpu as pltpu

D_MODEL = 1024
GRID_W = 64
CTX_LEN = 256
HEAD_DIM = 64
ROPE_THETA = 10000.0
EPS = 1e-6

GLA_HEADS = 4
GLA_DK = 32
GLA_DV = 64
GLA_WIDTH = GLA_HEADS * GLA_DV
GLA_QK = GLA_HEADS * GLA_DK
GLA_GATE_RANK = 16
GLA_GATE_NORM = 16.0
GLA_CHUNK = 64

MLA_HEADS = 6
MLA_NOPE = 64
MLA_ROPE = 32
MLA_V = 64
MLA_Q_LORA = 256
MLA_KV_LORA = 256
MLA_WIDTH = MLA_HEADS * MLA_V

GQA_HEADS = 6
GQA_KV_HEADS = 2
GQA_GROUP = GQA_HEADS // GQA_KV_HEADS
GQA_WIDTH = GQA_HEADS * HEAD_DIM

D_MIX = GLA_WIDTH + MLA_WIDTH + GQA_WIDTH

LANES = 128
TM = CTX_LEN
CHUNKS = TM // GLA_CHUNK
SEG = 640
W_IN_PAD = 3 * SEG + D_MIX
VMEM_LIMIT = 48 * 1024 * 1024
ATT_CK = 256
LOG2E = 1.4426950408889634

F32 = jnp.float32
BF16 = jnp.bfloat16


def _cparams(sem):
    return pltpu.CompilerParams(dimension_semantics=sem, vmem_limit_bytes=VMEM_LIMIT)


def _dot(a, b):
    return jnp.dot(a, b, preferred_element_type=F32)


def _dot_nt(a, b):
    return lax.dot_general(a, b, (((1,), (1,)), ((), ())), preferred_element_type=F32)


def _dot_tn(a, b):
    return lax.dot_general(a, b, (((0,), (0,)), ((), ())), preferred_element_type=F32)


def _silu(v):
    return v / (1.0 + jnp.exp(-v))


def _mod_kernel(c_ref, w_ref, b_ref, o_ref):
    s = _silu(c_ref[...]).astype(BF16)
    o_ref[...] = _dot(s, w_ref[...].astype(BF16)) + b_ref[...]


def _modulation(cs, w_mod, b_mod):
    depth = w_mod.shape[0]
    rows = cs.shape[0]
    return pl.pallas_call(
        _mod_kernel,
        grid=(depth, 3),
        in_specs=[
            pl.BlockSpec((rows, D_MODEL), lambda l, j: (0, 0)),
            pl.BlockSpec((None, D_MODEL, D_MODEL), lambda l, j: (l, 0, j)),
            pl.BlockSpec((None, 1, D_MODEL), lambda l, j: (l, 0, j)),
        ],
        out_specs=pl.BlockSpec((None, rows, D_MODEL), lambda l, j: (l, 0, j)),
        out_shape=jax.ShapeDtypeStruct((depth, rows, 3 * D_MODEL), F32),
        compiler_params=_cparams(("arbitrary", "arbitrary")),
        name="modulation",
    )(cs, w_mod, b_mod.reshape(depth, 1, 3 * D_MODEL))


def _rope(t, tab, shift):
    c, sa, sb = tab[:, :LANES], tab[:, LANES:2 * LANES], tab[:, 2 * LANES:]
    return t * c + pltpu.roll(t, LANES - shift, 1) * sa + pltpu.roll(t, shift, 1) * sb


def _prep_kernel(x_ref, mod_ref, nw_ref, win_ref, wa_ref, ba_ref, mqn_ref, wuq_ref, mkvn_ref,
                 wuk_ref, wuv_ref, gqn_ref, gkn_ref, rm_ref, rg_ref,
                 gq_ref, gk_ref, gv_ref, gg_ref, zg_ref, mq_ref, mk_ref, mvt_ref, cq_ref, ck_ref,
                 cvt_ref):
    x = x_ref[...]
    mod = mod_ref[...]
    shift, scale = mod[:, :D_MODEL], mod[:, D_MODEL:2 * D_MODEL]
    ms = jnp.mean(x * x, axis=-1, keepdims=True)
    h = (x * lax.rsqrt(ms + EPS) * nw_ref[...]) * (1.0 + scale) + shift
    hb = h.astype(BF16)

    uz = _dot(hb, win_ref[:, 3 * SEG:])
    zg_ref[...] = _silu(uz)

    ua = _dot(hb, win_ref[:, 0:SEG])
    gq_ref[...] = (ua[:, 0:GLA_QK] * GLA_DK ** -0.5).astype(BF16)
    gk_ref[...] = ua[:, GLA_QK:2 * GLA_QK].astype(BF16)
    gv_ref[...] = ua[:, 2 * GLA_QK:2 * GLA_QK + GLA_WIDTH].astype(BF16)
    a = ua[:, 2 * GLA_QK + GLA_WIDTH:SEG].astype(BF16)
    xg = _dot(a, wa_ref[...]) + ba_ref[...]
    log_sig = jnp.minimum(xg, 0.0) - jnp.log1p(jnp.exp(-jnp.abs(xg)))
    gg_ref[...] = log_sig * (1.0 / GLA_GATE_NORM)

    ub = _dot(hb, win_ref[:, SEG:2 * SEG])
    cq = ub[:, 0:MLA_Q_LORA]
    ckv = ub[:, MLA_Q_LORA:MLA_Q_LORA + MLA_KV_LORA]
    kr = ub[:, MLA_Q_LORA + MLA_KV_LORA:SEG]
    cqn = cq * lax.rsqrt(jnp.mean(cq * cq, axis=-1, keepdims=True) + EPS) * mqn_ref[...]
    ckvn = ckv * lax.rsqrt(jnp.mean(ckv * ckv, axis=-1, keepdims=True) + EPS) * mkvn_ref[...]
    qm = _dot(cqn.astype(BF16), wuq_ref[...])
    ckvb = ckvn.astype(BF16)
    kn = _dot(ckvb, wuk_ref[...])
    vm = _dot(ckvb, wuv_ref[...])
    rm = rm_ref[...]
    krr = _rope(kr, rm, MLA_ROPE // 4)
    mla_scale = (MLA_NOPE + MLA_ROPE) ** -0.5 * LOG2E
    for hd in range(MLA_HEADS):
        sl = slice(hd * LANES, (hd + 1) * LANES)
        mq_ref[:, sl] = (_rope(qm[:, sl], rm, MLA_ROPE // 4) * mla_scale).astype(BF16)
        mk_ref[:, sl] = (kn[:, sl] + krr).astype(BF16)
    for p in range(MLA_HEADS // 2):
        mvt_ref[p] = vm[:, p * LANES:(p + 1) * LANES].T.astype(BF16)

    uc = _dot(hb, win_ref[:, 2 * SEG:3 * SEG])
    rg = rg_ref[...]
    lo = lax.broadcasted_iota(jnp.int32, (TM, LANES), 1) < HEAD_DIM
    gqn = gqn_ref[...]
    for hd in range(GQA_HEADS):
        blk = uc[:, (hd // 2) * LANES:(hd // 2 + 1) * LANES]
        if hd % 2 == 1:
            blk = pltpu.roll(blk, HEAD_DIM, 1)
        xh = jnp.where(lo, blk, 0.0)
        msq = jnp.sum(xh * xh, axis=-1, keepdims=True) * (1.0 / HEAD_DIM)
        xh = xh * lax.rsqrt(msq + EPS) * gqn
        xh = _rope(xh, rg, HEAD_DIM // 4) * (HEAD_DIM ** -0.5 * LOG2E)
        if hd // GQA_GROUP == 1:
            xh = pltpu.roll(xh, HEAD_DIM, 1)
        cq_ref[:, hd * LANES:(hd + 1) * LANES] = xh.astype(BF16)
    k = uc[:, GQA_WIDTH:GQA_WIDTH + LANES]
    k2 = k * k
    ms_lo = jnp.sum(jnp.where(lo, k2, 0.0), axis=-1, keepdims=True) * (1.0 / HEAD_DIM)
    ms_hi = jnp.sum(jnp.where(lo, 0.0, k2), axis=-1, keepdims=True) * (1.0 / HEAD_DIM)
    inv = jnp.where(lo, lax.rsqrt(ms_lo + EPS), lax.rsqrt(ms_hi + EPS))
    kg = k * inv * gkn_ref[...]
    rg2 = jnp.concatenate([rg[:, j * LANES:(j + 1) * LANES]
                           + pltpu.roll(rg[:, j * LANES:(j + 1) * LANES], HEAD_DIM, 1)
                           for j in range(3)], axis=1)
    ck_ref[...] = _rope(kg, rg2, HEAD_DIM // 4).astype(BF16)
    cvt_ref[...] = uc[:, GQA_WIDTH + LANES:SEG].T.astype(BF16)


def _prep(xs, mod, lw, rope_m, rope_g):
    bsz, t, _ = xs.shape
    nt = t // TM
    tok = lambda w: pl.BlockSpec((None, TM, w), lambda b, i: (b, i, 0))
    full = lambda a: pl.BlockSpec(a.shape, lambda b, i: (0,) * a.ndim)
    tab = pl.BlockSpec((TM, 3 * LANES), lambda b, i: (i, 0))
    weights = [lw[n] for n in ("norm_w", "w_in", "w_a", "b_a", "mla_q_norm", "w_uq", "mla_kv_norm",
                               "w_uk", "w_uv", "gqa_q_norm", "gqa_k_norm")]
    out_shape = [
        jax.ShapeDtypeStruct((bsz, t, GLA_QK), BF16),
        jax.ShapeDtypeStruct((bsz, t, GLA_QK), BF16),
        jax.ShapeDtypeStruct((bsz, t, GLA_WIDTH), BF16),
        jax.ShapeDtypeStruct((bsz, t, 2 * GLA_QK), F32),
        jax.ShapeDtypeStruct((bsz, t, D_MIX), F32),
        jax.ShapeDtypeStruct((bsz, t, MLA_HEADS * LANES), BF16),
        jax.ShapeDtypeStruct((bsz, t, MLA_HEADS * LANES), BF16),
        jax.ShapeDtypeStruct((bsz, MLA_HEADS // 2, LANES, t), BF16),
        jax.ShapeDtypeStruct((bsz, t, GQA_HEADS * LANES), BF16),
        jax.ShapeDtypeStruct((bsz, t, LANES), BF16),
        jax.ShapeDtypeStruct((bsz, LANES, t), BF16),
    ]
    out_specs = [
        tok(GLA_QK), tok(GLA_QK), tok(GLA_WIDTH), tok(2 * GLA_QK), tok(D_MIX),
        tok(MLA_HEADS * LANES), tok(MLA_HEADS * LANES),
        pl.BlockSpec((None, MLA_HEADS // 2, LANES, TM), lambda b, i: (b, 0, 0, i)),
        tok(GQA_HEADS * LANES), tok(LANES),
        pl.BlockSpec((None, LANES, TM), lambda b, i: (b, 0, i)),
    ]
    return pl.pallas_call(
        _prep_kernel,
        grid=(bsz, nt),
        in_specs=[tok(D_MODEL),
                  pl.BlockSpec((None, 1, 3 * D_MODEL), lambda b, i: (2 * b + jnp.minimum(i, 1), 0, 0))]
                 + [full(w) for w in weights] + [tab, tab],
        out_specs=out_specs,
        out_shape=out_shape,
        compiler_params=_cparams(("parallel", "arbitrary")),
        name="prep",
    )(xs, mod, *weights, rope_m, rope_g)


def _gla_direction(q_ref, k_ref, v_ref, g_ref, l_ref, cmask_ref, qmask, omask, smask, s_ref, o_ref,
                   reverse):
    q = q_ref[...].astype(F32)
    k = k_ref[...].astype(F32)
    g = g_ref[...]
    g_hi = g.astype(BF16)
    r1 = g - g_hi.astype(F32)
    g_mid = r1.astype(BF16)
    g_lo = (r1 - g_mid.astype(F32)).astype(BF16)
    tri = l_ref[...]
    cum = _dot(tri, g_hi) + _dot(tri, g_mid) + _dot(tri, g_lo)
    cmask = cmask_ref[...]
    order = range(CHUNKS - 1, -1, -1) if reverse else range(CHUNKS)
    for c in order:
        sl = slice(c * GLA_CHUNK, (c + 1) * GLA_CHUNK)
        cum_c = cum[sl]
        last = cum_c[0:1] if reverse else cum_c[GLA_CHUNK - 1:GLA_CHUNK]
        qd = q[sl] * jnp.exp(cum_c)
        ki = k[sl] * jnp.exp(-cum_c)
        kd = k[sl] * jnp.exp(last - cum_c)
        vc = v_ref[sl, :]
        qh = jnp.concatenate([qd] * GLA_HEADS, axis=0) * qmask
        att = _dot_nt(qh.astype(BF16), ki.astype(BF16)) * cmask
        o_all = _dot(att.astype(BF16), vc) * omask
        o_intra = o_all[0:GLA_CHUNK]
        for hd in range(1, GLA_HEADS):
            o_intra = o_intra + o_all[hd * GLA_CHUNK:(hd + 1) * GLA_CHUNK]
        st = s_ref[...]
        o_inter = _dot_nt(qd.astype(BF16), st.astype(BF16))
        o_ref[sl, :] = o_inter + o_intra
        ds = _dot_tn(vc, kd.astype(BF16)) * smask
        s_ref[...] = st * jnp.exp(last) + ds


def _gla_kernel(qf_ref, kf_ref, vf_ref, gf_ref, qb_ref, kb_ref, vb_ref, gb_ref, lf_ref, lb_ref,
                cmf_ref, cmb_ref, qmask_ref, omask_ref, smask_ref, of_ref, ob_ref, sf_ref, sb_ref):
    @pl.when(pl.program_id(1) == 0)
    def _():
        sf_ref[...] = jnp.zeros_like(sf_ref)
        sb_ref[...] = jnp.zeros_like(sb_ref)

    qmask, omask, smask = qmask_ref[...], omask_ref[...], smask_ref[...]
    _gla_direction(qf_ref, kf_ref, vf_ref, gf_ref, lf_ref, cmf_ref, qmask, omask, smask, sf_ref,
                   of_ref, False)
    _gla_direction(qb_ref, kb_ref, vb_ref, gb_ref, lb_ref, cmb_ref, qmask, omask, smask, sb_ref,
                   ob_ref, True)


def _gla_consts():
    r = np.arange(TM)
    same = (r[:, None] // GLA_CHUNK) == (r[None, :] // GLA_CHUNK)
    lf = (same & (r[None, :] <= r[:, None])).astype(np.float32)
    lb = (same & (r[None, :] >= r[:, None])).astype(np.float32)
    rows = np.arange(GLA_HEADS * GLA_CHUNK)
    j = np.arange(GLA_CHUNK)
    cmf = ((rows[:, None] % GLA_CHUNK) >= j[None, :]).astype(np.float32)
    cmb = ((rows[:, None] % GLA_CHUNK) <= j[None, :]).astype(np.float32)
    qmask = ((rows[:, None] // GLA_CHUNK) == (np.arange(GLA_QK)[None, :] // GLA_DK)).astype(np.float32)
    omask = ((rows[:, None] // GLA_CHUNK) == (np.arange(GLA_WIDTH)[None, :] // GLA_DV)).astype(np.float32)
    smask = ((np.arange(GLA_WIDTH)[:, None] // GLA_DV) == (np.arange(GLA_QK)[None, :] // GLA_DK)).astype(np.float32)
    return (jnp.asarray(lf, BF16), jnp.asarray(lb, BF16), jnp.asarray(cmf), jnp.asarray(cmb),
            jnp.asarray(qmask), jnp.asarray(omask), jnp.asarray(smask))


def _gla(gq, gk, gv, gg):
    bsz, t, _ = gq.shape
    nt = t // TM
    consts = _gla_consts()
    fwd = lambda b, i: (b, i, 0)
    rev_blk = lambda i: jnp.where(i == 0, 0, nt - i)
    bwd = lambda b, i: (b, rev_blk(i), 0)
    bwd_g = lambda b, i: (b, rev_blk(i), 1)
    spec = lambda w, im: pl.BlockSpec((None, TM, w), im)
    full = lambda a: pl.BlockSpec(a.shape, lambda b, i: (0,) * a.ndim)
    return pl.pallas_call(
        _gla_kernel,
        grid=(bsz, nt),
        in_specs=[spec(GLA_QK, fwd), spec(GLA_QK, fwd), spec(GLA_WIDTH, fwd), spec(GLA_QK, fwd),
                  spec(GLA_QK, bwd), spec(GLA_QK, bwd), spec(GLA_WIDTH, bwd), spec(GLA_QK, bwd_g)]
                 + [full(a) for a in consts],
        out_specs=[spec(GLA_WIDTH, fwd), spec(GLA_WIDTH, bwd)],
        out_shape=[jax.ShapeDtypeStruct((bsz, t, GLA_WIDTH), F32)] * 2,
        scratch_shapes=[pltpu.VMEM((GLA_WIDTH, GLA_QK), F32)] * 2,
        compiler_params=_cparams(("parallel", "arbitrary")),
        name="gla",
    )(gq, gk, gv, gg, gq, gk, gv, gg, *consts)


def _head_values(vt_ref, h, cols, shared_kv):
    if shared_kv:
        r0 = (h // GQA_GROUP) * HEAD_DIM
        return vt_ref[r0:r0 + HEAD_DIM, cols]
    r0 = (h % 2) * MLA_V
    return vt_ref[h // 2, r0:r0 + MLA_V, cols]


def _write_heads(o_ref, outs_t):
    for p in range(len(outs_t) // 2):
        pair = jnp.concatenate([outs_t[2 * p], outs_t[2 * p + 1]], axis=0)
        o_ref[:, p * LANES:(p + 1) * LANES] = pair.T


def _attn_ctx_kernel(q_ref, k_ref, vt_ref, o_ref, *, shared_kv):
    heads = q_ref.shape[1] // LANES
    outs_t = []
    for h in range(heads):
        kh = k_ref[...] if shared_kv else k_ref[:, h * LANES:(h + 1) * LANES]
        s = _dot_nt(kh, q_ref[:, h * LANES:(h + 1) * LANES])
        p = jnp.exp2(s - jnp.max(s, axis=0, keepdims=True))
        l = jnp.sum(p, axis=0, keepdims=True)
        outs_t.append(_dot(_head_values(vt_ref, h, slice(None), shared_kv), p.astype(BF16)) / l)
    _write_heads(o_ref, outs_t)


def _attn_kernel(qa_ref, qb_ref, k_ref, vt_ref, o_ref, s0_ref, s1_ref, *, shared_kv):
    t = k_ref.shape[0]
    heads = qa_ref.shape[1] // LANES
    sbufs = (s0_ref, s1_ref)
    chunks = [slice(a, a + ATT_CK) for a in range(0, t, ATT_CK)]

    def q_of(h):
        sl = slice(h * LANES, (h + 1) * LANES)
        return jnp.concatenate([qa_ref[:, sl], qb_ref[:, sl]], axis=0)

    def fold8(v):
        return v.reshape(v.shape[0] // 8, 8, v.shape[1])

    def qk_chunk(h, qh, rows, m8):
        kh = k_ref[rows, :] if shared_kv else k_ref[rows, h * LANES:(h + 1) * LANES]
        s = _dot_nt(kh, qh)
        sbufs[h % 2][rows, :] = s
        smax = jnp.max(fold8(s), axis=0)
        return smax if m8 is None else jnp.maximum(m8, smax)

    m8 = None
    qh = q_of(0)
    for rows in chunks:
        m8 = qk_chunk(0, qh, rows, m8)
    outs_t = []
    for h in range(heads):
        m = jnp.max(m8, axis=0, keepdims=True)
        m8, acc, l8 = None, None, None
        if h + 1 < heads:
            qh = q_of(h + 1)
        for rows in chunks:
            if h + 1 < heads:
                m8 = qk_chunk(h + 1, qh, rows, m8)
            p = jnp.exp2(sbufs[h % 2][rows, :] - m)
            ps = jnp.sum(fold8(p), axis=0)
            l8 = ps if l8 is None else l8 + ps
            pv = _dot(_head_values(vt_ref, h, rows, shared_kv), p.astype(BF16))
            acc = pv if acc is None else acc + pv
        outs_t.append(acc / jnp.sum(l8, axis=0, keepdims=True))
    _write_heads(o_ref, outs_t)


def _attention(q, k, vt, shared_kv):
    bsz, t, width = q.shape
    n2 = (t - CTX_LEN) // (2 * TM)
    kw = k.shape[2]
    name = "attn_gqa" if shared_kv else "attn_mla"
    if shared_kv:
        v_spec = pl.BlockSpec((None, LANES, t), lambda b, j: (b, 0, 0))
        vc_spec = pl.BlockSpec((None, LANES, CTX_LEN), lambda b: (b, 0, 0))
    else:
        v_spec = pl.BlockSpec((None, vt.shape[1], LANES, t), lambda b, j: (b, 0, 0, 0))
        vc_spec = pl.BlockSpec((None, vt.shape[1], LANES, CTX_LEN), lambda b: (b, 0, 0, 0))
    y_lat = pl.pallas_call(
        functools.partial(_attn_kernel, shared_kv=shared_kv),
        grid=(bsz, n2),
        in_specs=[pl.BlockSpec((None, TM, width), lambda b, j: (b, 2 * j + 1, 0)),
                  pl.BlockSpec((None, TM, width), lambda b, j: (b, 2 * j + 2, 0)),
                  pl.BlockSpec((None, t, kw), lambda b, j: (b, 0, 0)), v_spec],
        out_specs=pl.BlockSpec((None, 2 * TM, width // 2), lambda b, j: (b, j, 0)),
        out_shape=jax.ShapeDtypeStruct((bsz, t - CTX_LEN, width // 2), F32),
        scratch_shapes=[pltpu.VMEM((t, 2 * TM), F32)] * 2,
        compiler_params=_cparams(("parallel", "arbitrary")),
        name=name,
    )(q, q, k, vt)
    y_ctx = pl.pallas_call(
        functools.partial(_attn_ctx_kernel, shared_kv=shared_kv),
        grid=(bsz,),
        in_specs=[pl.BlockSpec((None, CTX_LEN, width), lambda b: (b, 0, 0)),
                  pl.BlockSpec((None, CTX_LEN, kw), lambda b: (b, 0, 0)), vc_spec],
        out_specs=pl.BlockSpec((None, CTX_LEN, width // 2), lambda b: (b, 0, 0)),
        out_shape=jax.ShapeDtypeStruct((bsz, CTX_LEN, width // 2), F32),
        compiler_params=_cparams(("parallel",)),
        name=name + "_ctx",
    )(q, k, vt)
    return y_lat, y_ctx


def _outproj_kernel(x_ref, of_ref, ob_ref, yb_ref, ybc_ref, yc_ref, ycc_ref, zg_ref, gnw_ref,
                    mod_ref, wout_ref, fnw_ref, o_ref, *, final):
    lo = lax.broadcasted_iota(jnp.int32, (TM, LANES), 1) < GLA_DV
    o = of_ref[...] + ob_ref[...]
    gnw = gnw_ref[...]
    zg = zg_ref[...]
    parts = []
    for j in range(GLA_WIDTH // LANES):
        sl = slice(j * LANES, (j + 1) * LANES)
        oj = o[:, sl]
        o2 = oj * oj
        ms_lo = jnp.sum(jnp.where(lo, o2, 0.0), axis=-1, keepdims=True) * (1.0 / GLA_DV)
        ms_hi = jnp.sum(jnp.where(lo, 0.0, o2), axis=-1, keepdims=True) * (1.0 / GLA_DV)
        inv = jnp.where(lo, lax.rsqrt(ms_lo + EPS), lax.rsqrt(ms_hi + EPS))
        parts.append(oj * inv * gnw[:, sl] * zg[:, sl])
    if final:
        yb, yc = yb_ref[...], yc_ref[...]
    else:
        is_ctx = pl.program_id(1) == 0
        yb = jnp.where(is_ctx, ybc_ref[...], yb_ref[...])
        yc = jnp.where(is_ctx, ycc_ref[...], yc_ref[...])
    parts.append(yb * zg[:, GLA_WIDTH:GLA_WIDTH + MLA_WIDTH])
    parts.append(yc * zg[:, GLA_WIDTH + MLA_WIDTH:])
    y = jnp.concatenate(parts, axis=-1).astype(BF16)
    gate = mod_ref[...]
    xn = x_ref[...] + gate * _dot(y, wout_ref[...])
    if final:
        xn = xn * lax.rsqrt(jnp.mean(xn * xn, axis=-1, keepdims=True) + EPS) * fnw_ref[...]
    o_ref[...] = xn


def _outproj(xs, o_f, o_b, y_b, y_c, zg, mod, lw, final_norm_w, final):
    bsz, t, _ = xs.shape
    off = 1 if final else 0
    nt = t // TM - off
    tok = lambda w: pl.BlockSpec((None, TM, w), lambda b, i: (b, i + off, 0))
    lat = lambda w: pl.BlockSpec((None, TM, w), lambda b, i: (b, jnp.maximum(i + off - 1, 0), 0))
    ctx = lambda w: pl.BlockSpec((None, CTX_LEN, w), lambda b, i: (b, 0, 0))
    full = lambda a: pl.BlockSpec(a.shape, lambda b, i: (0,) * a.ndim)
    return pl.pallas_call(
        functools.partial(_outproj_kernel, final=final),
        grid=(bsz, nt),
        in_specs=[tok(D_MODEL), tok(GLA_WIDTH), tok(GLA_WIDTH), lat(MLA_WIDTH), ctx(MLA_WIDTH),
                  lat(GQA_WIDTH), ctx(GQA_WIDTH), tok(D_MIX), full(lw["gla_norm"]),
                  pl.BlockSpec((None, 1, D_MODEL),
                               lambda b, i: (2 * b + jnp.minimum(i + off, 1), 0, 2)),
                  full(lw["w_out"]), full(final_norm_w)],
        out_specs=pl.BlockSpec((None, TM, D_MODEL), lambda b, i: (b, i, 0)),
        out_shape=jax.ShapeDtypeStruct((bsz, nt * TM, D_MODEL), F32),
        compiler_params=_cparams(("parallel", "arbitrary")),
        name="outproj",
    )(xs, o_f, o_b, *y_b, *y_c, zg, lw["gla_norm"], mod, lw["w_out"], final_norm_w)


def _rope_tables(rows, d_rot, lane0):
    quarter = d_rot // 4
    n = rows * GRID_W
    row = jnp.repeat(jnp.arange(rows), GRID_W).astype(F32)
    col = jnp.tile(jnp.arange(GRID_W), rows).astype(F32)
    freqs = ROPE_THETA ** (-jnp.arange(quarter, dtype=F32) / quarter)
    ang = jnp.stack([row[:, None] * freqs, col[:, None] * freqs], axis=1)
    cos, sin = jnp.cos(ang), jnp.sin(ang)
    zero = jnp.zeros_like(sin)
    cos_l = jnp.stack([cos, cos], axis=2).reshape(n, d_rot)
    sa_l = jnp.stack([-sin, zero], axis=2).reshape(n, d_rot)
    sb_l = jnp.stack([zero, sin], axis=2).reshape(n, d_rot)

    def place(v, fill):
        out = jnp.full((n, LANES), fill, F32)
        return out.at[:, lane0:lane0 + d_rot].set(v)

    c_fill = 1.0 if lane0 > 0 else 0.0
    lat = jnp.concatenate([place(cos_l, c_fill), place(sa_l, 0.0), place(sb_l, 0.0)], axis=1)
    ident = jnp.concatenate([place(jnp.ones((n, d_rot), F32), c_fill)[:CTX_LEN],
                             jnp.zeros((CTX_LEN, 2 * LANES), F32)], axis=1)
    return jnp.concatenate([ident, lat], axis=0)


def _layer_weights(l, norm_w, w_in, gla_w_a_fwd, gla_b_a_fwd, gla_w_a_bwd, gla_b_a_bwd, gla_norm_w,
                   mla_q_norm_w, mla_w_uq, mla_kv_norm_w, mla_w_ukv, gqa_q_norm_w, gqa_k_norm_w,
                   w_out):
    w = w_in[l]
    zc = lambda n: jnp.zeros((D_MODEL, n), F32)
    c0 = 2 * GLA_QK + 2 * GLA_WIDTH
    m0 = c0 + 2 * GLA_GATE_RANK
    g0 = m0 + MLA_Q_LORA + MLA_KV_LORA + MLA_ROPE + MLA_WIDTH
    seg_a = [w[:, 0:2 * GLA_QK + GLA_WIDTH], w[:, c0:m0], zc(LANES - 2 * GLA_GATE_RANK)]
    kr0 = m0 + MLA_Q_LORA + MLA_KV_LORA
    seg_b = [w[:, m0:kr0], zc(MLA_NOPE), w[:, kr0:kr0 + MLA_ROPE], zc(LANES - MLA_NOPE - MLA_ROPE)]
    seg_c = [w[:, g0:g0 + GQA_WIDTH + 2 * LANES]]
    seg_z = [w[:, 2 * GLA_QK + GLA_WIDTH:c0], w[:, kr0 + MLA_ROPE:g0], w[:, g0 + GQA_WIDTH + 2 * LANES:]]
    w_pad = jnp.concatenate(seg_a + seg_b + seg_c + seg_z, axis=1).astype(BF16)

    w_a = jnp.zeros((LANES, 2 * GLA_QK), F32)
    w_a = w_a.at[0:GLA_GATE_RANK, 0:GLA_QK].set(gla_w_a_fwd[l])
    w_a = w_a.at[GLA_GATE_RANK:2 * GLA_GATE_RANK, GLA_QK:].set(gla_w_a_bwd[l])
    b_a = jnp.concatenate([gla_b_a_fwd[l], gla_b_a_bwd[l]])[None, :]

    dq = MLA_NOPE + MLA_ROPE
    uq = mla_w_uq[l].reshape(MLA_Q_LORA, MLA_HEADS, dq)
    uq = jnp.pad(uq, ((0, 0), (0, 0), (0, LANES - dq))).reshape(MLA_Q_LORA, MLA_HEADS * LANES)
    ukv = mla_w_ukv[l].reshape(MLA_KV_LORA, MLA_HEADS, MLA_NOPE + MLA_V)
    uk = jnp.pad(ukv[:, :, :MLA_NOPE], ((0, 0), (0, 0), (0, LANES - MLA_NOPE)))
    uk = uk.reshape(MLA_KV_LORA, MLA_HEADS * LANES)
    uv = ukv[:, :, MLA_NOPE:].reshape(MLA_KV_LORA, MLA_WIDTH)

    pad_lane = lambda v: jnp.concatenate([v, jnp.zeros((LANES - v.shape[0],), F32)])[None, :]
    return {
        "norm_w": norm_w[l][None, :],
        "w_in": w_pad,
        "w_a": w_a.astype(BF16),
        "b_a": b_a,
        "mla_q_norm": mla_q_norm_w[l][None, :],
        "w_uq": uq.astype(BF16),
        "mla_kv_norm": mla_kv_norm_w[l][None, :],
        "w_uk": uk.astype(BF16),
        "w_uv": uv.astype(BF16),
        "gqa_q_norm": pad_lane(gqa_q_norm_w[l]),
        "gqa_k_norm": jnp.tile(gqa_k_norm_w[l], GQA_KV_HEADS)[None, :],
        "gla_norm": jnp.tile(gla_norm_w[l], GLA_HEADS)[None, :],
        "w_out": w_out[l].astype(BF16),
    }


def kernel(x, c, ctx, c_ctx, norm_w, w_mod, b_mod, w_in, gla_w_a_fwd, gla_b_a_fwd, gla_w_a_bwd,
           gla_b_a_bwd, gla_norm_w, mla_q_norm_w, mla_w_uq, mla_kv_norm_w, mla_w_ukv,
           gqa_q_norm_w, gqa_k_norm_w, w_out, final_norm_w):
    bsz, n, _ = x.shape
    depth = w_in.shape[0]
    assert ctx.shape[1] == CTX_LEN and n % (2 * TM) == 0 and n % GRID_W == 0
    rows = n // GRID_W
    rope_m = _rope_tables(rows, MLA_ROPE, MLA_NOPE)
    rope_g = _rope_tables(rows, HEAD_DIM, 0)

    mod_rows = -(-(bsz + 1) // 8) * 8
    cs = jnp.zeros((mod_rows, D_MODEL), F32).at[:bsz].set(c).at[bsz].set(c_ctx)
    mod_all = _modulation(cs, w_mod, b_mod)
    fnw = final_norm_w[None, :]

    xs = jnp.concatenate([ctx, x], axis=1)
    for l in range(depth):
        lw = _layer_weights(l, norm_w, w_in, gla_w_a_fwd, gla_b_a_fwd, gla_w_a_bwd, gla_b_a_bwd,
                            gla_norm_w, mla_q_norm_w, mla_w_uq, mla_kv_norm_w, mla_w_ukv,
                            gqa_q_norm_w, gqa_k_norm_w, w_out)
        m = mod_all[l]
        mod = jnp.stack([jnp.broadcast_to(m[bsz], (bsz, 3 * D_MODEL)), m[:bsz]], axis=1)
        mod = mod.reshape(2 * bsz, 1, 3 * D_MODEL)
        gq, gk, gv, gg, zg, mq, mk, mvt, cq, ck, cvt = _prep(xs, mod, lw, rope_m, rope_g)
        o_f, o_b = _gla(gq, gk, gv, gg)
        y_b = _attention(mq, mk, mvt, shared_kv=False)
        y_c = _attention(cq, ck, cvt, shared_kv=True)
        xs = _outproj(xs, o_f, o_b, y_b, y_c, zg, mod, lw, fnw, final=(l == depth - 1))
    return xs
```

```python
import functools

import numpy as np
import jax
import jax.numpy as jnp
from jax import lax
from jax.experimental import pallas as pl
from jax.experimental.pallas import tpu as pltpu

D_MODEL = 1024
GRID_W = 64
CTX_LEN = 256
HEAD_DIM = 64
ROPE_THETA = 10000.0
EPS = 1e-6

GLA_HEADS = 4
GLA_DK = 32
GLA_DV = 64
GLA_WIDTH = GLA_HEADS * GLA_DV
GLA_QK = GLA_HEADS * GLA_DK
GLA_GATE_RANK = 16
GLA_GATE_NORM = 16.0
GLA_CHUNK = 64

MLA_HEADS = 6
MLA_NOPE = 64
MLA_ROPE = 32
MLA_V = 64
MLA_Q_LORA = 256
MLA_KV_LORA = 256
MLA_WIDTH = MLA_HEADS * MLA_V

GQA_HEADS = 6
GQA_KV_HEADS = 2
GQA_GROUP = GQA_HEADS // GQA_KV_HEADS
GQA_WIDTH = GQA_HEADS * HEAD_DIM

D_MIX = GLA_WIDTH + MLA_WIDTH + GQA_WIDTH

LANES = 128
TM = CTX_LEN
CHUNKS = TM // GLA_CHUNK
SEG = 640
W_IN_PAD = 3 * SEG + D_MIX
VMEM_LIMIT = 48 * 1024 * 1024
ATT_CK = 256
ATT_SUB = 32
LOG2E = 1.4426950408889634

F32 = jnp.float32
BF16 = jnp.bfloat16


def _cparams(sem):
    return pltpu.CompilerParams(dimension_semantics=sem, vmem_limit_bytes=VMEM_LIMIT)


def _dot(a, b):
    return jnp.dot(a, b, preferred_element_type=F32)


def _dot_nt(a, b):
    return lax.dot_general(a, b, (((1,), (1,)), ((), ())), preferred_element_type=F32)


def _dot_tn(a, b):
    return lax.dot_general(a, b, (((0,), (0,)), ((), ())), preferred_element_type=F32)


def _silu(v):
    return v / (1.0 + jnp.exp(-v))


def _mod_kernel(c_ref, w_ref, b_ref, o_ref):
    s = _silu(c_ref[...]).astype(BF16)
    o_ref[...] = _dot(s, w_ref[...].astype(BF16)) + b_ref[...]


def _modulation(cs, w_mod, b_mod):
    depth = w_mod.shape[0]
    rows = cs.shape[0]
    return pl.pallas_call(
        _mod_kernel,
        grid=(depth, 3),
        in_specs=[
            pl.BlockSpec((rows, D_MODEL), lambda l, j: (0, 0)),
            pl.BlockSpec((None, D_MODEL, D_MODEL), lambda l, j: (l, 0, j)),
            pl.BlockSpec((None, 1, D_MODEL), lambda l, j: (l, 0, j)),
        ],
        out_specs=pl.BlockSpec((None, rows, D_MODEL), lambda l, j: (l, 0, j)),
        out_shape=jax.ShapeDtypeStruct((depth, rows, 3 * D_MODEL), F32),
        compiler_params=_cparams(("arbitrary", "arbitrary")),
        name="modulation",
    )(cs, w_mod, b_mod.reshape(depth, 1, 3 * D_MODEL))


def _rope(t, tab, shift):
    c, sa, sb = tab[:, :LANES], tab[:, LANES:2 * LANES], tab[:, 2 * LANES:]
    return t * c + pltpu.roll(t, LANES - shift, 1) * sa + pltpu.roll(t, shift, 1) * sb


def _prep_kernel(x_ref, mod_ref, nw_ref, win_ref, wa_ref, ba_ref, mqn_ref, wuq_ref, mkvn_ref,
                 wuk_ref, wuv_ref, gqn_ref, gkn_ref, rm_ref, rg_ref,
                 gq_ref, gk_ref, gv_ref, gg_ref, zg_ref, mq_ref, mk_ref, mvt_ref, cq_ref, ck_ref,
                 cvt_ref):
    x = x_ref[...]
    mod = mod_ref[...]
    shift, scale = mod[:, :D_MODEL], mod[:, D_MODEL:2 * D_MODEL]
    ms = jnp.mean(x * x, axis=-1, keepdims=True)
    h = (x * lax.rsqrt(ms + EPS) * nw_ref[...]) * (1.0 + scale) + shift
    hb = h.astype(BF16)

    uz = _dot(hb, win_ref[:, 3 * SEG:])
    zg_ref[...] = _silu(uz)

    ua = _dot(hb, win_ref[:, 0:SEG])
    gq_ref[...] = (ua[:, 0:GLA_QK] * GLA_DK ** -0.5).astype(BF16)
    gk_ref[...] = ua[:, GLA_QK:2 * GLA_QK].astype(BF16)
    gv_ref[...] = ua[:, 2 * GLA_QK:2 * GLA_QK + GLA_WIDTH].astype(BF16)
    a = ua[:, 2 * GLA_QK + GLA_WIDTH:SEG].astype(BF16)
    xg = _dot(a, wa_ref[...]) + ba_ref[...]
    log_sig = jnp.minimum(xg, 0.0) - jnp.log1p(jnp.exp(-jnp.abs(xg)))
    gg_ref[...] = log_sig * (1.0 / GLA_GATE_NORM)

    ub = _dot(hb, win_ref[:, SEG:2 * SEG])
    cq = ub[:, 0:MLA_Q_LORA]
    ckv = ub[:, MLA_Q_LORA:MLA_Q_LORA + MLA_KV_LORA]
    kr = ub[:, MLA_Q_LORA + MLA_KV_LORA:SEG]
    cqn = cq * lax.rsqrt(jnp.mean(cq * cq, axis=-1, keepdims=True) + EPS) * mqn_ref[...]
    ckvn = ckv * lax.rsqrt(jnp.mean(ckv * ckv, axis=-1, keepdims=True) + EPS) * mkvn_ref[...]
    qm = _dot(cqn.astype(BF16), wuq_ref[...])
    ckvb = ckvn.astype(BF16)
    kn = _dot(ckvb, wuk_ref[...])
    vm = _dot(ckvb, wuv_ref[...])
    rm = rm_ref[...]
    krr = _rope(kr, rm, MLA_ROPE // 4)
    mla_scale = (MLA_NOPE + MLA_ROPE) ** -0.5 * LOG2E
    for hd in range(MLA_HEADS):
        sl = slice(hd * LANES, (hd + 1) * LANES)
        mq_ref[:, sl] = (_rope(qm[:, sl], rm, MLA_ROPE // 4) * mla_scale).astype(BF16)
        mk_ref[:, sl] = (kn[:, sl] + krr).astype(BF16)
    for p in range(MLA_HEADS // 2):
        mvt_ref[p] = vm[:, p * LANES:(p + 1) * LANES].T.astype(BF16)

    uc = _dot(hb, win_ref[:, 2 * SEG:3 * SEG])
    rg = rg_ref[...]
    lo = lax.broadcasted_iota(jnp.int32, (TM, LANES), 1) < HEAD_DIM
    gqn = gqn_ref[...]
    for hd in range(GQA_HEADS):
        blk = uc[:, (hd // 2) * LANES:(hd // 2 + 1) * LANES]
        if hd % 2 == 1:
            blk = pltpu.roll(blk, HEAD_DIM, 1)
        xh = jnp.where(lo, blk, 0.0)
        msq = jnp.sum(xh * xh, axis=-1, keepdims=True) * (1.0 / HEAD_DIM)
        xh = xh * lax.rsqrt(msq + EPS) * gqn
        xh = _rope(xh, rg, HEAD_DIM // 4) * (HEAD_DIM ** -0.5 * LOG2E)
        if hd // GQA_GROUP == 1:
            xh = pltpu.roll(xh, HEAD_DIM, 1)
        cq_ref[:, hd * LANES:(hd + 1) * LANES] = xh.astype(BF16)
    k = uc[:, GQA_WIDTH:GQA_WIDTH + LANES]
    k2 = k * k
    ms_lo = jnp.sum(jnp.where(lo, k2, 0.0), axis=-1, keepdims=True) * (1.0 / HEAD_DIM)
    ms_hi = jnp.sum(jnp.where(lo, 0.0, k2), axis=-1, keepdims=True) * (1.0 / HEAD_DIM)
    inv = jnp.where(lo, lax.rsqrt(ms_lo + EPS), lax.rsqrt(ms_hi + EPS))
    kg = k * inv * gkn_ref[...]
    rg2 = jnp.concatenate([rg[:, j * LANES:(j + 1) * LANES]
                           + pltpu.roll(rg[:, j * LANES:(j + 1) * LANES], HEAD_DIM, 1)
                           for j in range(3)], axis=1)
    ck_ref[...] = _rope(kg, rg2, HEAD_DIM // 4).astype(BF16)
    cvt_ref[...] = uc[:, GQA_WIDTH + LANES:SEG].T.astype(BF16)


def _prep(xs, mod, lw, rope_m, rope_g):
    bsz, t, _ = xs.shape
    nt = t // TM
    tok = lambda w: pl.BlockSpec((None, TM, w), lambda b, i: (b, i, 0))
    full = lambda a: pl.BlockSpec(a.shape, lambda b, i: (0,) * a.ndim)
    tab = pl.BlockSpec((TM, 3 * LANES), lambda b, i: (i, 0))
    weights = [lw[n] for n in ("norm_w", "w_in", "w_a", "b_a", "mla_q_norm", "w_uq", "mla_kv_norm",
                               "w_uk", "w_uv", "gqa_q_norm", "gqa_k_norm")]
    out_shape = [
        jax.ShapeDtypeStruct((bsz, t, GLA_QK), BF16),
        jax.ShapeDtypeStruct((bsz, t, GLA_QK), BF16),
        jax.ShapeDtypeStruct((bsz, t, GLA_WIDTH), BF16),
        jax.ShapeDtypeStruct((bsz, t, 2 * GLA_QK), F32),
        jax.ShapeDtypeStruct((bsz, t, D_MIX), F32),
        jax.ShapeDtypeStruct((bsz, t, MLA_HEADS * LANES), BF16),
        jax.ShapeDtypeStruct((bsz, t, MLA_HEADS * LANES), BF16),
        jax.ShapeDtypeStruct((bsz, MLA_HEADS // 2, LANES, t), BF16),
        jax.ShapeDtypeStruct((bsz, t, GQA_HEADS * LANES), BF16),
        jax.ShapeDtypeStruct((bsz, t, LANES), BF16),
        jax.ShapeDtypeStruct((bsz, LANES, t), BF16),
    ]
    out_specs = [
        tok(GLA_QK), tok(GLA_QK), tok(GLA_WIDTH), tok(2 * GLA_QK), tok(D_MIX),
        tok(MLA_HEADS * LANES), tok(MLA_HEADS * LANES),
        pl.BlockSpec((None, MLA_HEADS // 2, LANES, TM), lambda b, i: (b, 0, 0, i)),
        tok(GQA_HEADS * LANES), tok(LANES),
        pl.BlockSpec((None, LANES, TM), lambda b, i: (b, 0, i)),
    ]
    return pl.pallas_call(
        _prep_kernel,
        grid=(bsz, nt),
        in_specs=[tok(D_MODEL),
                  pl.BlockSpec((None, 1, 3 * D_MODEL), lambda b, i: (2 * b + jnp.minimum(i, 1), 0, 0))]
                 + [full(w) for w in weights] + [tab, tab],
        out_specs=out_specs,
        out_shape=out_shape,
        compiler_params=_cparams(("parallel", "arbitrary")),
        name="prep",
    )(xs, mod, *weights, rope_m, rope_g)


def _gla_direction(q_ref, k_ref, v_ref, g_ref, l_ref, cmask_ref, qmask, omask, smask, s_ref, o_ref,
                   reverse):
    q = q_ref[...].astype(F32)
    k = k_ref[...].astype(F32)
    g = g_ref[...]
    g_hi = g.astype(BF16)
    r1 = g - g_hi.astype(F32)
    g_mid = r1.astype(BF16)
    g_lo = (r1 - g_mid.astype(F32)).astype(BF16)
    tri = l_ref[...]
    cum = _dot(tri, g_hi) + _dot(tri, g_mid) + _dot(tri, g_lo)
    cmask = cmask_ref[...]
    order = range(CHUNKS - 1, -1, -1) if reverse else range(CHUNKS)
    for c in order:
        sl = slice(c * GLA_CHUNK, (c + 1) * GLA_CHUNK)
        cum_c = cum[sl]
        last = cum_c[0:1] if reverse else cum_c[GLA_CHUNK - 1:GLA_CHUNK]
        qd = q[sl] * jnp.exp(cum_c)
        ki = k[sl] * jnp.exp(-cum_c)
        kd = k[sl] * jnp.exp(last - cum_c)
        vc = v_ref[sl, :]
        qh = jnp.concatenate([qd] * GLA_HEADS, axis=0) * qmask
        att = _dot_nt(qh.astype(BF16), ki.astype(BF16)) * cmask
        o_all = _dot(att.astype(BF16), vc) * omask
        o_intra = o_all[0:GLA_CHUNK]
        for hd in range(1, GLA_HEADS):
            o_intra = o_intra + o_all[hd * GLA_CHUNK:(hd + 1) * GLA_CHUNK]
        st = s_ref[...]
        o_inter = _dot_nt(qd.astype(BF16), st.astype(BF16))
        o_ref[sl, :] = o_inter + o_intra
        ds = _dot_tn(vc, kd.astype(BF16)) * smask
        s_ref[...] = st * jnp.exp(last) + ds


def _gla_kernel(qf_ref, kf_ref, vf_ref, gf_ref, qb_ref, kb_ref, vb_ref, gb_ref, lf_ref, lb_ref,
                cmf_ref, cmb_ref, qmask_ref, omask_ref, smask_ref, of_ref, ob_ref, sf_ref, sb_ref):
    @pl.when(pl.program_id(1) == 0)
    def _():
        sf_ref[...] = jnp.zeros_like(sf_ref)
        sb_ref[...] = jnp.zeros_like(sb_ref)

    qmask, omask, smask = qmask_ref[...], omask_ref[...], smask_ref[...]
    _gla_direction(qf_ref, kf_ref, vf_ref, gf_ref, lf_ref, cmf_ref, qmask, omask, smask, sf_ref,
                   of_ref, False)
    _gla_direction(qb_ref, kb_ref, vb_ref, gb_ref, lb_ref, cmb_ref, qmask, omask, smask, sb_ref,
                   ob_ref, True)


def _gla_consts():
    r = np.arange(TM)
    same = (r[:, None] // GLA_CHUNK) == (r[None, :] // GLA_CHUNK)
    lf = (same & (r[None, :] <= r[:, None])).astype(np.float32)
    lb = (same & (r[None, :] >= r[:, None])).astype(np.float32)
    rows = np.arange(GLA_HEADS * GLA_CHUNK)
    j = np.arange(GLA_CHUNK)
    cmf = ((rows[:, None] % GLA_CHUNK) >= j[None, :]).astype(np.float32)
    cmb = ((rows[:, None] % GLA_CHUNK) <= j[None, :]).astype(np.float32)
    qmask = ((rows[:, None] // GLA_CHUNK) == (np.arange(GLA_QK)[None, :] // GLA_DK)).astype(np.float32)
    omask = ((rows[:, None] // GLA_CHUNK) == (np.arange(GLA_WIDTH)[None, :] // GLA_DV)).astype(np.float32)
    smask = ((np.arange(GLA_WIDTH)[:, None] // GLA_DV) == (np.arange(GLA_QK)[None, :] // GLA_DK)).astype(np.float32)
    return (jnp.asarray(lf, BF16), jnp.asarray(lb, BF16), jnp.asarray(cmf), jnp.asarray(cmb),
            jnp.asarray(qmask), jnp.asarray(omask), jnp.asarray(smask))


def _gla(gq, gk, gv, gg):
    bsz, t, _ = gq.shape
    nt = t // TM
    consts = _gla_consts()
    fwd = lambda b, i: (b, i, 0)
    rev_blk = lambda i: jnp.where(i == 0, 0, nt - i)
    bwd = lambda b, i: (b, rev_blk(i), 0)
    bwd_g = lambda b, i: (b, rev_blk(i), 1)
    spec = lambda w, im: pl.BlockSpec((None, TM, w), im)
    full = lambda a: pl.BlockSpec(a.shape, lambda b, i: (0,) * a.ndim)
    return pl.pallas_call(
        _gla_kernel,
        grid=(bsz, nt),
        in_specs=[spec(GLA_QK, fwd), spec(GLA_QK, fwd), spec(GLA_WIDTH, fwd), spec(GLA_QK, fwd),
                  spec(GLA_QK, bwd), spec(GLA_QK, bwd), spec(GLA_WIDTH, bwd), spec(GLA_QK, bwd_g)]
                 + [full(a) for a in consts],
        out_specs=[spec(GLA_WIDTH, fwd), spec(GLA_WIDTH, bwd)],
        out_shape=[jax.ShapeDtypeStruct((bsz, t, GLA_WIDTH), F32)] * 2,
        scratch_shapes=[pltpu.VMEM((GLA_WIDTH, GLA_QK), F32)] * 2,
        compiler_params=_cparams(("parallel", "arbitrary")),
        name="gla",
    )(gq, gk, gv, gg, gq, gk, gv, gg, *consts)


def _head_values(vt_ref, h, cols, shared_kv):
    if shared_kv:
        r0 = (h // GQA_GROUP) * HEAD_DIM
        return vt_ref[r0:r0 + HEAD_DIM, cols]
    r0 = (h % 2) * MLA_V
    return vt_ref[h // 2, r0:r0 + MLA_V, cols]


def _write_heads(o_ref, outs_t):
    for p in range(len(outs_t) // 2):
        pair = jnp.concatenate([outs_t[2 * p], outs_t[2 * p + 1]], axis=0)
        o_ref[:, p * LANES:(p + 1) * LANES] = pair.T


def _attn_ctx_kernel(q_ref, k_ref, vt_ref, o_ref, *, shared_kv):
    heads = q_ref.shape[1] // LANES
    outs_t = []
    for h in range(heads):
        kh = k_ref[...] if shared_kv else k_ref[:, h * LANES:(h + 1) * LANES]
        s = _dot_nt(kh, q_ref[:, h * LANES:(h + 1) * LANES])
        p = jnp.exp2(s - jnp.max(s, axis=0, keepdims=True))
        l = jnp.sum(p, axis=0, keepdims=True)
        outs_t.append(_dot(_head_values(vt_ref, h, slice(None), shared_kv), p.astype(BF16)) / l)
    _write_heads(o_ref, outs_t)


def _attn_kernel(z_ref, qa_ref, qb_ref, k_ref, vt_ref, o_ref, s_ref, p_ref, *, shared_kv):
    t = k_ref.shape[0]
    heads = qa_ref.shape[1] // LANES
    chunks = [slice(a, a + ATT_CK) for a in range(0, t, ATT_CK)]
    z = z_ref[0]

    def dyn(start, size):
        return pl.ds(pl.multiple_of(z + start, ATT_SUB), size)

    def q_of(h):
        sl = slice(h * LANES, (h + 1) * LANES)
        return jnp.concatenate([qa_ref[:, sl], qb_ref[:, sl]], axis=0)

    def fold8(v):
        return v.reshape(v.shape[0] // 8, 8, v.shape[1])

    def qk(h, qh, rows):
        kh = k_ref[rows, :] if shared_kv else k_ref[rows, h * LANES:(h + 1) * LANES]
        return _dot_nt(kh, qh)

    def put_scores(s, rows, m8):
        s_ref[dyn(rows.start, ATT_CK), :] = s
        smax = jnp.max(fold8(s), axis=0)
        return smax if m8 is None else jnp.maximum(m8, smax)

    ones = jnp.ones((16, ATT_CK), BF16)
    m8 = None
    qh = q_of(0)
    for rows in chunks:
        m8 = put_scores(qk(0, qh, rows), rows, m8)
    outs_t = []
    for h in range(heads):
        mb = jnp.broadcast_to(jnp.max(m8, axis=0, keepdims=True), (ATT_SUB, m8.shape[1]))
        m8, acc = None, None
        more = h + 1 < heads
        if more:
            qh = q_of(h + 1)
            s_next = qk(h + 1, qh, chunks[0])
        for c, rows in enumerate(chunks):
            if more:
                s_cur = s_next
                if c + 1 < len(chunks):
                    s_next = qk(h + 1, qh, chunks[c + 1])
            vt1 = jnp.concatenate([_head_values(vt_ref, h, rows, shared_kv), ones], axis=0)
            p0 = (c % 2) * ATT_CK
            for r in range(0, ATT_CK, ATT_SUB):
                p = jnp.exp2(s_ref[dyn(rows.start + r, ATT_SUB), :] - mb)
                p_ref[dyn(p0 + r, ATT_SUB), :] = p.astype(BF16)
            if more:
                m8 = put_scores(s_cur, rows, m8)
            pv = _dot(vt1, p_ref[dyn(p0, ATT_CK), :])
            acc = pv if acc is None else acc + pv
        outs_t.append(acc[0:HEAD_DIM] / acc[HEAD_DIM:HEAD_DIM + 1])
    _write_heads(o_ref, outs_t)


def _attention(q, k, vt, shared_kv):
    bsz, t, width = q.shape
    n2 = (t - CTX_LEN) // (2 * TM)
    kw = k.shape[2]
    name = "attn_gqa" if shared_kv else "attn_mla"
    if shared_kv:
        v_spec = pl.BlockSpec((None, LANES, t), lambda b, j: (b, 0, 0))
        vc_spec = pl.BlockSpec((None, LANES, CTX_LEN), lambda b: (b, 0, 0))
    else:
        v_spec = pl.BlockSpec((None, vt.shape[1], LANES, t), lambda b, j: (b, 0, 0, 0))
        vc_spec = pl.BlockSpec((None, vt.shape[1], LANES, CTX_LEN), lambda b: (b, 0, 0, 0))
    y_lat = pl.pallas_call(
        functools.partial(_attn_kernel, shared_kv=shared_kv),
        grid=(bsz, n2),
        in_specs=[pl.BlockSpec(memory_space=pltpu.SMEM),
                  pl.BlockSpec((None, TM, width), lambda b, j: (b, 2 * j + 1, 0)),
                  pl.BlockSpec((None, TM, width), lambda b, j: (b, 2 * j + 2, 0)),
                  pl.BlockSpec((None, t, kw), lambda b, j: (b, 0, 0)), v_spec],
        out_specs=pl.BlockSpec((None, 2 * TM, width // 2), lambda b, j: (b, j, 0)),
        out_shape=jax.ShapeDtypeStruct((bsz, t - CTX_LEN, width // 2), F32),
        scratch_shapes=[pltpu.VMEM((t, 2 * TM), F32), pltpu.VMEM((2 * ATT_CK, 2 * TM), BF16)],
        compiler_params=_cparams(("parallel", "arbitrary")),
        name=name,
    )(jnp.zeros((1,), jnp.int32), q, q, k, vt)
    y_ctx = pl.pallas_call(
        functools.partial(_attn_ctx_kernel, shared_kv=shared_kv),
        grid=(bsz,),
        in_specs=[pl.BlockSpec((None, CTX_LEN, width), lambda b: (b, 0, 0)),
                  pl.BlockSpec((None, CTX_LEN, kw), lambda b: (b, 0, 0)), vc_spec],
        out_specs=pl.BlockSpec((None, CTX_LEN, width // 2), lambda b: (b, 0, 0)),
        out_shape=jax.ShapeDtypeStruct((bsz, CTX_LEN, width // 2), F32),
        compiler_params=_cparams(("parallel",)),
        name=name + "_ctx",
    )(q, k, vt)
    return y_lat, y_ctx


def _outproj_kernel(x_ref, of_ref, ob_ref, yb_ref, ybc_ref, yc_ref, ycc_ref, zg_ref, gnw_ref,
                    mod_ref, wout_ref, fnw_ref, o_ref, *, final):
    lo = lax.broadcasted_iota(jnp.int32, (TM, LANES), 1) < GLA_DV
    o = of_ref[...] + ob_ref[...]
    gnw = gnw_ref[...]
    zg = zg_ref[...]
    parts = []
    for j in range(GLA_WIDTH // LANES):
        sl = slice(j * LANES, (j + 1) * LANES)
        oj = o[:, sl]
        o2 = oj * oj
        ms_lo = jnp.sum(jnp.where(lo, o2, 0.0), axis=-1, keepdims=True) * (1.0 / GLA_DV)
        ms_hi = jnp.sum(jnp.where(lo, 0.0, o2), axis=-1, keepdims=True) * (1.0 / GLA_DV)
        inv = jnp.where(lo, lax.rsqrt(ms_lo + EPS), lax.rsqrt(ms_hi + EPS))
        parts.append(oj * inv * gnw[:, sl] * zg[:, sl])
    if final:
        yb, yc = yb_ref[...], yc_ref[...]
    else:
        is_ctx = pl.program_id(1) == 0
        yb = jnp.where(is_ctx, ybc_ref[...], yb_ref[...])
        yc = jnp.where(is_ctx, ycc_ref[...], yc_ref[...])
    parts.append(yb * zg[:, GLA_WIDTH:GLA_WIDTH + MLA_WIDTH])
    parts.append(yc * zg[:, GLA_WIDTH + MLA_WIDTH:])
    y = jnp.concatenate(parts, axis=-1).astype(BF16)
    gate = mod_ref[...]
    xn = x_ref[...] + gate * _dot(y, wout_ref[...])
    if final:
        xn = xn * lax.rsqrt(jnp.mean(xn * xn, axis=-1, keepdims=True) + EPS) * fnw_ref[...]
    o_ref[...] = xn


def _outproj(xs, o_f, o_b, y_b, y_c, zg, mod, lw, final_norm_w, final):
    bsz, t, _ = xs.shape
    off = 1 if final else 0
    nt = t // TM - off
    tok = lambda w: pl.BlockSpec((None, TM, w), lambda b, i: (b, i + off, 0))
    lat = lambda w: pl.BlockSpec((None, TM, w), lambda b, i: (b, jnp.maximum(i + off - 1, 0), 0))
    ctx = lambda w: pl.BlockSpec((None, CTX_LEN, w), lambda b, i: (b, 0, 0))
    full = lambda a: pl.BlockSpec(a.shape, lambda b, i: (0,) * a.ndim)
    return pl.pallas_call(
        functools.partial(_outproj_kernel, final=final),
        grid=(bsz, nt),
        in_specs=[tok(D_MODEL), tok(GLA_WIDTH), tok(GLA_WIDTH), lat(MLA_WIDTH), ctx(MLA_WIDTH),
                  lat(GQA_WIDTH), ctx(GQA_WIDTH), tok(D_MIX), full(lw["gla_norm"]),
                  pl.BlockSpec((None, 1, D_MODEL),
                               lambda b, i: (2 * b + jnp.minimum(i + off, 1), 0, 2)),
                  full(lw["w_out"]), full(final_norm_w)],
        out_specs=pl.BlockSpec((None, TM, D_MODEL), lambda b, i: (b, i, 0)),
        out_shape=jax.ShapeDtypeStruct((bsz, nt * TM, D_MODEL), F32),
        compiler_params=_cparams(("parallel", "arbitrary")),
        name="outproj",
    )(xs, o_f, o_b, *y_b, *y_c, zg, lw["gla_norm"], mod, lw["w_out"], final_norm_w)


def _rope_tables(rows, d_rot, lane0):
    quarter = d_rot // 4
    n = rows * GRID_W
    row = jnp.repeat(jnp.arange(rows), GRID_W).astype(F32)
    col = jnp.tile(jnp.arange(GRID_W), rows).astype(F32)
    freqs = ROPE_THETA ** (-jnp.arange(quarter, dtype=F32) / quarter)
    ang = jnp.stack([row[:, None] * freqs, col[:, None] * freqs], axis=1)
    cos, sin = jnp.cos(ang), jnp.sin(ang)
    zero = jnp.zeros_like(sin)
    cos_l = jnp.stack([cos, cos], axis=2).reshape(n, d_rot)
    sa_l = jnp.stack([-sin, zero], axis=2).reshape(n, d_rot)
    sb_l = jnp.stack([zero, sin], axis=2).reshape(n, d_rot)

    def place(v, fill):
        out = jnp.full((n, LANES), fill, F32)
        return out.at[:, lane0:lane0 + d_rot].set(v)

    c_fill = 1.0 if lane0 > 0 else 0.0
    lat = jnp.concatenate([place(cos_l, c_fill), place(sa_l, 0.0), place(sb_l, 0.0)], axis=1)
    ident = jnp.concatenate([place(jnp.ones((n, d_rot), F32), c_fill)[:CTX_LEN],
                             jnp.zeros((CTX_LEN, 2 * LANES), F32)], axis=1)
    return jnp.concatenate([ident, lat], axis=0)


def _layer_weights(l, norm_w, w_in, gla_w_a_fwd, gla_b_a_fwd, gla_w_a_bwd, gla_b_a_bwd, gla_norm_w,
                   mla_q_norm_w, mla_w_uq, mla_kv_norm_w, mla_w_ukv, gqa_q_norm_w, gqa_k_norm_w,
                   w_out):
    w = w_in[l]
    zc = lambda n: jnp.zeros((D_MODEL, n), F32)
    c0 = 2 * GLA_QK + 2 * GLA_WIDTH
    m0 = c0 + 2 * GLA_GATE_RANK
    g0 = m0 + MLA_Q_LORA + MLA_KV_LORA + MLA_ROPE + MLA_WIDTH
    seg_a = [w[:, 0:2 * GLA_QK + GLA_WIDTH], w[:, c0:m0], zc(LANES - 2 * GLA_GATE_RANK)]
    kr0 = m0 + MLA_Q_LORA + MLA_KV_LORA
    seg_b = [w[:, m0:kr0], zc(MLA_NOPE), w[:, kr0:kr0 + MLA_ROPE], zc(LANES - MLA_NOPE - MLA_ROPE)]
    seg_c = [w[:, g0:g0 + GQA_WIDTH + 2 * LANES]]
    seg_z = [w[:, 2 * GLA_QK + GLA_WIDTH:c0], w[:, kr0 + MLA_ROPE:g0], w[:, g0 + GQA_WIDTH + 2 * LANES:]]
    w_pad = jnp.concatenate(seg_a + seg_b + seg_c + seg_z, axis=1).astype(BF16)

    w_a = jnp.zeros((LANES, 2 * GLA_QK), F32)
    w_a = w_a.at[0:GLA_GATE_RANK, 0:GLA_QK].set(gla_w_a_fwd[l])
    w_a = w_a.at[GLA_GATE_RANK:2 * GLA_GATE_RANK, GLA_QK:].set(gla_w_a_bwd[l])
    b_a = jnp.concatenate([gla_b_a_fwd[l], gla_b_a_bwd[l]])[None, :]

    dq = MLA_NOPE + MLA_ROPE
    uq = mla_w_uq[l].reshape(MLA_Q_LORA, MLA_HEADS, dq)
    uq = jnp.pad(uq, ((0, 0), (0, 0), (0, LANES - dq))).reshape(MLA_Q_LORA, MLA_HEADS * LANES)
    ukv = mla_w_ukv[l].reshape(MLA_KV_LORA, MLA_HEADS, MLA_NOPE + MLA_V)
    uk = jnp.pad(ukv[:, :, :MLA_NOPE], ((0, 0), (0, 0), (0, LANES - MLA_NOPE)))
    uk = uk.reshape(MLA_KV_LORA, MLA_HEADS * LANES)
    uv = ukv[:, :, MLA_NOPE:].reshape(MLA_KV_LORA, MLA_WIDTH)

    pad_lane = lambda v: jnp.concatenate([v, jnp.zeros((LANES - v.shape[0],), F32)])[None, :]
    return {
        "norm_w": norm_w[l][None, :],
        "w_in": w_pad,
        "w_a": w_a.astype(BF16),
        "b_a": b_a,
        "mla_q_norm": mla_q_norm_w[l][None, :],
        "w_uq": uq.astype(BF16),
        "mla_kv_norm": mla_kv_norm_w[l][None, :],
        "w_uk": uk.astype(BF16),
        "w_uv": uv.astype(BF16),
        "gqa_q_norm": pad_lane(gqa_q_norm_w[l]),
        "gqa_k_norm": jnp.tile(gqa_k_norm_w[l], GQA_KV_HEADS)[None, :],
        "gla_norm": jnp.tile(gla_norm_w[l], GLA_HEADS)[None, :],
        "w_out": w_out[l].astype(BF16),
    }


def kernel(x, c, ctx, c_ctx, norm_w, w_mod, b_mod, w_in, gla_w_a_fwd, gla_b_a_fwd, gla_w_a_bwd,
           gla_b_a_bwd, gla_norm_w, mla_q_norm_w, mla_w_uq, mla_kv_norm_w, mla_w_ukv,
           gqa_q_norm_w, gqa_k_norm_w, w_out, final_norm_w):
    bsz, n, _ = x.shape
    depth = w_in.shape[0]
    assert ctx.shape[1] == CTX_LEN and n % (2 * TM) == 0 and n % GRID_W == 0
    rows = n // GRID_W
    rope_m = _rope_tables(rows, MLA_ROPE, MLA_NOPE)
    rope_g = _rope_tables(rows, HEAD_DIM, 0)

    mod_rows = -(-(bsz + 1) // 8) * 8
    cs = jnp.zeros((mod_rows, D_MODEL), F32).at[:bsz].set(c).at[bsz].set(c_ctx)
    mod_all = _modulation(cs, w_mod, b_mod)
    fnw = final_norm_w[None, :]

    xs = jnp.concatenate([ctx, x], axis=1)
    for l in range(depth):
        lw = _layer_weights(l, norm_w, w_in, gla_w_a_fwd, gla_b_a_fwd, gla_w_a_bwd, gla_b_a_bwd,
                            gla_norm_w, mla_q_norm_w, mla_w_uq, mla_kv_norm_w, mla_w_ukv,
                            gqa_q_norm_w, gqa_k_norm_w, w_out)
        m = mod_all[l]
        mod = jnp.stack([jnp.broadcast_to(m[bsz], (bsz, 3 * D_MODEL)), m[:bsz]], axis=1)
        mod = mod.reshape(2 * bsz, 1, 3 * D_MODEL)
        gq, gk, gv, gg, zg, mq, mk, mvt, cq, ck, cvt = _prep(xs, mod, lw, rope_m, rope_g)
        o_f, o_b = _gla(gq, gk, gv, gg)
        y_b = _attention(mq, mk, mvt, shared_kv=False)
        y_c = _attention(cq, ck, cvt, shared_kv=True)
        xs = _outproj(xs, o_f, o_b, y_b, y_c, zg, mod, lw, fnw, final=(l == depth - 1))
    return xs
```

```python
import functools

import numpy as np
import jax
import jax.numpy as jnp
from jax import lax
from jax.experimental import pallas as pl
from jax.experimental.pallas import tpu as pltpu

D_MODEL = 1024
GRID_W = 64
CTX_LEN = 256
HEAD_DIM = 64
ROPE_THETA = 10000.0
EPS = 1e-6

GLA_HEADS = 4
GLA_DK = 32
GLA_DV = 64
GLA_WIDTH = GLA_HEADS * GLA_DV
GLA_QK = GLA_HEADS * GLA_DK
GLA_GATE_RANK = 16
GLA_GATE_NORM = 16.0
GLA_CHUNK = 64

MLA_HEADS = 6
MLA_NOPE = 64
MLA_ROPE = 32
MLA_V = 64
MLA_Q_LORA = 256
MLA_KV_LORA = 256
MLA_WIDTH = MLA_HEADS * MLA_V

GQA_HEADS = 6
GQA_KV_HEADS = 2
GQA_GROUP = GQA_HEADS // GQA_KV_HEADS
GQA_WIDTH = GQA_HEADS * HEAD_DIM

D_MIX = GLA_WIDTH + MLA_WIDTH + GQA_WIDTH

LANES = 128
TM = CTX_LEN
CHUNKS = TM // GLA_CHUNK
SEG = 640
W_IN_PAD = 3 * SEG + D_MIX
VMEM_LIMIT = 48 * 1024 * 1024
ATT_CK = 256
ATT_SUB = 32
LOG2E = 1.4426950408889634

F32 = jnp.float32
BF16 = jnp.bfloat16


def _cparams(sem):
    return pltpu.CompilerParams(dimension_semantics=sem, vmem_limit_bytes=VMEM_LIMIT)


def _dot(a, b):
    return jnp.dot(a, b, preferred_element_type=F32)


def _dot_nt(a, b):
    return lax.dot_general(a, b, (((1,), (1,)), ((), ())), preferred_element_type=F32)


def _dot_tn(a, b):
    return lax.dot_general(a, b, (((0,), (0,)), ((), ())), preferred_element_type=F32)


def _silu(v):
    return v / (1.0 + jnp.exp(-v))


def _mod_kernel(c_ref, w_ref, b_ref, o_ref):
    s = _silu(c_ref[...]).astype(BF16)
    o_ref[...] = _dot(s, w_ref[...].astype(BF16)) + b_ref[...]


def _modulation(cs, w_mod, b_mod):
    depth = w_mod.shape[0]
    rows = cs.shape[0]
    return pl.pallas_call(
        _mod_kernel,
        grid=(depth, 3),
        in_specs=[
            pl.BlockSpec((rows, D_MODEL), lambda l, j: (0, 0)),
            pl.BlockSpec((None, D_MODEL, D_MODEL), lambda l, j: (l, 0, j)),
            pl.BlockSpec((None, 1, D_MODEL), lambda l, j: (l, 0, j)),
        ],
        out_specs=pl.BlockSpec((None, rows, D_MODEL), lambda l, j: (l, 0, j)),
        out_shape=jax.ShapeDtypeStruct((depth, rows, 3 * D_MODEL), F32),
        compiler_params=_cparams(("arbitrary", "arbitrary")),
        name="modulation",
    )(cs, w_mod, b_mod.reshape(depth, 1, 3 * D_MODEL))


def _rope(t, tab, shift):
    c, sa, sb = tab[:, :LANES], tab[:, LANES:2 * LANES], tab[:, 2 * LANES:]
    return t * c + pltpu.roll(t, LANES - shift, 1) * sa + pltpu.roll(t, shift, 1) * sb


def _prep_kernel(x_ref, mod_ref, nw_ref, win_ref, wa_ref, ba_ref, mqn_ref, wuq_ref, mkvn_ref,
                 wuk_ref, wuv_ref, gqn_ref, gkn_ref, rm_ref, rg_ref,
                 gq_ref, gk_ref, gv_ref, gg_ref, zg_ref, mq_ref, mk_ref, mvt_ref, cq_ref, ck_ref,
                 cvt_ref):
    x = x_ref[...]
    mod = mod_ref[...]
    shift, scale = mod[:, :D_MODEL], mod[:, D_MODEL:2 * D_MODEL]
    ms = jnp.mean(x * x, axis=-1, keepdims=True)
    h = (x * lax.rsqrt(ms + EPS) * nw_ref[...]) * (1.0 + scale) + shift
    hb = h.astype(BF16)

    uz = _dot(hb, win_ref[:, 3 * SEG:])
    zg_ref[...] = _silu(uz)

    ua = _dot(hb, win_ref[:, 0:SEG])
    gq_ref[...] = (ua[:, 0:GLA_QK] * GLA_DK ** -0.5).astype(BF16)
    gk_ref[...] = ua[:, GLA_QK:2 * GLA_QK].astype(BF16)
    gv_ref[...] = ua[:, 2 * GLA_QK:2 * GLA_QK + GLA_WIDTH].astype(BF16)
    a = ua[:, 2 * GLA_QK + GLA_WIDTH:SEG].astype(BF16)
    xg = _dot(a, wa_ref[...]) + ba_ref[...]
    log_sig = jnp.minimum(xg, 0.0) - jnp.log1p(jnp.exp(-jnp.abs(xg)))
    gg_ref[...] = log_sig * (1.0 / GLA_GATE_NORM)

    ub = _dot(hb, win_ref[:, SEG:2 * SEG])
    cq = ub[:, 0:MLA_Q_LORA]
    ckv = ub[:, MLA_Q_LORA:MLA_Q_LORA + MLA_KV_LORA]
    kr = ub[:, MLA_Q_LORA + MLA_KV_LORA:SEG]
    cqn = cq * lax.rsqrt(jnp.mean(cq * cq, axis=-1, keepdims=True) + EPS) * mqn_ref[...]
    ckvn = ckv * lax.rsqrt(jnp.mean(ckv * ckv, axis=-1, keepdims=True) + EPS) * mkvn_ref[...]
    qm = _dot(cqn.astype(BF16), wuq_ref[...])
    ckvb = ckvn.astype(BF16)
    kn = _dot(ckvb, wuk_ref[...])
    vm = _dot(ckvb, wuv_ref[...])
    rm = rm_ref[...]
    krr = _rope(kr, rm, MLA_ROPE // 4)
    mla_scale = (MLA_NOPE + MLA_ROPE) ** -0.5 * LOG2E
    for hd in range(MLA_HEADS):
        sl = slice(hd * LANES, (hd + 1) * LANES)
        mq_ref[:, sl] = (_rope(qm[:, sl], rm, MLA_ROPE // 4) * mla_scale).astype(BF16)
        mk_ref[:, sl] = (kn[:, sl] + krr).astype(BF16)
    for p in range(MLA_HEADS // 2):
        mvt_ref[p] = vm[:, p * LANES:(p + 1) * LANES].T.astype(BF16)

    uc = _dot(hb, win_ref[:, 2 * SEG:3 * SEG])
    rg = rg_ref[...]
    lo = lax.broadcasted_iota(jnp.int32, (TM, LANES), 1) < HEAD_DIM
    gqn = gqn_ref[...]
    for hd in range(GQA_HEADS):
        blk = uc[:, (hd // 2) * LANES:(hd // 2 + 1) * LANES]
        if hd % 2 == 1:
            blk = pltpu.roll(blk, HEAD_DIM, 1)
        xh = jnp.where(lo, blk, 0.0)
        msq = jnp.sum(xh * xh, axis=-1, keepdims=True) * (1.0 / HEAD_DIM)
        xh = xh * lax.rsqrt(msq + EPS) * gqn
        xh = _rope(xh, rg, HEAD_DIM // 4) * (HEAD_DIM ** -0.5 * LOG2E)
        if hd // GQA_GROUP == 1:
            xh = pltpu.roll(xh, HEAD_DIM, 1)
        cq_ref[:, hd * LANES:(hd + 1) * LANES] = xh.astype(BF16)
    k = uc[:, GQA_WIDTH:GQA_WIDTH + LANES]
    k2 = k * k
    ms_lo = jnp.sum(jnp.where(lo, k2, 0.0), axis=-1, keepdims=True) * (1.0 / HEAD_DIM)
    ms_hi = jnp.sum(jnp.where(lo, 0.0, k2), axis=-1, keepdims=True) * (1.0 / HEAD_DIM)
    inv = jnp.where(lo, lax.rsqrt(ms_lo + EPS), lax.rsqrt(ms_hi + EPS))
    kg = k * inv * gkn_ref[...]
    rg2 = jnp.concatenate([rg[:, j * LANES:(j + 1) * LANES]
                           + pltpu.roll(rg[:, j * LANES:(j + 1) * LANES], HEAD_DIM, 1)
                           for j in range(3)], axis=1)
    ck_ref[...] = _rope(kg, rg2, HEAD_DIM // 4).astype(BF16)
    cvt_ref[...] = uc[:, GQA_WIDTH + LANES:SEG].T.astype(BF16)


def _prep(xs, mod, lw, rope_m, rope_g):
    bsz, t, _ = xs.shape
    nt = t // TM
    tok = lambda w: pl.BlockSpec((None, TM, w), lambda b, i: (b, i, 0))
    full = lambda a: pl.BlockSpec(a.shape, lambda b, i: (0,) * a.ndim)
    tab = pl.BlockSpec((TM, 3 * LANES), lambda b, i: (i, 0))
    weights = [lw[n] for n in ("norm_w", "w_in", "w_a", "b_a", "mla_q_norm", "w_uq", "mla_kv_norm",
                               "w_uk", "w_uv", "gqa_q_norm", "gqa_k_norm")]
    out_shape = [
        jax.ShapeDtypeStruct((bsz, t, GLA_QK), BF16),
        jax.ShapeDtypeStruct((bsz, t, GLA_QK), BF16),
        jax.ShapeDtypeStruct((bsz, t, GLA_WIDTH), BF16),
        jax.ShapeDtypeStruct((bsz, t, 2 * GLA_QK), F32),
        jax.ShapeDtypeStruct((bsz, t, D_MIX), F32),
        jax.ShapeDtypeStruct((bsz, t, MLA_HEADS * LANES), BF16),
        jax.ShapeDtypeStruct((bsz, t, MLA_HEADS * LANES), BF16),
        jax.ShapeDtypeStruct((bsz, MLA_HEADS // 2, LANES, t), BF16),
        jax.ShapeDtypeStruct((bsz, t, GQA_HEADS * LANES), BF16),
        jax.ShapeDtypeStruct((bsz, t, LANES), BF16),
        jax.ShapeDtypeStruct((bsz, LANES, t), BF16),
    ]
    out_specs = [
        tok(GLA_QK), tok(GLA_QK), tok(GLA_WIDTH), tok(2 * GLA_QK), tok(D_MIX),
        tok(MLA_HEADS * LANES), tok(MLA_HEADS * LANES),
        pl.BlockSpec((None, MLA_HEADS // 2, LANES, TM), lambda b, i: (b, 0, 0, i)),
        tok(GQA_HEADS * LANES), tok(LANES),
        pl.BlockSpec((None, LANES, TM), lambda b, i: (b, 0, i)),
    ]
    return pl.pallas_call(
        _prep_kernel,
        grid=(bsz, nt),
        in_specs=[tok(D_MODEL),
                  pl.BlockSpec((None, 1, 3 * D_MODEL), lambda b, i: (2 * b + jnp.minimum(i, 1), 0, 0))]
                 + [full(w) for w in weights] + [tab, tab],
        out_specs=out_specs,
        out_shape=out_shape,
        compiler_params=_cparams(("parallel", "arbitrary")),
        name="prep",
    )(xs, mod, *weights, rope_m, rope_g)


def _gla_direction(q_ref, k_ref, v_ref, g_ref, l_ref, cmask_ref, qmask, omask, smask, s_ref, o_ref,
                   reverse):
    q = q_ref[...].astype(F32)
    k = k_ref[...].astype(F32)
    g = g_ref[...]
    g_hi = g.astype(BF16)
    r1 = g - g_hi.astype(F32)
    g_mid = r1.astype(BF16)
    g_lo = (r1 - g_mid.astype(F32)).astype(BF16)
    tri = l_ref[...]
    cum = _dot(tri, g_hi) + _dot(tri, g_mid) + _dot(tri, g_lo)
    cmask = cmask_ref[...]
    order = range(CHUNKS - 1, -1, -1) if reverse else range(CHUNKS)
    for c in order:
        sl = slice(c * GLA_CHUNK, (c + 1) * GLA_CHUNK)
        cum_c = cum[sl]
        last = cum_c[0:1] if reverse else cum_c[GLA_CHUNK - 1:GLA_CHUNK]
        qd = q[sl] * jnp.exp(cum_c)
        ki = k[sl] * jnp.exp(-cum_c)
        kd = k[sl] * jnp.exp(last - cum_c)
        vc = v_ref[sl, :]
        qh = jnp.concatenate([qd] * GLA_HEADS, axis=0) * qmask
        att = _dot_nt(qh.astype(BF16), ki.astype(BF16)) * cmask
        o_all = _dot(att.astype(BF16), vc) * omask
        o_intra = o_all[0:GLA_CHUNK]
        for hd in range(1, GLA_HEADS):
            o_intra = o_intra + o_all[hd * GLA_CHUNK:(hd + 1) * GLA_CHUNK]
        st = s_ref[...]
        o_inter = _dot_nt(qd.astype(BF16), st.astype(BF16))
        o_ref[sl, :] = o_inter + o_intra
        ds = _dot_tn(vc, kd.astype(BF16)) * smask
        s_ref[...] = st * jnp.exp(last) + ds


def _gla_kernel(qf_ref, kf_ref, vf_ref, gf_ref, qb_ref, kb_ref, vb_ref, gb_ref, lf_ref, lb_ref,
                cmf_ref, cmb_ref, qmask_ref, omask_ref, smask_ref, of_ref, ob_ref, sf_ref, sb_ref):
    @pl.when(pl.program_id(1) == 0)
    def _():
        sf_ref[...] = jnp.zeros_like(sf_ref)
        sb_ref[...] = jnp.zeros_like(sb_ref)

    qmask, omask, smask = qmask_ref[...], omask_ref[...], smask_ref[...]
    _gla_direction(qf_ref, kf_ref, vf_ref, gf_ref, lf_ref, cmf_ref, qmask, omask, smask, sf_ref,
                   of_ref, False)
    _gla_direction(qb_ref, kb_ref, vb_ref, gb_ref, lb_ref, cmb_ref, qmask, omask, smask, sb_ref,
                   ob_ref, True)


def _gla_consts():
    r = np.arange(TM)
    same = (r[:, None] // GLA_CHUNK) == (r[None, :] // GLA_CHUNK)
    lf = (same & (r[None, :] <= r[:, None])).astype(np.float32)
    lb = (same & (r[None, :] >= r[:, None])).astype(np.float32)
    rows = np.arange(GLA_HEADS * GLA_CHUNK)
    j = np.arange(GLA_CHUNK)
    cmf = ((rows[:, None] % GLA_CHUNK) >= j[None, :]).astype(np.float32)
    cmb = ((rows[:, None] % GLA_CHUNK) <= j[None, :]).astype(np.float32)
    qmask = ((rows[:, None] // GLA_CHUNK) == (np.arange(GLA_QK)[None, :] // GLA_DK)).astype(np.float32)
    omask = ((rows[:, None] // GLA_CHUNK) == (np.arange(GLA_WIDTH)[None, :] // GLA_DV)).astype(np.float32)
    smask = ((np.arange(GLA_WIDTH)[:, None] // GLA_DV) == (np.arange(GLA_QK)[None, :] // GLA_DK)).astype(np.float32)
    return (jnp.asarray(lf, BF16), jnp.asarray(lb, BF16), jnp.asarray(cmf), jnp.asarray(cmb),
            jnp.asarray(qmask), jnp.asarray(omask), jnp.asarray(smask))


def _gla(gq, gk, gv, gg):
    bsz, t, _ = gq.shape
    nt = t // TM
    consts = _gla_consts()
    fwd = lambda b, i: (b, i, 0)
    rev_blk = lambda i: jnp.where(i == 0, 0, nt - i)
    bwd = lambda b, i: (b, rev_blk(i), 0)
    bwd_g = lambda b, i: (b, rev_blk(i), 1)
    spec = lambda w, im: pl.BlockSpec((None, TM, w), im)
    full = lambda a: pl.BlockSpec(a.shape, lambda b, i: (0,) * a.ndim)
    return pl.pallas_call(
        _gla_kernel,
        grid=(bsz, nt),
        in_specs=[spec(GLA_QK, fwd), spec(GLA_QK, fwd), spec(GLA_WIDTH, fwd), spec(GLA_QK, fwd),
                  spec(GLA_QK, bwd), spec(GLA_QK, bwd), spec(GLA_WIDTH, bwd), spec(GLA_QK, bwd_g)]
                 + [full(a) for a in consts],
        out_specs=[spec(GLA_WIDTH, fwd), spec(GLA_WIDTH, bwd)],
        out_shape=[jax.ShapeDtypeStruct((bsz, t, GLA_WIDTH), F32)] * 2,
        scratch_shapes=[pltpu.VMEM((GLA_WIDTH, GLA_QK), F32)] * 2,
        compiler_params=_cparams(("parallel", "arbitrary")),
        name="gla",
    )(gq, gk, gv, gg, gq, gk, gv, gg, *consts)


def _head_values(vt_ref, h, cols, shared_kv):
    if shared_kv:
        r0 = (h // GQA_GROUP) * HEAD_DIM
        return vt_ref[r0:r0 + HEAD_DIM, cols]
    r0 = (h % 2) * MLA_V
    return vt_ref[h // 2, r0:r0 + MLA_V, cols]


def _write_heads(o_ref, outs_t):
    for p in range(len(outs_t) // 2):
        pair = jnp.concatenate([outs_t[2 * p], outs_t[2 * p + 1]], axis=0)
        o_ref[:, p * LANES:(p + 1) * LANES] = pair.T


def _attn_ctx_kernel(q_ref, k_ref, vt_ref, o_ref, *, shared_kv):
    heads = q_ref.shape[1] // LANES
    outs_t = []
    for h in range(heads):
        kh = k_ref[...] if shared_kv else k_ref[:, h * LANES:(h + 1) * LANES]
        s = _dot_nt(kh, q_ref[:, h * LANES:(h + 1) * LANES])
        p = jnp.exp2(s - jnp.max(s, axis=0, keepdims=True))
        l = jnp.sum(p, axis=0, keepdims=True)
        outs_t.append(_dot(_head_values(vt_ref, h, slice(None), shared_kv), p.astype(BF16)) / l)
    _write_heads(o_ref, outs_t)


def _attn_kernel(z_ref, qa_ref, qb_ref, k_ref, vt_ref, o_ref, s_ref, p_ref, *, shared_kv):
    t = k_ref.shape[0]
    heads = qa_ref.shape[1] // LANES
    chunks = [slice(a, a + ATT_CK) for a in range(0, t, ATT_CK)]
    z = z_ref[0]

    def dyn(start, size):
        return pl.ds(pl.multiple_of(z + start, ATT_SUB), size)

    def q_of(h):
        sl = slice(h * LANES, (h + 1) * LANES)
        qh = jnp.concatenate([qa_ref[:, sl], qb_ref[:, sl]], axis=0)
        return qh.astype(F32).T.astype(BF16)

    def fold8(v):
        return v.reshape(v.shape[0] // 8, 8, v.shape[1])

    def qk(h, qh, rows):
        kh = k_ref[rows, :] if shared_kv else k_ref[rows, h * LANES:(h + 1) * LANES]
        return _dot(kh, qh)

    def put_scores(s, rows, m8):
        s_ref[dyn(rows.start, ATT_CK), :] = s
        smax = jnp.max(fold8(s), axis=0)
        return smax if m8 is None else jnp.maximum(m8, smax)

    ones = jnp.ones((16, ATT_CK), BF16)
    m8 = None
    qh = q_of(0)
    for rows in chunks:
        m8 = put_scores(qk(0, qh, rows), rows, m8)
    outs_t = []
    for h in range(heads):
        mb = jnp.broadcast_to(jnp.max(m8, axis=0, keepdims=True), (ATT_SUB, m8.shape[1]))
        m8, acc = None, None
        more = h + 1 < heads
        if more:
            qh = q_of(h + 1)
            s_next = qk(h + 1, qh, chunks[0])
        for c, rows in enumerate(chunks):
            if more:
                s_cur = s_next
                if c + 1 < len(chunks):
                    s_next = qk(h + 1, qh, chunks[c + 1])
            vt1 = jnp.concatenate([_head_values(vt_ref, h, rows, shared_kv), ones], axis=0)
            p0 = (c % 2) * ATT_CK
            for r in range(0, ATT_CK, ATT_SUB):
                p = jnp.exp2(s_ref[dyn(rows.start + r, ATT_SUB), :] - mb)
                p_ref[dyn(p0 + r, ATT_SUB), :] = p.astype(BF16)
            if more:
                m8 = put_scores(s_cur, rows, m8)
            pv = _dot(vt1, p_ref[dyn(p0, ATT_CK), :])
            acc = pv if acc is None else acc + pv
        outs_t.append(acc[0:HEAD_DIM] / acc[HEAD_DIM:HEAD_DIM + 1])
    _write_heads(o_ref, outs_t)


def _attention(q, k, vt, shared_kv):
    bsz, t, width = q.shape
    n2 = (t - CTX_LEN) // (2 * TM)
    kw = k.shape[2]
    name = "attn_gqa" if shared_kv else "attn_mla"
    if shared_kv:
        v_spec = pl.BlockSpec((None, LANES, t), lambda b, j: (b, 0, 0))
        vc_spec = pl.BlockSpec((None, LANES, CTX_LEN), lambda b: (b, 0, 0))
    else:
        v_spec = pl.BlockSpec((None, vt.shape[1], LANES, t), lambda b, j: (b, 0, 0, 0))
        vc_spec = pl.BlockSpec((None, vt.shape[1], LANES, CTX_LEN), lambda b: (b, 0, 0, 0))
    y_lat = pl.pallas_call(
        functools.partial(_attn_kernel, shared_kv=shared_kv),
        grid=(bsz, n2),
        in_specs=[pl.BlockSpec(memory_space=pltpu.SMEM),
                  pl.BlockSpec((None, TM, width), lambda b, j: (b, 2 * j + 1, 0)),
                  pl.BlockSpec((None, TM, width), lambda b, j: (b, 2 * j + 2, 0)),
                  pl.BlockSpec((None, t, kw), lambda b, j: (b, 0, 0)), v_spec],
        out_specs=pl.BlockSpec((None, 2 * TM, width // 2), lambda b, j: (b, j, 0)),
        out_shape=jax.ShapeDtypeStruct((bsz, t - CTX_LEN, width // 2), F32),
        scratch_shapes=[pltpu.VMEM((t, 2 * TM), F32), pltpu.VMEM((2 * ATT_CK, 2 * TM), BF16)],
        compiler_params=_cparams(("parallel", "arbitrary")),
        name=name,
    )(jnp.zeros((1,), jnp.int32), q, q, k, vt)
    y_ctx = pl.pallas_call(
        functools.partial(_attn_ctx_kernel, shared_kv=shared_kv),
        grid=(bsz,),
        in_specs=[pl.BlockSpec((None, CTX_LEN, width), lambda b: (b, 0, 0)),
                  pl.BlockSpec((None, CTX_LEN, kw), lambda b: (b, 0, 0)), vc_spec],
        out_specs=pl.BlockSpec((None, CTX_LEN, width // 2), lambda b: (b, 0, 0)),
        out_shape=jax.ShapeDtypeStruct((bsz, CTX_LEN, width // 2), F32),
        compiler_params=_cparams(("parallel",)),
        name=name + "_ctx",
    )(q, k, vt)
    return y_lat, y_ctx


def _outproj_kernel(x_ref, of_ref, ob_ref, yb_ref, ybc_ref, yc_ref, ycc_ref, zg_ref, gnw_ref,
                    mod_ref, wout_ref, fnw_ref, o_ref, *, final):
    lo = lax.broadcasted_iota(jnp.int32, (TM, LANES), 1) < GLA_DV
    o = of_ref[...] + ob_ref[...]
    gnw = gnw_ref[...]
    zg = zg_ref[...]
    parts = []
    for j in range(GLA_WIDTH // LANES):
        sl = slice(j * LANES, (j + 1) * LANES)
        oj = o[:, sl]
        o2 = oj * oj
        ms_lo = jnp.sum(jnp.where(lo, o2, 0.0), axis=-1, keepdims=True) * (1.0 / GLA_DV)
        ms_hi = jnp.sum(jnp.where(lo, 0.0, o2), axis=-1, keepdims=True) * (1.0 / GLA_DV)
        inv = jnp.where(lo, lax.rsqrt(ms_lo + EPS), lax.rsqrt(ms_hi + EPS))
        parts.append(oj * inv * gnw[:, sl] * zg[:, sl])
    if final:
        yb, yc = yb_ref[...], yc_ref[...]
    else:
        is_ctx = pl.program_id(1) == 0
        yb = jnp.where(is_ctx, ybc_ref[...], yb_ref[...])
        yc = jnp.where(is_ctx, ycc_ref[...], yc_ref[...])
    parts.append(yb * zg[:, GLA_WIDTH:GLA_WIDTH + MLA_WIDTH])
    parts.append(yc * zg[:, GLA_WIDTH + MLA_WIDTH:])
    y = jnp.concatenate(parts, axis=-1).astype(BF16)
    gate = mod_ref[...]
    xn = x_ref[...] + gate * _dot(y, wout_ref[...])
    if final:
        xn = xn * lax.rsqrt(jnp.mean(xn * xn, axis=-1, keepdims=True) + EPS) * fnw_ref[...]
    o_ref[...] = xn


def _outproj(xs, o_f, o_b, y_b, y_c, zg, mod, lw, final_norm_w, final):
    bsz, t, _ = xs.shape
    off = 1 if final else 0
    nt = t // TM - off
    tok = lambda w: pl.BlockSpec((None, TM, w), lambda b, i: (b, i + off, 0))
    lat = lambda w: pl.BlockSpec((None, TM, w), lambda b, i: (b, jnp.maximum(i + off - 1, 0), 0))
    ctx = lambda w: pl.BlockSpec((None, CTX_LEN, w), lambda b, i: (b, 0, 0))
    full = lambda a: pl.BlockSpec(a.shape, lambda b, i: (0,) * a.ndim)
    return pl.pallas_call(
        functools.partial(_outproj_kernel, final=final),
        grid=(bsz, nt),
        in_specs=[tok(D_MODEL), tok(GLA_WIDTH), tok(GLA_WIDTH), lat(MLA_WIDTH), ctx(MLA_WIDTH),
                  lat(GQA_WIDTH), ctx(GQA_WIDTH), tok(D_MIX), full(lw["gla_norm"]),
                  pl.BlockSpec((None, 1, D_MODEL),
                               lambda b, i: (2 * b + jnp.minimum(i + off, 1), 0, 2)),
                  full(lw["w_out"]), full(final_norm_w)],
        out_specs=pl.BlockSpec((None, TM, D_MODEL), lambda b, i: (b, i, 0)),
        out_shape=jax.ShapeDtypeStruct((bsz, nt * TM, D_MODEL), F32),
        compiler_params=_cparams(("parallel", "arbitrary")),
        name="outproj",
    )(xs, o_f, o_b, *y_b, *y_c, zg, lw["gla_norm"], mod, lw["w_out"], final_norm_w)


def _rope_tables(rows, d_rot, lane0):
    quarter = d_rot // 4
    n = rows * GRID_W
    row = jnp.repeat(jnp.arange(rows), GRID_W).astype(F32)
    col = jnp.tile(jnp.arange(GRID_W), rows).astype(F32)
    freqs = ROPE_THETA ** (-jnp.arange(quarter, dtype=F32) / quarter)
    ang = jnp.stack([row[:, None] * freqs, col[:, None] * freqs], axis=1)
    cos, sin = jnp.cos(ang), jnp.sin(ang)
    zero = jnp.zeros_like(sin)
    cos_l = jnp.stack([cos, cos], axis=2).reshape(n, d_rot)
    sa_l = jnp.stack([-sin, zero], axis=2).reshape(n, d_rot)
    sb_l = jnp.stack([zero, sin], axis=2).reshape(n, d_rot)

    def place(v, fill):
        out = jnp.full((n, LANES), fill, F32)
        return out.at[:, lane0:lane0 + d_rot].set(v)

    c_fill = 1.0 if lane0 > 0 else 0.0
    lat = jnp.concatenate([place(cos_l, c_fill), place(sa_l, 0.0), place(sb_l, 0.0)], axis=1)
    ident = jnp.concatenate([place(jnp.ones((n, d_rot), F32), c_fill)[:CTX_LEN],
                             jnp.zeros((CTX_LEN, 2 * LANES), F32)], axis=1)
    return jnp.concatenate([ident, lat], axis=0)


def _layer_weights(l, norm_w, w_in, gla_w_a_fwd, gla_b_a_fwd, gla_w_a_bwd, gla_b_a_bwd, gla_norm_w,
                   mla_q_norm_w, mla_w_uq, mla_kv_norm_w, mla_w_ukv, gqa_q_norm_w, gqa_k_norm_w,
                   w_out):
    w = w_in[l]
    zc = lambda n: jnp.zeros((D_MODEL, n), F32)
    c0 = 2 * GLA_QK + 2 * GLA_WIDTH
    m0 = c0 + 2 * GLA_GATE_RANK
    g0 = m0 + MLA_Q_LORA + MLA_KV_LORA + MLA_ROPE + MLA_WIDTH
    seg_a = [w[:, 0:2 * GLA_QK + GLA_WIDTH], w[:, c0:m0], zc(LANES - 2 * GLA_GATE_RANK)]
    kr0 = m0 + MLA_Q_LORA + MLA_KV_LORA
    seg_b = [w[:, m0:kr0], zc(MLA_NOPE), w[:, kr0:kr0 + MLA_ROPE], zc(LANES - MLA_NOPE - MLA_ROPE)]
    seg_c = [w[:, g0:g0 + GQA_WIDTH + 2 * LANES]]
    seg_z = [w[:, 2 * GLA_QK + GLA_WIDTH:c0], w[:, kr0 + MLA_ROPE:g0], w[:, g0 + GQA_WIDTH + 2 * LANES:]]
    w_pad = jnp.concatenate(seg_a + seg_b + seg_c + seg_z, axis=1).astype(BF16)

    w_a = jnp.zeros((LANES, 2 * GLA_QK), F32)
    w_a = w_a.at[0:GLA_GATE_RANK, 0:GLA_QK].set(gla_w_a_fwd[l])
    w_a = w_a.at[GLA_GATE_RANK:2 * GLA_GATE_RANK, GLA_QK:].set(gla_w_a_bwd[l])
    b_a = jnp.concatenate([gla_b_a_fwd[l], gla_b_a_bwd[l]])[None, :]

    dq = MLA_NOPE + MLA_ROPE
    uq = mla_w_uq[l].reshape(MLA_Q_LORA, MLA_HEADS, dq)
    uq = jnp.pad(uq, ((0, 0), (0, 0), (0, LANES - dq))).reshape(MLA_Q_LORA, MLA_HEADS * LANES)
    ukv = mla_w_ukv[l].reshape(MLA_KV_LORA, MLA_HEADS, MLA_NOPE + MLA_V)
    uk = jnp.pad(ukv[:, :, :MLA_NOPE], ((0, 0), (0, 0), (0, LANES - MLA_NOPE)))
    uk = uk.reshape(MLA_KV_LORA, MLA_HEADS * LANES)
    uv = ukv[:, :, MLA_NOPE:].reshape(MLA_KV_LORA, MLA_WIDTH)

    pad_lane = lambda v: jnp.concatenate([v, jnp.zeros((LANES - v.shape[0],), F32)])[None, :]
    return {
        "norm_w": norm_w[l][None, :],
        "w_in": w_pad,
        "w_a": w_a.astype(BF16),
        "b_a": b_a,
        "mla_q_norm": mla_q_norm_w[l][None, :],
        "w_uq": uq.astype(BF16),
        "mla_kv_norm": mla_kv_norm_w[l][None, :],
        "w_uk": uk.astype(BF16),
        "w_uv": uv.astype(BF16),
        "gqa_q_norm": pad_lane(gqa_q_norm_w[l]),
        "gqa_k_norm": jnp.tile(gqa_k_norm_w[l], GQA_KV_HEADS)[None, :],
        "gla_norm": jnp.tile(gla_norm_w[l], GLA_HEADS)[None, :],
        "w_out": w_out[l].astype(BF16),
    }


def kernel(x, c, ctx, c_ctx, norm_w, w_mod, b_mod, w_in, gla_w_a_fwd, gla_b_a_fwd, gla_w_a_bwd,
           gla_b_a_bwd, gla_norm_w, mla_q_norm_w, mla_w_uq, mla_kv_norm_w, mla_w_ukv,
           gqa_q_norm_w, gqa_k_norm_w, w_out, final_norm_w):
    bsz, n, _ = x.shape
    depth = w_in.shape[0]
    assert ctx.shape[1] == CTX_LEN and n % (2 * TM) == 0 and n % GRID_W == 0
    rows = n // GRID_W
    rope_m = _rope_tables(rows, MLA_ROPE, MLA_NOPE)
    rope_g = _rope_tables(rows, HEAD_DIM, 0)

    mod_rows = -(-(bsz + 1) // 8) * 8
    cs = jnp.zeros((mod_rows, D_MODEL), F32).at[:bsz].set(c).at[bsz].set(c_ctx)
    mod_all = _modulation(cs, w_mod, b_mod)
    fnw = final_norm_w[None, :]

    xs = jnp.concatenate([ctx, x], axis=1)
    for l in range(depth):
        lw = _layer_weights(l, norm_w, w_in, gla_w_a_fwd, gla_b_a_fwd, gla_w_a_bwd, gla_b_a_bwd,
                            gla_norm_w, mla_q_norm_w, mla_w_uq, mla_kv_norm_w, mla_w_ukv,
                            gqa_q_norm_w, gqa_k_norm_w, w_out)
        m = mod_all[l]
        mod = jnp.stack([jnp.broadcast_to(m[bsz], (bsz, 3 * D_MODEL)), m[:bsz]], axis=1)
        mod = mod.reshape(2 * bsz, 1, 3 * D_MODEL)
        gq, gk, gv, gg, zg, mq, mk, mvt, cq, ck, cvt = _prep(xs, mod, lw, rope_m, rope_g)
        o_f, o_b = _gla(gq, gk, gv, gg)
        y_b = _attention(mq, mk, mvt, shared_kv=False)
        y_c = _attention(cq, ck, cvt, shared_kv=True)
        xs = _outproj(xs, o_f, o_b, y_b, y_c, zg, mod, lw, fnw, final=(l == depth - 1))
    return xs
```

```python
import functools

import numpy as np
import jax
import jax.numpy as jnp
from jax import lax
from jax.experimental import pallas as pl
from jax.experimental.pallas import tpu as pltpu

D_MODEL = 1024
GRID_W = 64
CTX_LEN = 256
HEAD_DIM = 64
ROPE_THETA = 10000.0
EPS = 1e-6

GLA_HEADS = 4
GLA_DK = 32
GLA_DV = 64
GLA_WIDTH = GLA_HEADS * GLA_DV
GLA_QK = GLA_HEADS * GLA_DK
GLA_GATE_RANK = 16
GLA_GATE_NORM = 16.0
GLA_CHUNK = 64

MLA_HEADS = 6
MLA_NOPE = 64
MLA_ROPE = 32
MLA_V = 64
MLA_Q_LORA = 256
MLA_KV_LORA = 256
MLA_WIDTH = MLA_HEADS * MLA_V

GQA_HEADS = 6
GQA_KV_HEADS = 2
GQA_GROUP = GQA_HEADS // GQA_KV_HEADS
GQA_WIDTH = GQA_HEADS * HEAD_DIM

D_MIX = GLA_WIDTH + MLA_WIDTH + GQA_WIDTH

LANES = 128
TM = CTX_LEN
CHUNKS = TM // GLA_CHUNK
SEG = 640
W_IN_PAD = 3 * SEG + D_MIX
VMEM_LIMIT = 48 * 1024 * 1024
ATT_CK = 256
ATT_SUB = 32
LOG2E = 1.4426950408889634

F32 = jnp.float32
BF16 = jnp.bfloat16


def _cparams(sem):
    return pltpu.CompilerParams(dimension_semantics=sem, vmem_limit_bytes=VMEM_LIMIT)


def _dot(a, b):
    return jnp.dot(a, b, preferred_element_type=F32)


def _dot_nt(a, b):
    return lax.dot_general(a, b, (((1,), (1,)), ((), ())), preferred_element_type=F32)


def _dot_tn(a, b):
    return lax.dot_general(a, b, (((0,), (0,)), ((), ())), preferred_element_type=F32)


def _silu(v):
    return v / (1.0 + jnp.exp(-v))


def _mod_kernel(c_ref, w_ref, b_ref, o_ref):
    s = _silu(c_ref[...]).astype(BF16)
    o_ref[...] = _dot(s, w_ref[...].astype(BF16)) + b_ref[...]


def _modulation(cs, w_mod, b_mod):
    depth = w_mod.shape[0]
    rows = cs.shape[0]
    return pl.pallas_call(
        _mod_kernel,
        grid=(depth, 3),
        in_specs=[
            pl.BlockSpec((rows, D_MODEL), lambda l, j: (0, 0)),
            pl.BlockSpec((None, D_MODEL, D_MODEL), lambda l, j: (l, 0, j)),
            pl.BlockSpec((None, 1, D_MODEL), lambda l, j: (l, 0, j)),
        ],
        out_specs=pl.BlockSpec((None, rows, D_MODEL), lambda l, j: (l, 0, j)),
        out_shape=jax.ShapeDtypeStruct((depth, rows, 3 * D_MODEL), F32),
        compiler_params=_cparams(("arbitrary", "arbitrary")),
        name="modulation",
    )(cs, w_mod, b_mod.reshape(depth, 1, 3 * D_MODEL))


def _rope(t, tab, shift):
    c, sa, sb = tab[:, :LANES], tab[:, LANES:2 * LANES], tab[:, 2 * LANES:]
    return t * c + pltpu.roll(t, LANES - shift, 1) * sa + pltpu.roll(t, shift, 1) * sb


def _prep_kernel(x_ref, mod_ref, nw_ref, win_ref, wa_ref, ba_ref, mqn_ref, wuq_ref, mkvn_ref,
                 wuk_ref, wuv_ref, gqn_ref, gkn_ref, rm_ref, rg_ref,
                 gq_ref, gk_ref, gv_ref, gg_ref, zg_ref, mq_ref, mk_ref, mvt_ref, cq_ref, ck_ref,
                 cvt_ref):
    x = x_ref[...]
    mod = mod_ref[...]
    shift, scale = mod[:, :D_MODEL], mod[:, D_MODEL:2 * D_MODEL]
    ms = jnp.mean(x * x, axis=-1, keepdims=True)
    h = (x * lax.rsqrt(ms + EPS) * nw_ref[...]) * (1.0 + scale) + shift
    hb = h.astype(BF16)

    uz = _dot(hb, win_ref[:, 3 * SEG:])
    zg_ref[...] = _silu(uz)

    ua = _dot(hb, win_ref[:, 0:SEG])
    gq_ref[...] = (ua[:, 0:GLA_QK] * GLA_DK ** -0.5).astype(BF16)
    gk_ref[...] = ua[:, GLA_QK:2 * GLA_QK].astype(BF16)
    gv_ref[...] = ua[:, 2 * GLA_QK:2 * GLA_QK + GLA_WIDTH].astype(BF16)
    a = ua[:, 2 * GLA_QK + GLA_WIDTH:SEG].astype(BF16)
    xg = _dot(a, wa_ref[...]) + ba_ref[...]
    log_sig = jnp.minimum(xg, 0.0) - jnp.log1p(jnp.exp(-jnp.abs(xg)))
    gg_ref[...] = log_sig * (1.0 / GLA_GATE_NORM)

    ub = _dot(hb, win_ref[:, SEG:2 * SEG])
    cq = ub[:, 0:MLA_Q_LORA]
    ckv = ub[:, MLA_Q_LORA:MLA_Q_LORA + MLA_KV_LORA]
    kr = ub[:, MLA_Q_LORA + MLA_KV_LORA:SEG]
    cqn = cq * lax.rsqrt(jnp.mean(cq * cq, axis=-1, keepdims=True) + EPS) * mqn_ref[...]
    ckvn = ckv * lax.rsqrt(jnp.mean(ckv * ckv, axis=-1, keepdims=True) + EPS) * mkvn_ref[...]
    qm = _dot(cqn.astype(BF16), wuq_ref[...])
    ckvb = ckvn.astype(BF16)
    kn = _dot(ckvb, wuk_ref[...])
    vm = _dot(ckvb, wuv_ref[...])
    rm = rm_ref[...]
    krr = _rope(kr, rm, MLA_ROPE // 4)
    mla_scale = (MLA_NOPE + MLA_ROPE) ** -0.5 * LOG2E
    for hd in range(MLA_HEADS):
        sl = slice(hd * LANES, (hd + 1) * LANES)
        mq_ref[:, sl] = (_rope(qm[:, sl], rm, MLA_ROPE // 4) * mla_scale).astype(BF16)
        mk_ref[:, sl] = (kn[:, sl] + krr).astype(BF16)
    for p in range(MLA_HEADS // 2):
        mvt_ref[p] = vm[:, p * LANES:(p + 1) * LANES].T.astype(BF16)

    uc = _dot(hb, win_ref[:, 2 * SEG:3 * SEG])
    rg = rg_ref[...]
    lo = lax.broadcasted_iota(jnp.int32, (TM, LANES), 1) < HEAD_DIM
    gqn = gqn_ref[...]
    for hd in range(GQA_HEADS):
        blk = uc[:, (hd // 2) * LANES:(hd // 2 + 1) * LANES]
        if hd % 2 == 1:
            blk = pltpu.roll(blk, HEAD_DIM, 1)
        xh = jnp.where(lo, blk, 0.0)
        msq = jnp.sum(xh * xh, axis=-1, keepdims=True) * (1.0 / HEAD_DIM)
        xh = xh * lax.rsqrt(msq + EPS) * gqn
        xh = _rope(xh, rg, HEAD_DIM // 4) * (HEAD_DIM ** -0.5 * LOG2E)
        if hd // GQA_GROUP == 1:
            xh = pltpu.roll(xh, HEAD_DIM, 1)
        cq_ref[:, hd * LANES:(hd + 1) * LANES] = xh.astype(BF16)
    k = uc[:, GQA_WIDTH:GQA_WIDTH + LANES]
    k2 = k * k
    ms_lo = jnp.sum(jnp.where(lo, k2, 0.0), axis=-1, keepdims=True) * (1.0 / HEAD_DIM)
    ms_hi = jnp.sum(jnp.where(lo, 0.0, k2), axis=-1, keepdims=True) * (1.0 / HEAD_DIM)
    inv = jnp.where(lo, lax.rsqrt(ms_lo + EPS), lax.rsqrt(ms_hi + EPS))
    kg = k * inv * gkn_ref[...]
    rg2 = jnp.concatenate([rg[:, j * LANES:(j + 1) * LANES]
                           + pltpu.roll(rg[:, j * LANES:(j + 1) * LANES], HEAD_DIM, 1)
                           for j in range(3)], axis=1)
    ck_ref[...] = _rope(kg, rg2, HEAD_DIM // 4).astype(BF16)
    cvt_ref[...] = uc[:, GQA_WIDTH + LANES:SEG].T.astype(BF16)


def _prep(xs, mod, lw, rope_m, rope_g):
    bsz, t, _ = xs.shape
    nt = t // TM
    tok = lambda w: pl.BlockSpec((None, TM, w), lambda b, i: (b, i, 0))
    full = lambda a: pl.BlockSpec(a.shape, lambda b, i: (0,) * a.ndim)
    tab = pl.BlockSpec((TM, 3 * LANES), lambda b, i: (i, 0))
    weights = [lw[n] for n in ("norm_w", "w_in", "w_a", "b_a", "mla_q_norm", "w_uq", "mla_kv_norm",
                               "w_uk", "w_uv", "gqa_q_norm", "gqa_k_norm")]
    out_shape = [
        jax.ShapeDtypeStruct((bsz, t, GLA_QK), BF16),
        jax.ShapeDtypeStruct((bsz, t, GLA_QK), BF16),
        jax.ShapeDtypeStruct((bsz, t, GLA_WIDTH), BF16),
        jax.ShapeDtypeStruct((bsz, t, 2 * GLA_QK), F32),
        jax.ShapeDtypeStruct((bsz, t, D_MIX), F32),
        jax.ShapeDtypeStruct((bsz, t, MLA_HEADS * LANES), BF16),
        jax.ShapeDtypeStruct((bsz, t, MLA_HEADS * LANES), BF16),
        jax.ShapeDtypeStruct((bsz, MLA_HEADS // 2, LANES, t), BF16),
        jax.ShapeDtypeStruct((bsz, t, GQA_HEADS * LANES), BF16),
        jax.ShapeDtypeStruct((bsz, t, LANES), BF16),
        jax.ShapeDtypeStruct((bsz, LANES, t), BF16),
    ]
    out_specs = [
        tok(GLA_QK), tok(GLA_QK), tok(GLA_WIDTH), tok(2 * GLA_QK), tok(D_MIX),
        tok(MLA_HEADS * LANES), tok(MLA_HEADS * LANES),
        pl.BlockSpec((None, MLA_HEADS // 2, LANES, TM), lambda b, i: (b, 0, 0, i)),
        tok(GQA_HEADS * LANES), tok(LANES),
        pl.BlockSpec((None, LANES, TM), lambda b, i: (b, 0, i)),
    ]
    return pl.pallas_call(
        _prep_kernel,
        grid=(bsz, nt),
        in_specs=[tok(D_MODEL),
                  pl.BlockSpec((None, 1, 3 * D_MODEL), lambda b, i: (2 * b + jnp.minimum(i, 1), 0, 0))]
                 + [full(w) for w in weights] + [tab, tab],
        out_specs=out_specs,
        out_shape=out_shape,
        compiler_params=_cparams(("parallel", "arbitrary")),
        name="prep",
    )(xs, mod, *weights, rope_m, rope_g)


def _gla_direction(q_ref, k_ref, v_ref, g_ref, l_ref, cmask_ref, qmask, omask, smask, s_ref, o_ref,
                   reverse):
    q = q_ref[...].astype(F32)
    k = k_ref[...].astype(F32)
    g = g_ref[...]
    g_hi = g.astype(BF16)
    r1 = g - g_hi.astype(F32)
    g_mid = r1.astype(BF16)
    g_lo = (r1 - g_mid.astype(F32)).astype(BF16)
    tri = l_ref[...]
    cum = _dot(tri, g_hi) + _dot(tri, g_mid) + _dot(tri, g_lo)
    cmask = cmask_ref[...]
    order = range(CHUNKS - 1, -1, -1) if reverse else range(CHUNKS)
    for c in order:
        sl = slice(c * GLA_CHUNK, (c + 1) * GLA_CHUNK)
        cum_c = cum[sl]
        last = cum_c[0:1] if reverse else cum_c[GLA_CHUNK - 1:GLA_CHUNK]
        qd = q[sl] * jnp.exp(cum_c)
        ki = k[sl] * jnp.exp(-cum_c)
        kd = k[sl] * jnp.exp(last - cum_c)
        vc = v_ref[sl, :]
        qh = jnp.concatenate([qd] * GLA_HEADS, axis=0) * qmask
        att = _dot_nt(qh.astype(BF16), ki.astype(BF16)) * cmask
        o_all = _dot(att.astype(BF16), vc) * omask
        o_intra = o_all[0:GLA_CHUNK]
        for hd in range(1, GLA_HEADS):
            o_intra = o_intra + o_all[hd * GLA_CHUNK:(hd + 1) * GLA_CHUNK]
        st = s_ref[...]
        o_inter = _dot_nt(qd.astype(BF16), st.astype(BF16))
        o_ref[sl, :] = o_inter + o_intra
        ds = _dot_tn(vc, kd.astype(BF16)) * smask
        s_ref[...] = st * jnp.exp(last) + ds


def _gla_kernel(qf_ref, kf_ref, vf_ref, gf_ref, qb_ref, kb_ref, vb_ref, gb_ref, lf_ref, lb_ref,
                cmf_ref, cmb_ref, qmask_ref, omask_ref, smask_ref, of_ref, ob_ref, sf_ref, sb_ref):
    @pl.when(pl.program_id(1) == 0)
    def _():
        sf_ref[...] = jnp.zeros_like(sf_ref)
        sb_ref[...] = jnp.zeros_like(sb_ref)

    qmask, omask, smask = qmask_ref[...], omask_ref[...], smask_ref[...]
    _gla_direction(qf_ref, kf_ref, vf_ref, gf_ref, lf_ref, cmf_ref, qmask, omask, smask, sf_ref,
                   of_ref, False)
    _gla_direction(qb_ref, kb_ref, vb_ref, gb_ref, lb_ref, cmb_ref, qmask, omask, smask, sb_ref,
                   ob_ref, True)


def _gla_consts():
    r = np.arange(TM)
    same = (r[:, None] // GLA_CHUNK) == (r[None, :] // GLA_CHUNK)
    lf = (same & (r[None, :] <= r[:, None])).astype(np.float32)
    lb = (same & (r[None, :] >= r[:, None])).astype(np.float32)
    rows = np.arange(GLA_HEADS * GLA_CHUNK)
    j = np.arange(GLA_CHUNK)
    cmf = ((rows[:, None] % GLA_CHUNK) >= j[None, :]).astype(np.float32)
    cmb = ((rows[:, None] % GLA_CHUNK) <= j[None, :]).astype(np.float32)
    qmask = ((rows[:, None] // GLA_CHUNK) == (np.arange(GLA_QK)[None, :] // GLA_DK)).astype(np.float32)
    omask = ((rows[:, None] // GLA_CHUNK) == (np.arange(GLA_WIDTH)[None, :] // GLA_DV)).astype(np.float32)
    smask = ((np.arange(GLA_WIDTH)[:, None] // GLA_DV) == (np.arange(GLA_QK)[None, :] // GLA_DK)).astype(np.float32)
    return (jnp.asarray(lf, BF16), jnp.asarray(lb, BF16), jnp.asarray(cmf), jnp.asarray(cmb),
            jnp.asarray(qmask), jnp.asarray(omask), jnp.asarray(smask))


def _gla(gq, gk, gv, gg):
    bsz, t, _ = gq.shape
    nt = t // TM
    consts = _gla_consts()
    fwd = lambda b, i: (b, i, 0)
    rev_blk = lambda i: jnp.where(i == 0, 0, nt - i)
    bwd = lambda b, i: (b, rev_blk(i), 0)
    bwd_g = lambda b, i: (b, rev_blk(i), 1)
    spec = lambda w, im: pl.BlockSpec((None, TM, w), im)
    full = lambda a: pl.BlockSpec(a.shape, lambda b, i: (0,) * a.ndim)
    return pl.pallas_call(
        _gla_kernel,
        grid=(bsz, nt),
        in_specs=[spec(GLA_QK, fwd), spec(GLA_QK, fwd), spec(GLA_WIDTH, fwd), spec(GLA_QK, fwd),
                  spec(GLA_QK, bwd), spec(GLA_QK, bwd), spec(GLA_WIDTH, bwd), spec(GLA_QK, bwd_g)]
                 + [full(a) for a in consts],
        out_specs=[spec(GLA_WIDTH, fwd), spec(GLA_WIDTH, bwd)],
        out_shape=[jax.ShapeDtypeStruct((bsz, t, GLA_WIDTH), F32)] * 2,
        scratch_shapes=[pltpu.VMEM((GLA_WIDTH, GLA_QK), F32)] * 2,
        compiler_params=_cparams(("parallel", "arbitrary")),
        name="gla",
    )(gq, gk, gv, gg, gq, gk, gv, gg, *consts)


def _head_values(vt_ref, h, cols, shared_kv):
    if shared_kv:
        r0 = (h // GQA_GROUP) * HEAD_DIM
        return vt_ref[r0:r0 + HEAD_DIM, cols]
    r0 = (h % 2) * MLA_V
    return vt_ref[h // 2, r0:r0 + MLA_V, cols]


def _write_heads(o_ref, outs_t):
    for p in range(len(outs_t) // 2):
        pair = jnp.concatenate([outs_t[2 * p], outs_t[2 * p + 1]], axis=0)
        o_ref[:, p * LANES:(p + 1) * LANES] = pair.T


def _attn_ctx_kernel(q_ref, k_ref, vt_ref, o_ref, *, shared_kv):
    heads = q_ref.shape[1] // LANES
    outs_t = []
    for h in range(heads):
        kh = k_ref[...] if shared_kv else k_ref[:, h * LANES:(h + 1) * LANES]
        s = _dot_nt(kh, q_ref[:, h * LANES:(h + 1) * LANES])
        p = jnp.exp2(s - jnp.max(s, axis=0, keepdims=True))
        l = jnp.sum(p, axis=0, keepdims=True)
        outs_t.append(_dot(_head_values(vt_ref, h, slice(None), shared_kv), p.astype(BF16)) / l)
    _write_heads(o_ref, outs_t)


def _attn_kernel(z_ref, qa_ref, qb_ref, k_ref, vt_ref, o_ref, s_ref, p_ref, *, shared_kv):
    t = k_ref.shape[0]
    heads = qa_ref.shape[1] // LANES
    chunks = [slice(a, a + ATT_CK) for a in range(0, t, ATT_CK)]
    z = z_ref[0]

    def dyn(start, size):
        return pl.ds(pl.multiple_of(z + start, ATT_SUB), size)

    def q_of(h):
        sl = slice(h * LANES, (h + 1) * LANES)
        qh = jnp.concatenate([qa_ref[:, sl], qb_ref[:, sl]], axis=0)
        return qh.astype(F32).T.astype(BF16)

    def fold8(v):
        return v.reshape(v.shape[0] // 8, 8, v.shape[1])

    def qk(h, qh, rows):
        kh = k_ref[rows, :] if shared_kv else k_ref[rows, h * LANES:(h + 1) * LANES]
        return _dot(kh, qh)

    def put_scores(s, rows, m8):
        s_ref[dyn(rows.start, ATT_CK), :] = s
        smax = jnp.max(fold8(s), axis=0)
        return smax if m8 is None else jnp.maximum(m8, smax)

    ones = jnp.ones((16, ATT_CK), BF16)
    m8 = None
    qh = q_of(0)
    for rows in chunks:
        m8 = put_scores(qk(0, qh, rows), rows, m8)
    outs_t = []
    for h in range(heads):
        mb = jnp.broadcast_to(jnp.max(m8, axis=0, keepdims=True), (ATT_SUB, m8.shape[1]))
        m8, acc = None, None
        more = h + 1 < heads
        if more:
            qh = q_of(h + 1)
            s_next = qk(h + 1, qh, chunks[0])
        for c, rows in enumerate(chunks):
            if more:
                s_cur = s_next
                if c + 1 < len(chunks):
                    s_next = qk(h + 1, qh, chunks[c + 1])
            vt1 = jnp.concatenate([_head_values(vt_ref, h, rows, shared_kv), ones], axis=0)
            p0 = (c % 2) * ATT_CK
            for r in range(0, ATT_CK, ATT_SUB):
                p = jnp.exp2(s_ref[dyn(rows.start + r, ATT_SUB), :] - mb)
                p_ref[p0 + r:p0 + r + ATT_SUB, :] = p.astype(BF16)
            if more:
                m8 = put_scores(s_cur, rows, m8)
            pv = _dot(vt1, p_ref[p0:p0 + ATT_CK, :])
            acc = pv if acc is None else acc + pv
        outs_t.append(acc[0:HEAD_DIM] / acc[HEAD_DIM:HEAD_DIM + 1])
    _write_heads(o_ref, outs_t)


def _attention(q, k, vt, shared_kv):
    bsz, t, width = q.shape
    n2 = (t - CTX_LEN) // (2 * TM)
    kw = k.shape[2]
    name = "attn_gqa" if shared_kv else "attn_mla"
    if shared_kv:
        v_spec = pl.BlockSpec((None, LANES, t), lambda b, j: (b, 0, 0))
        vc_spec = pl.BlockSpec((None, LANES, CTX_LEN), lambda b: (b, 0, 0))
    else:
        v_spec = pl.BlockSpec((None, vt.shape[1], LANES, t), lambda b, j: (b, 0, 0, 0))
        vc_spec = pl.BlockSpec((None, vt.shape[1], LANES, CTX_LEN), lambda b: (b, 0, 0, 0))
    y_lat = pl.pallas_call(
        functools.partial(_attn_kernel, shared_kv=shared_kv),
        grid=(bsz, n2),
        in_specs=[pl.BlockSpec(memory_space=pltpu.SMEM),
                  pl.BlockSpec((None, TM, width), lambda b, j: (b, 2 * j + 1, 0)),
                  pl.BlockSpec((None, TM, width), lambda b, j: (b, 2 * j + 2, 0)),
                  pl.BlockSpec((None, t, kw), lambda b, j: (b, 0, 0)), v_spec],
        out_specs=pl.BlockSpec((None, 2 * TM, width // 2), lambda b, j: (b, j, 0)),
        out_shape=jax.ShapeDtypeStruct((bsz, t - CTX_LEN, width // 2), F32),
        scratch_shapes=[pltpu.VMEM((t, 2 * TM), F32), pltpu.VMEM((2 * ATT_CK, 2 * TM), BF16)],
        compiler_params=_cparams(("parallel", "arbitrary")),
        name=name,
    )(jnp.zeros((1,), jnp.int32), q, q, k, vt)
    y_ctx = pl.pallas_call(
        functools.partial(_attn_ctx_kernel, shared_kv=shared_kv),
        grid=(bsz,),
        in_specs=[pl.BlockSpec((None, CTX_LEN, width), lambda b: (b, 0, 0)),
                  pl.BlockSpec((None, CTX_LEN, kw), lambda b: (b, 0, 0)), vc_spec],
        out_specs=pl.BlockSpec((None, CTX_LEN, width // 2), lambda b: (b, 0, 0)),
        out_shape=jax.ShapeDtypeStruct((bsz, CTX_LEN, width // 2), F32),
        compiler_params=_cparams(("parallel",)),
        name=name + "_ctx",
    )(q, k, vt)
    return y_lat, y_ctx


def _outproj_kernel(x_ref, of_ref, ob_ref, yb_ref, ybc_ref, yc_ref, ycc_ref, zg_ref, gnw_ref,
                    mod_ref, wout_ref, fnw_ref, o_ref, *, final):
    lo = lax.broadcasted_iota(jnp.int32, (TM, LANES), 1) < GLA_DV
    o = of_ref[...] + ob_ref[...]
    gnw = gnw_ref[...]
    zg = zg_ref[...]
    parts = []
    for j in range(GLA_WIDTH // LANES):
        sl = slice(j * LANES, (j + 1) * LANES)
        oj = o[:, sl]
        o2 = oj * oj
        ms_lo = jnp.sum(jnp.where(lo, o2, 0.0), axis=-1, keepdims=True) * (1.0 / GLA_DV)
        ms_hi = jnp.sum(jnp.where(lo, 0.0, o2), axis=-1, keepdims=True) * (1.0 / GLA_DV)
        inv = jnp.where(lo, lax.rsqrt(ms_lo + EPS), lax.rsqrt(ms_hi + EPS))
        parts.append(oj * inv * gnw[:, sl] * zg[:, sl])
    if final:
        yb, yc = yb_ref[...], yc_ref[...]
    else:
        is_ctx = pl.program_id(1) == 0
        yb = jnp.where(is_ctx, ybc_ref[...], yb_ref[...])
        yc = jnp.where(is_ctx, ycc_ref[...], yc_ref[...])
    parts.append(yb * zg[:, GLA_WIDTH:GLA_WIDTH + MLA_WIDTH])
    parts.append(yc * zg[:, GLA_WIDTH + MLA_WIDTH:])
    y = jnp.concatenate(parts, axis=-1).astype(BF16)
    gate = mod_ref[...]
    xn = x_ref[...] + gate * _dot(y, wout_ref[...])
    if final:
        xn = xn * lax.rsqrt(jnp.mean(xn * xn, axis=-1, keepdims=True) + EPS) * fnw_ref[...]
    o_ref[...] = xn


def _outproj(xs, o_f, o_b, y_b, y_c, zg, mod, lw, final_norm_w, final):
    bsz, t, _ = xs.shape
    off = 1 if final else 0
    nt = t // TM - off
    tok = lambda w: pl.BlockSpec((None, TM, w), lambda b, i: (b, i + off, 0))
    lat = lambda w: pl.BlockSpec((None, TM, w), lambda b, i: (b, jnp.maximum(i + off - 1, 0), 0))
    ctx = lambda w: pl.BlockSpec((None, CTX_LEN, w), lambda b, i: (b, 0, 0))
    full = lambda a: pl.BlockSpec(a.shape, lambda b, i: (0,) * a.ndim)
    return pl.pallas_call(
        functools.partial(_outproj_kernel, final=final),
        grid=(bsz, nt),
        in_specs=[tok(D_MODEL), tok(GLA_WIDTH), tok(GLA_WIDTH), lat(MLA_WIDTH), ctx(MLA_WIDTH),
                  lat(GQA_WIDTH), ctx(GQA_WIDTH), tok(D_MIX), full(lw["gla_norm"]),
                  pl.BlockSpec((None, 1, D_MODEL),
                               lambda b, i: (2 * b + jnp.minimum(i + off, 1), 0, 2)),
                  full(lw["w_out"]), full(final_norm_w)],
        out_specs=pl.BlockSpec((None, TM, D_MODEL), lambda b, i: (b, i, 0)),
        out_shape=jax.ShapeDtypeStruct((bsz, nt * TM, D_MODEL), F32),
        compiler_params=_cparams(("parallel", "arbitrary")),
        name="outproj",
    )(xs, o_f, o_b, *y_b, *y_c, zg, lw["gla_norm"], mod, lw["w_out"], final_norm_w)


def _rope_tables(rows, d_rot, lane0):
    quarter = d_rot // 4
    n = rows * GRID_W
    row = jnp.repeat(jnp.arange(rows), GRID_W).astype(F32)
    col = jnp.tile(jnp.arange(GRID_W), rows).astype(F32)
    freqs = ROPE_THETA ** (-jnp.arange(quarter, dtype=F32) / quarter)
    ang = jnp.stack([row[:, None] * freqs, col[:, None] * freqs], axis=1)
    cos, sin = jnp.cos(ang), jnp.sin(ang)
    zero = jnp.zeros_like(sin)
    cos_l = jnp.stack([cos, cos], axis=2).reshape(n, d_rot)
    sa_l = jnp.stack([-sin, zero], axis=2).reshape(n, d_rot)
    sb_l = jnp.stack([zero, sin], axis=2).reshape(n, d_rot)

    def place(v, fill):
        out = jnp.full((n, LANES), fill, F32)
        return out.at[:, lane0:lane0 + d_rot].set(v)

    c_fill = 1.0 if lane0 > 0 else 0.0
    lat = jnp.concatenate([place(cos_l, c_fill), place(sa_l, 0.0), place(sb_l, 0.0)], axis=1)
    ident = jnp.concatenate([place(jnp.ones((n, d_rot), F32), c_fill)[:CTX_LEN],
                             jnp.zeros((CTX_LEN, 2 * LANES), F32)], axis=1)
    return jnp.concatenate([ident, lat], axis=0)


def _layer_weights(l, norm_w, w_in, gla_w_a_fwd, gla_b_a_fwd, gla_w_a_bwd, gla_b_a_bwd, gla_norm_w,
                   mla_q_norm_w, mla_w_uq, mla_kv_norm_w, mla_w_ukv, gqa_q_norm_w, gqa_k_norm_w,
                   w_out):
    w = w_in[l]
    zc = lambda n: jnp.zeros((D_MODEL, n), F32)
    c0 = 2 * GLA_QK + 2 * GLA_WIDTH
    m0 = c0 + 2 * GLA_GATE_RANK
    g0 = m0 + MLA_Q_LORA + MLA_KV_LORA + MLA_ROPE + MLA_WIDTH
    seg_a = [w[:, 0:2 * GLA_QK + GLA_WIDTH], w[:, c0:m0], zc(LANES - 2 * GLA_GATE_RANK)]
    kr0 = m0 + MLA_Q_LORA + MLA_KV_LORA
    seg_b = [w[:, m0:kr0], zc(MLA_NOPE), w[:, kr0:kr0 + MLA_ROPE], zc(LANES - MLA_NOPE - MLA_ROPE)]
    seg_c = [w[:, g0:g0 + GQA_WIDTH + 2 * LANES]]
    seg_z = [w[:, 2 * GLA_QK + GLA_WIDTH:c0], w[:, kr0 + MLA_ROPE:g0], w[:, g0 + GQA_WIDTH + 2 * LANES:]]
    w_pad = jnp.concatenate(seg_a + seg_b + seg_c + seg_z, axis=1).astype(BF16)

    w_a = jnp.zeros((LANES, 2 * GLA_QK), F32)
    w_a = w_a.at[0:GLA_GATE_RANK, 0:GLA_QK].set(gla_w_a_fwd[l])
    w_a = w_a.at[GLA_GATE_RANK:2 * GLA_GATE_RANK, GLA_QK:].set(gla_w_a_bwd[l])
    b_a = jnp.concatenate([gla_b_a_fwd[l], gla_b_a_bwd[l]])[None, :]

    dq = MLA_NOPE + MLA_ROPE
    uq = mla_w_uq[l].reshape(MLA_Q_LORA, MLA_HEADS, dq)
    uq = jnp.pad(uq, ((0, 0), (0, 0), (0, LANES - dq))).reshape(MLA_Q_LORA, MLA_HEADS * LANES)
    ukv = mla_w_ukv[l].reshape(MLA_KV_LORA, MLA_HEADS, MLA_NOPE + MLA_V)
    uk = jnp.pad(ukv[:, :, :MLA_NOPE], ((0, 0), (0, 0), (0, LANES - MLA_NOPE)))
    uk = uk.reshape(MLA_KV_LORA, MLA_HEADS * LANES)
    uv = ukv[:, :, MLA_NOPE:].reshape(MLA_KV_LORA, MLA_WIDTH)

    pad_lane = lambda v: jnp.concatenate([v, jnp.zeros((LANES - v.shape[0],), F32)])[None, :]
    return {
        "norm_w": norm_w[l][None, :],
        "w_in": w_pad,
        "w_a": w_a.astype(BF16),
        "b_a": b_a,
        "mla_q_norm": mla_q_norm_w[l][None, :],
        "w_uq": uq.astype(BF16),
        "mla_kv_norm": mla_kv_norm_w[l][None, :],
        "w_uk": uk.astype(BF16),
        "w_uv": uv.astype(BF16),
        "gqa_q_norm": pad_lane(gqa_q_norm_w[l]),
        "gqa_k_norm": jnp.tile(gqa_k_norm_w[l], GQA_KV_HEADS)[None, :],
        "gla_norm": jnp.tile(gla_norm_w[l], GLA_HEADS)[None, :],
        "w_out": w_out[l].astype(BF16),
    }


def kernel(x, c, ctx, c_ctx, norm_w, w_mod, b_mod, w_in, gla_w_a_fwd, gla_b_a_fwd, gla_w_a_bwd,
           gla_b_a_bwd, gla_norm_w, mla_q_norm_w, mla_w_uq, mla_kv_norm_w, mla_w_ukv,
           gqa_q_norm_w, gqa_k_norm_w, w_out, final_norm_w):
    bsz, n, _ = x.shape
    depth = w_in.shape[0]
    assert ctx.shape[1] == CTX_LEN and n % (2 * TM) == 0 and n % GRID_W == 0
    rows = n // GRID_W
    rope_m = _rope_tables(rows, MLA_ROPE, MLA_NOPE)
    rope_g = _rope_tables(rows, HEAD_DIM, 0)

    mod_rows = -(-(bsz + 1) // 8) * 8
    cs = jnp.zeros((mod_rows, D_MODEL), F32).at[:bsz].set(c).at[bsz].set(c_ctx)
    mod_all = _modulation(cs, w_mod, b_mod)
    fnw = final_norm_w[None, :]

    xs = jnp.concatenate([ctx, x], axis=1)
    for l in range(depth):
        lw = _layer_weights(l, norm_w, w_in, gla_w_a_fwd, gla_b_a_fwd, gla_w_a_bwd, gla_b_a_bwd,
                            gla_norm_w, mla_q_norm_w, mla_w_uq, mla_kv_norm_w, mla_w_ukv,
                            gqa_q_norm_w, gqa_k_norm_w, w_out)
        m = mod_all[l]
        mod = jnp.stack([jnp.broadcast_to(m[bsz], (bsz, 3 * D_MODEL)), m[:bsz]], axis=1)
        mod = mod.reshape(2 * bsz, 1, 3 * D_MODEL)
        gq, gk, gv, gg, zg, mq, mk, mvt, cq, ck, cvt = _prep(xs, mod, lw, rope_m, rope_g)
        o_f, o_b = _gla(gq, gk, gv, gg)
        y_b = _attention(mq, mk, mvt, shared_kv=False)
        y_c = _attention(cq, ck, cvt, shared_kv=True)
        xs = _outproj(xs, o_f, o_b, y_b, y_c, zg, mod, lw, fnw, final=(l == depth - 1))
    return xs
```

```python
import functools

import numpy as np
import jax
import jax.numpy as jnp
from jax import lax
from jax.experimental import pallas as pl
from jax.experimental.pallas import tpu as pltpu

D_MODEL = 1024
GRID_W = 64
CTX_LEN = 256
HEAD_DIM = 64
ROPE_THETA = 10000.0
EPS = 1e-6

GLA_HEADS = 4
GLA_DK = 32
GLA_DV = 64
GLA_WIDTH = GLA_HEADS * GLA_DV
GLA_QK = GLA_HEADS * GLA_DK
GLA_GATE_RANK = 16
GLA_GATE_NORM = 16.0
GLA_CHUNK = 64

MLA_HEADS = 6
MLA_NOPE = 64
MLA_ROPE = 32
MLA_V = 64
MLA_Q_LORA = 256
MLA_KV_LORA = 256
MLA_WIDTH = MLA_HEADS * MLA_V

GQA_HEADS = 6
GQA_KV_HEADS = 2
GQA_GROUP = GQA_HEADS // GQA_KV_HEADS
GQA_WIDTH = GQA_HEADS * HEAD_DIM

D_MIX = GLA_WIDTH + MLA_WIDTH + GQA_WIDTH

LANES = 128
TM = CTX_LEN
CHUNKS = TM // GLA_CHUNK
SEG_A = 640
SEG_B = 768
SEG_C = 1920
W_IN_PAD = SEG_A + SEG_B + SEG_C + D_MIX
VMEM_LIMIT = 48 * 1024 * 1024
ATT_CK = 256
ATT_SUB = 32
LOG2E = 1.4426950408889634

F32 = jnp.float32
BF16 = jnp.bfloat16


def _cparams(sem):
    return pltpu.CompilerParams(dimension_semantics=sem, vmem_limit_bytes=VMEM_LIMIT)


def _dot(a, b):
    return jnp.dot(a, b, preferred_element_type=F32)


def _dot_nt(a, b):
    return lax.dot_general(a, b, (((1,), (1,)), ((), ())), preferred_element_type=F32)


def _dot_tn(a, b):
    return lax.dot_general(a, b, (((0,), (0,)), ((), ())), preferred_element_type=F32)


def _silu(v):
    return v / (1.0 + jnp.exp(-v))


def _mod_kernel(c_ref, w_ref, b_ref, o_ref):
    s = _silu(c_ref[...]).astype(BF16)
    o_ref[...] = _dot(s, w_ref[...].astype(BF16)) + b_ref[...]


def _modulation(cs, w_mod, b_mod):
    depth = w_mod.shape[0]
    rows = cs.shape[0]
    return pl.pallas_call(
        _mod_kernel,
        grid=(depth, 3),
        in_specs=[
            pl.BlockSpec((rows, D_MODEL), lambda l, j: (0, 0)),
            pl.BlockSpec((None, D_MODEL, D_MODEL), lambda l, j: (l, 0, j)),
            pl.BlockSpec((None, 1, D_MODEL), lambda l, j: (l, 0, j)),
        ],
        out_specs=pl.BlockSpec((None, rows, D_MODEL), lambda l, j: (l, 0, j)),
        out_shape=jax.ShapeDtypeStruct((depth, rows, 3 * D_MODEL), F32),
        compiler_params=_cparams(("arbitrary", "arbitrary")),
        name="modulation",
    )(cs, w_mod, b_mod.reshape(depth, 1, 3 * D_MODEL))


def _prep_kernel(x_ref, mod_ref, nw_ref, win_ref, wa_ref, ba_ref, mqn_ref, wuq_ref, wuqr_ref,
                 mkvn_ref, wuk_ref, wuv_ref, gqn_ref, gqnr_ref, gkn_ref, gknr_ref, rm_ref, rg_ref,
                 gq_ref, gk_ref, gv_ref, gg_ref, zg_ref, mq_ref, mk_ref, mvt_ref, cq_ref, ck_ref,
                 cvt_ref):
    x = x_ref[...]
    mod = mod_ref[...]
    shift, scale = mod[:, :D_MODEL], mod[:, D_MODEL:2 * D_MODEL]
    ms = jnp.mean(x * x, axis=-1, keepdims=True)
    h = (x * lax.rsqrt(ms + EPS) * nw_ref[...]) * (1.0 + scale) + shift
    hb = h.astype(BF16)

    uz = _dot(hb, win_ref[:, SEG_A + SEG_B + SEG_C:])
    zg_ref[...] = _silu(uz)

    ua = _dot(hb, win_ref[:, 0:SEG_A])
    gq_ref[...] = (ua[:, 0:GLA_QK] * GLA_DK ** -0.5).astype(BF16)
    gk_ref[...] = ua[:, GLA_QK:2 * GLA_QK].astype(BF16)
    gv_ref[...] = ua[:, 2 * GLA_QK:2 * GLA_QK + GLA_WIDTH].astype(BF16)
    a = ua[:, 2 * GLA_QK + GLA_WIDTH:SEG_A].astype(BF16)
    xg = _dot(a, wa_ref[...]) + ba_ref[...]
    log_sig = jnp.minimum(xg, 0.0) - jnp.log1p(jnp.exp(-jnp.abs(xg)))
    gg_ref[...] = log_sig * (1.0 / GLA_GATE_NORM)

    ub = _dot(hb, win_ref[:, SEG_A:SEG_A + SEG_B])
    cq = ub[:, 0:MLA_Q_LORA]
    ckv = ub[:, MLA_Q_LORA:MLA_Q_LORA + MLA_KV_LORA]
    kr = ub[:, MLA_Q_LORA + MLA_KV_LORA:MLA_Q_LORA + MLA_KV_LORA + LANES]
    kr_rot = ub[:, MLA_Q_LORA + MLA_KV_LORA + LANES:SEG_B]
    cqn = cq * lax.rsqrt(jnp.mean(cq * cq, axis=-1, keepdims=True) + EPS) * mqn_ref[...]
    ckvn = ckv * lax.rsqrt(jnp.mean(ckv * ckv, axis=-1, keepdims=True) + EPS) * mkvn_ref[...]
    cqb = cqn.astype(BF16)
    qm = _dot(cqb, wuq_ref[...])
    qm_rot = _dot(cqb, wuqr_ref[...])
    ckvb = ckvn.astype(BF16)
    kn = _dot(ckvb, wuk_ref[...])
    vm = _dot(ckvb, wuv_ref[...])
    cos_m, sin_m = rm_ref[:, :LANES], rm_ref[:, LANES:]
    k_rope = kr * cos_m + kr_rot * sin_m
    mla_scale = (MLA_NOPE + MLA_ROPE) ** -0.5 * LOG2E
    for hd in range(MLA_HEADS):
        sl = slice(hd * LANES, (hd + 1) * LANES)
        mq_ref[:, sl] = ((qm[:, sl] * cos_m + qm_rot[:, sl] * sin_m) * mla_scale).astype(BF16)
        mk_ref[:, sl] = (kn[:, sl] + k_rope).astype(BF16)
    for p in range(MLA_HEADS // 2):
        mvt_ref[p] = vm[:, p * LANES:(p + 1) * LANES].T.astype(BF16)

    uc = _dot(hb, win_ref[:, SEG_A + SEG_B:SEG_A + SEG_B + SEG_C])
    cos_g, sin_g = rg_ref[:, :LANES], rg_ref[:, LANES:]
    qa, qb = cos_g * gqn_ref[...], sin_g * gqnr_ref[...]
    gqa_scale = HEAD_DIM ** -0.5 * LOG2E
    nq = GQA_HEADS * LANES
    for hd in range(GQA_HEADS):
        sl = slice(hd * LANES, (hd + 1) * LANES)
        xh = uc[:, sl]
        xr = uc[:, nq + hd * LANES:nq + (hd + 1) * LANES]
        msq = jnp.sum(xh * xh, axis=-1, keepdims=True) * (1.0 / HEAD_DIM)
        cq_ref[:, sl] = ((xh * qa + xr * qb) * (lax.rsqrt(msq + EPS) * gqa_scale)).astype(BF16)
    k = uc[:, 2 * nq:2 * nq + LANES]
    k_rot = uc[:, 2 * nq + LANES:2 * nq + 2 * LANES]
    lo = lax.broadcasted_iota(jnp.int32, (TM, LANES), 1) < HEAD_DIM
    k2 = k * k
    ms_lo = jnp.sum(jnp.where(lo, k2, 0.0), axis=-1, keepdims=True) * (1.0 / HEAD_DIM)
    ms_hi = jnp.sum(jnp.where(lo, 0.0, k2), axis=-1, keepdims=True) * (1.0 / HEAD_DIM)
    inv = jnp.where(lo, lax.rsqrt(ms_lo + EPS), lax.rsqrt(ms_hi + EPS))
    ck_ref[...] = ((k * (cos_g * gkn_ref[...]) + k_rot * (sin_g * gknr_ref[...])) * inv).astype(BF16)
    cvt_ref[...] = uc[:, 2 * nq + 2 * LANES:SEG_C].T.astype(BF16)


def _prep(xs, mod, lw, rope_m, rope_g):
    bsz, t, _ = xs.shape
    nt = t // TM
    tok = lambda w: pl.BlockSpec((None, TM, w), lambda b, i: (b, i, 0))
    full = lambda a: pl.BlockSpec(a.shape, lambda b, i: (0,) * a.ndim)
    tab = pl.BlockSpec((TM, 2 * LANES), lambda b, i: (i, 0))
    weights = [lw[n] for n in ("norm_w", "w_in", "w_a", "b_a", "mla_q_norm", "w_uq", "w_uq_rot",
                               "mla_kv_norm", "w_uk", "w_uv", "gqa_q_norm", "gqa_q_norm_rot",
                               "gqa_k_norm", "gqa_k_norm_rot")]
    out_shape = [
        jax.ShapeDtypeStruct((bsz, t, GLA_QK), BF16),
        jax.ShapeDtypeStruct((bsz, t, GLA_QK), BF16),
        jax.ShapeDtypeStruct((bsz, t, GLA_WIDTH), BF16),
        jax.ShapeDtypeStruct((bsz, t, 2 * GLA_QK), F32),
        jax.ShapeDtypeStruct((bsz, t, D_MIX), F32),
        jax.ShapeDtypeStruct((bsz, t, MLA_HEADS * LANES), BF16),
        jax.ShapeDtypeStruct((bsz, t, MLA_HEADS * LANES), BF16),
        jax.ShapeDtypeStruct((bsz, MLA_HEADS // 2, LANES, t), BF16),
        jax.ShapeDtypeStruct((bsz, t, GQA_HEADS * LANES), BF16),
        jax.ShapeDtypeStruct((bsz, t, LANES), BF16),
        jax.ShapeDtypeStruct((bsz, LANES, t), BF16),
    ]
    out_specs = [
        tok(GLA_QK), tok(GLA_QK), tok(GLA_WIDTH), tok(2 * GLA_QK), tok(D_MIX),
        tok(MLA_HEADS * LANES), tok(MLA_HEADS * LANES),
        pl.BlockSpec((None, MLA_HEADS // 2, LANES, TM), lambda b, i: (b, 0, 0, i)),
        tok(GQA_HEADS * LANES), tok(LANES),
        pl.BlockSpec((None, LANES, TM), lambda b, i: (b, 0, i)),
    ]
    return pl.pallas_call(
        _prep_kernel,
        grid=(bsz, nt),
        in_specs=[tok(D_MODEL),
                  pl.BlockSpec((None, 1, 3 * D_MODEL), lambda b, i: (2 * b + jnp.minimum(i, 1), 0, 0))]
                 + [full(w) for w in weights] + [tab, tab],
        out_specs=out_specs,
        out_shape=out_shape,
        compiler_params=_cparams(("parallel", "arbitrary")),
        name="prep",
    )(xs, mod, *weights, rope_m, rope_g)


def _gla_direction(q_ref, k_ref, v_ref, g_ref, l_ref, cmask_ref, qmask, omask, smask, s_ref, o_ref,
                   reverse):
    q = q_ref[...].astype(F32)
    k = k_ref[...].astype(F32)
    g = g_ref[...]
    g_hi = g.astype(BF16)
    r1 = g - g_hi.astype(F32)
    g_mid = r1.astype(BF16)
    g_lo = (r1 - g_mid.astype(F32)).astype(BF16)
    tri = l_ref[...]
    cum = _dot(tri, g_hi) + _dot(tri, g_mid) + _dot(tri, g_lo)
    cmask = cmask_ref[...]
    order = range(CHUNKS - 1, -1, -1) if reverse else range(CHUNKS)
    for c in order:
        sl = slice(c * GLA_CHUNK, (c + 1) * GLA_CHUNK)
        cum_c = cum[sl]
        last = cum_c[0:1] if reverse else cum_c[GLA_CHUNK - 1:GLA_CHUNK]
        qd = q[sl] * jnp.exp(cum_c)
        ki = k[sl] * jnp.exp(-cum_c)
        kd = k[sl] * jnp.exp(last - cum_c)
        vc = v_ref[sl, :]
        qh = jnp.concatenate([qd] * GLA_HEADS, axis=0) * qmask
        att = _dot_nt(qh.astype(BF16), ki.astype(BF16)) * cmask
        o_all = _dot(att.astype(BF16), vc) * omask
        o_intra = o_all[0:GLA_CHUNK]
        for hd in range(1, GLA_HEADS):
            o_intra = o_intra + o_all[hd * GLA_CHUNK:(hd + 1) * GLA_CHUNK]
        st = s_ref[...]
        o_inter = _dot_nt(qd.astype(BF16), st.astype(BF16))
        o_ref[sl, :] = o_inter + o_intra
        ds = _dot_tn(vc, kd.astype(BF16)) * smask
        s_ref[...] = st * jnp.exp(last) + ds


def _gla_kernel(qf_ref, kf_ref, vf_ref, gf_ref, qb_ref, kb_ref, vb_ref, gb_ref, lf_ref, lb_ref,
                cmf_ref, cmb_ref, qmask_ref, omask_ref, smask_ref, of_ref, ob_ref, sf_ref, sb_ref):
    @pl.when(pl.program_id(1) == 0)
    def _():
        sf_ref[...] = jnp.zeros_like(sf_ref)
        sb_ref[...] = jnp.zeros_like(sb_ref)

    qmask, omask, smask = qmask_ref[...], omask_ref[...], smask_ref[...]
    _gla_direction(qf_ref, kf_ref, vf_ref, gf_ref, lf_ref, cmf_ref, qmask, omask, smask, sf_ref,
                   of_ref, False)
    _gla_direction(qb_ref, kb_ref, vb_ref, gb_ref, lb_ref, cmb_ref, qmask, omask, smask, sb_ref,
                   ob_ref, True)


def _gla_consts():
    r = np.arange(TM)
    same = (r[:, None] // GLA_CHUNK) == (r[None, :] // GLA_CHUNK)
    lf = (same & (r[None, :] <= r[:, None])).astype(np.float32)
    lb = (same & (r[None, :] >= r[:, None])).astype(np.float32)
    rows = np.arange(GLA_HEADS * GLA_CHUNK)
    j = np.arange(GLA_CHUNK)
    cmf = ((rows[:, None] % GLA_CHUNK) >= j[None, :]).astype(np.float32)
    cmb = ((rows[:, None] % GLA_CHUNK) <= j[None, :]).astype(np.float32)
    qmask = ((rows[:, None] // GLA_CHUNK) == (np.arange(GLA_QK)[None, :] // GLA_DK)).astype(np.float32)
    omask = ((rows[:, None] // GLA_CHUNK) == (np.arange(GLA_WIDTH)[None, :] // GLA_DV)).astype(np.float32)
    smask = ((np.arange(GLA_WIDTH)[:, None] // GLA_DV) == (np.arange(GLA_QK)[None, :] // GLA_DK)).astype(np.float32)
    return (jnp.asarray(lf, BF16), jnp.asarray(lb, BF16), jnp.asarray(cmf), jnp.asarray(cmb),
            jnp.asarray(qmask), jnp.asarray(omask), jnp.asarray(smask))


def _gla(gq, gk, gv, gg):
    bsz, t, _ = gq.shape
    nt = t // TM
    consts = _gla_consts()
    fwd = lambda b, i: (b, i, 0)
    rev_blk = lambda i: jnp.where(i == 0, 0, nt - i)
    bwd = lambda b, i: (b, rev_blk(i), 0)
    bwd_g = lambda b, i: (b, rev_blk(i), 1)
    spec = lambda w, im: pl.BlockSpec((None, TM, w), im)
    full = lambda a: pl.BlockSpec(a.shape, lambda b, i: (0,) * a.ndim)
    return pl.pallas_call(
        _gla_kernel,
        grid=(bsz, nt),
        in_specs=[spec(GLA_QK, fwd), spec(GLA_QK, fwd), spec(GLA_WIDTH, fwd), spec(GLA_QK, fwd),
                  spec(GLA_QK, bwd), spec(GLA_QK, bwd), spec(GLA_WIDTH, bwd), spec(GLA_QK, bwd_g)]
                 + [full(a) for a in consts],
        out_specs=[spec(GLA_WIDTH, fwd), spec(GLA_WIDTH, bwd)],
        out_shape=[jax.ShapeDtypeStruct((bsz, t, GLA_WIDTH), F32)] * 2,
        scratch_shapes=[pltpu.VMEM((GLA_WIDTH, GLA_QK), F32)] * 2,
        compiler_params=_cparams(("parallel", "arbitrary")),
        name="gla",
    )(gq, gk, gv, gg, gq, gk, gv, gg, *consts)


def _head_values(vt_ref, h, cols, shared_kv):
    if shared_kv:
        r0 = (h // GQA_GROUP) * HEAD_DIM
        return vt_ref[r0:r0 + HEAD_DIM, cols]
    r0 = (h % 2) * MLA_V
    return vt_ref[h // 2, r0:r0 + MLA_V, cols]


def _write_heads(o_ref, outs_t):
    for p in range(len(outs_t) // 2):
        pair = jnp.concatenate([outs_t[2 * p], outs_t[2 * p + 1]], axis=0)
        o_ref[:, p * LANES:(p + 1) * LANES] = pair.T


def _attn_ctx_kernel(q_ref, k_ref, vt_ref, o_ref, *, shared_kv):
    heads = q_ref.shape[1] // LANES
    outs_t = []
    for h in range(heads):
        kh = k_ref[...] if shared_kv else k_ref[:, h * LANES:(h + 1) * LANES]
        s = _dot_nt(kh, q_ref[:, h * LANES:(h + 1) * LANES])
        p = jnp.exp2(s - jnp.max(s, axis=0, keepdims=True))
        l = jnp.sum(p, axis=0, keepdims=True)
        outs_t.append(_dot(_head_values(vt_ref, h, slice(None), shared_kv), p.astype(BF16)) / l)
    _write_heads(o_ref, outs_t)


def _attn_kernel(z_ref, qa_ref, qb_ref, k_ref, vt_ref, o_ref, s_ref, p_ref, *, shared_kv):
    t = k_ref.shape[0]
    heads = qa_ref.shape[1] // LANES
    chunks = [slice(a, a + ATT_CK) for a in range(0, t, ATT_CK)]
    z = z_ref[0]

    def dyn(start, size):
        return pl.ds(pl.multiple_of(z + start, ATT_SUB), size)

    def q_of(h):
        sl = slice(h * LANES, (h + 1) * LANES)
        qh = jnp.concatenate([qa_ref[:, sl], qb_ref[:, sl]], axis=0)
        return qh.astype(F32).T.astype(BF16)

    def fold8(v):
        return v.reshape(v.shape[0] // 8, 8, v.shape[1])

    def qk(h, qh, rows):
        kh = k_ref[rows, :] if shared_kv else k_ref[rows, h * LANES:(h + 1) * LANES]
        return _dot(kh, qh)

    def put_scores(s, rows, m8):
        s_ref[dyn(rows.start, ATT_CK), :] = s
        smax = jnp.max(fold8(s), axis=0)
        return smax if m8 is None else jnp.maximum(m8, smax)

    ones = jnp.ones((16, ATT_CK), BF16)
    m8 = None
    qh = q_of(0)
    for rows in chunks:
        m8 = put_scores(qk(0, qh, rows), rows, m8)
    outs_t = []
    for h in range(heads):
        mb = jnp.broadcast_to(jnp.max(m8, axis=0, keepdims=True), (ATT_SUB, m8.shape[1]))
        m8, acc = None, None
        more = h + 1 < heads
        if more:
            qh = q_of(h + 1)
            s_next = qk(h + 1, qh, chunks[0])
        for c, rows in enumerate(chunks):
            if more:
                s_cur = s_next
                if c + 1 < len(chunks):
                    s_next = qk(h + 1, qh, chunks[c + 1])
            vt1 = jnp.concatenate([_head_values(vt_ref, h, rows, shared_kv), ones], axis=0)
            p0 = (c % 2) * ATT_CK
            for r in range(0, ATT_CK, ATT_SUB):
                p = jnp.exp2(s_ref[dyn(rows.start + r, ATT_SUB), :] - mb)
                p_ref[p0 + r:p0 + r + ATT_SUB, :] = p.astype(BF16)
            if more:
                m8 = put_scores(s_cur, rows, m8)
            pv = _dot(vt1, p_ref[p0:p0 + ATT_CK, :])
            acc = pv if acc is None else acc + pv
        outs_t.append(acc[0:HEAD_DIM] / acc[HEAD_DIM:HEAD_DIM + 1])
    _write_heads(o_ref, outs_t)


def _attention(q, k, vt, shared_kv):
    bsz, t, width = q.shape
    n2 = (t - CTX_LEN) // (2 * TM)
    kw = k.shape[2]
    name = "attn_gqa" if shared_kv else "attn_mla"
    if shared_kv:
        v_spec = pl.BlockSpec((None, LANES, t), lambda b, j: (b, 0, 0))
        vc_spec = pl.BlockSpec((None, LANES, CTX_LEN), lambda b: (b, 0, 0))
    else:
        v_spec = pl.BlockSpec((None, vt.shape[1], LANES, t), lambda b, j: (b, 0, 0, 0))
        vc_spec = pl.BlockSpec((None, vt.shape[1], LANES, CTX_LEN), lambda b: (b, 0, 0, 0))
    y_lat = pl.pallas_call(
        functools.partial(_attn_kernel, shared_kv=shared_kv),
        grid=(bsz, n2),
        in_specs=[pl.BlockSpec(memory_space=pltpu.SMEM),
                  pl.BlockSpec((None, TM, width), lambda b, j: (b, 2 * j + 1, 0)),
                  pl.BlockSpec((None, TM, width), lambda b, j: (b, 2 * j + 2, 0)),
                  pl.BlockSpec((None, t, kw), lambda b, j: (b, 0, 0)), v_spec],
        out_specs=pl.BlockSpec((None, 2 * TM, width // 2), lambda b, j: (b, j, 0)),
        out_shape=jax.ShapeDtypeStruct((bsz, t - CTX_LEN, width // 2), F32),
        scratch_shapes=[pltpu.VMEM((t, 2 * TM), F32), pltpu.VMEM((2 * ATT_CK, 2 * TM), BF16)],
        compiler_params=_cparams(("parallel", "arbitrary")),
        name=name,
    )(jnp.zeros((1,), jnp.int32), q, q, k, vt)
    y_ctx = pl.pallas_call(
        functools.partial(_attn_ctx_kernel, shared_kv=shared_kv),
        grid=(bsz,),
        in_specs=[pl.BlockSpec((None, CTX_LEN, width), lambda b: (b, 0, 0)),
                  pl.BlockSpec((None, CTX_LEN, kw), lambda b: (b, 0, 0)), vc_spec],
        out_specs=pl.BlockSpec((None, CTX_LEN, width // 2), lambda b: (b, 0, 0)),
        out_shape=jax.ShapeDtypeStruct((bsz, CTX_LEN, width // 2), F32),
        compiler_params=_cparams(("parallel",)),
        name=name + "_ctx",
    )(q, k, vt)
    return y_lat, y_ctx


def _outproj_kernel(x_ref, of_ref, ob_ref, yb_ref, ybc_ref, yc_ref, ycc_ref, zg_ref, gnw_ref,
                    mod_ref, wout_ref, fnw_ref, o_ref, *, final):
    lo = lax.broadcasted_iota(jnp.int32, (TM, LANES), 1) < GLA_DV
    o = of_ref[...] + ob_ref[...]
    gnw = gnw_ref[...]
    zg = zg_ref[...]
    parts = []
    for j in range(GLA_WIDTH // LANES):
        sl = slice(j * LANES, (j + 1) * LANES)
        oj = o[:, sl]
        o2 = oj * oj
        ms_lo = jnp.sum(jnp.where(lo, o2, 0.0), axis=-1, keepdims=True) * (1.0 / GLA_DV)
        ms_hi = jnp.sum(jnp.where(lo, 0.0, o2), axis=-1, keepdims=True) * (1.0 / GLA_DV)
        inv = jnp.where(lo, lax.rsqrt(ms_lo + EPS), lax.rsqrt(ms_hi + EPS))
        parts.append(oj * inv * gnw[:, sl] * zg[:, sl])
    if final:
        yb, yc = yb_ref[...], yc_ref[...]
    else:
        is_ctx = pl.program_id(1) == 0
        yb = jnp.where(is_ctx, ybc_ref[...], yb_ref[...])
        yc = jnp.where(is_ctx, ycc_ref[...], yc_ref[...])
    parts.append(yb * zg[:, GLA_WIDTH:GLA_WIDTH + MLA_WIDTH])
    parts.append(yc * zg[:, GLA_WIDTH + MLA_WIDTH:])
    y = jnp.concatenate(parts, axis=-1).astype(BF16)
    gate = mod_ref[...]
    xn = x_ref[...] + gate * _dot(y, wout_ref[...])
    if final:
        xn = xn * lax.rsqrt(jnp.mean(xn * xn, axis=-1, keepdims=True) + EPS) * fnw_ref[...]
    o_ref[...] = xn


def _outproj(xs, o_f, o_b, y_b, y_c, zg, mod, lw, final_norm_w, final):
    bsz, t, _ = xs.shape
    off = 1 if final else 0
    nt = t // TM - off
    tok = lambda w: pl.BlockSpec((None, TM, w), lambda b, i: (b, i + off, 0))
    lat = lambda w: pl.BlockSpec((None, TM, w), lambda b, i: (b, jnp.maximum(i + off - 1, 0), 0))
    ctx = lambda w: pl.BlockSpec((None, CTX_LEN, w), lambda b, i: (b, 0, 0))
    full = lambda a: pl.BlockSpec(a.shape, lambda b, i: (0,) * a.ndim)
    return pl.pallas_call(
        functools.partial(_outproj_kernel, final=final),
        grid=(bsz, nt),
        in_specs=[tok(D_MODEL), tok(GLA_WIDTH), tok(GLA_WIDTH), lat(MLA_WIDTH), ctx(MLA_WIDTH),
                  lat(GQA_WIDTH), ctx(GQA_WIDTH), tok(D_MIX), full(lw["gla_norm"]),
                  pl.BlockSpec((None, 1, D_MODEL),
                               lambda b, i: (2 * b + jnp.minimum(i + off, 1), 0, 2)),
                  full(lw["w_out"]), full(final_norm_w)],
        out_specs=pl.BlockSpec((None, TM, D_MODEL), lambda b, i: (b, i, 0)),
        out_shape=jax.ShapeDtypeStruct((bsz, nt * TM, D_MODEL), F32),
        compiler_params=_cparams(("parallel", "arbitrary")),
        name="outproj",
    )(xs, o_f, o_b, *y_b, *y_c, zg, lw["gla_norm"], mod, lw["w_out"], final_norm_w)


def _rot_src(d_rot):
    quarter = d_rot // 4
    i = np.arange(d_rot)
    return np.where(i % (2 * quarter) < quarter, i + quarter, i - quarter)


def _rope_tables(rows, d_rot, lanes):
    quarter = d_rot // 4
    n = rows * GRID_W
    row = jnp.repeat(jnp.arange(rows), GRID_W).astype(F32)
    col = jnp.tile(jnp.arange(GRID_W), rows).astype(F32)
    freqs = ROPE_THETA ** (-jnp.arange(quarter, dtype=F32) / quarter)
    ang = jnp.stack([row[:, None] * freqs, col[:, None] * freqs], axis=1)
    cos, sin = jnp.cos(ang), jnp.sin(ang)
    cos_l = jnp.stack([cos, cos], axis=2).reshape(n, d_rot)
    sin_l = jnp.stack([-sin, sin], axis=2).reshape(n, d_rot)
    cos_t = jnp.ones((n, LANES), F32)
    sin_t = jnp.zeros((n, LANES), F32)
    for lane0 in lanes:
        cos_t = cos_t.at[:, lane0:lane0 + d_rot].set(cos_l)
        sin_t = sin_t.at[:, lane0:lane0 + d_rot].set(sin_l)
    ident = jnp.concatenate([jnp.ones((CTX_LEN, LANES), F32), jnp.zeros((CTX_LEN, LANES), F32)], 1)
    return jnp.concatenate([ident, jnp.concatenate([cos_t, sin_t], axis=1)], axis=0)


def _take_cols(w, idx):
    w_ext = jnp.concatenate([w, jnp.zeros((w.shape[0], 1), w.dtype)], axis=1)
    return w_ext[:, np.asarray(idx)]


def _w_in_columns():
    c0 = 2 * GLA_QK + 2 * GLA_WIDTH
    m0 = c0 + 2 * GLA_GATE_RANK
    kr0 = m0 + MLA_Q_LORA + MLA_KV_LORA
    g0 = kr0 + MLA_ROPE + MLA_WIDTH
    gk0 = g0 + GQA_WIDTH
    gv0 = gk0 + GQA_KV_HEADS * HEAD_DIM
    gz0 = gv0 + GQA_KV_HEADS * HEAD_DIM
    pad = lambda n: [-1] * n
    seg_a = list(range(0, 2 * GLA_QK + GLA_WIDTH)) + list(range(c0, m0)) + pad(LANES - 2 * GLA_GATE_RANK)
    rs_m, rs_g = _rot_src(MLA_ROPE), _rot_src(HEAD_DIM)
    tail = pad(LANES - MLA_NOPE - MLA_ROPE)
    seg_b = (list(range(m0, kr0)) + pad(MLA_NOPE) + list(range(kr0, kr0 + MLA_ROPE)) + tail
             + pad(MLA_NOPE) + [kr0 + int(j) for j in rs_m] + tail)
    q_pad, q_rot = [], []
    for h in range(GQA_HEADS):
        base = g0 + h * HEAD_DIM
        own = list(range(base, base + HEAD_DIM))
        rot = [base + int(j) for j in rs_g]
        half = h // GQA_GROUP
        q_pad += own + pad(HEAD_DIM) if half == 0 else pad(HEAD_DIM) + own
        q_rot += rot + pad(HEAD_DIM) if half == 0 else pad(HEAD_DIM) + rot
    k_nat = list(range(gk0, gv0))
    k_rot = [gk0 + kv * HEAD_DIM + int(j) for kv in range(GQA_KV_HEADS) for j in rs_g]
    seg_c = q_pad + q_rot + k_nat + k_rot + list(range(gv0, gz0))
    seg_z = (list(range(2 * GLA_QK + GLA_WIDTH, c0)) + list(range(kr0 + MLA_ROPE, g0))
             + list(range(gz0, gz0 + GQA_WIDTH)))
    cols = seg_a + seg_b + seg_c + seg_z
    assert len(seg_a) == SEG_A and len(seg_b) == SEG_B and len(seg_c) == SEG_C and len(cols) == W_IN_PAD
    return cols


def _layer_weights(l, norm_w, w_in, gla_w_a_fwd, gla_b_a_fwd, gla_w_a_bwd, gla_b_a_bwd, gla_norm_w,
                   mla_q_norm_w, mla_w_uq, mla_kv_norm_w, mla_w_ukv, gqa_q_norm_w, gqa_k_norm_w,
                   w_out):
    w_pad = _take_cols(w_in[l], _w_in_columns()).astype(BF16)

    w_a = jnp.zeros((LANES, 2 * GLA_QK), F32)
    w_a = w_a.at[0:GLA_GATE_RANK, 0:GLA_QK].set(gla_w_a_fwd[l])
    w_a = w_a.at[GLA_GATE_RANK:2 * GLA_GATE_RANK, GLA_QK:].set(gla_w_a_bwd[l])
    b_a = jnp.concatenate([gla_b_a_fwd[l], gla_b_a_bwd[l]])[None, :]

    dq = MLA_NOPE + MLA_ROPE
    rs_m, rs_g = _rot_src(MLA_ROPE), _rot_src(HEAD_DIM)
    uq_cols, uq_rot_cols = [], []
    for h in range(MLA_HEADS):
        uq_cols += list(range(h * dq, (h + 1) * dq)) + [-1] * (LANES - dq)
        uq_rot_cols += ([-1] * MLA_NOPE + [h * dq + MLA_NOPE + int(j) for j in rs_m]
                        + [-1] * (LANES - dq))
    ukv = mla_w_ukv[l].reshape(MLA_KV_LORA, MLA_HEADS, MLA_NOPE + MLA_V)
    uk = jnp.pad(ukv[:, :, :MLA_NOPE], ((0, 0), (0, 0), (0, LANES - MLA_NOPE)))
    uk = uk.reshape(MLA_KV_LORA, MLA_HEADS * LANES)
    uv = ukv[:, :, MLA_NOPE:].reshape(MLA_KV_LORA, MLA_WIDTH)

    both = lambda v: jnp.tile(v, LANES // HEAD_DIM)[None, :]
    return {
        "norm_w": norm_w[l][None, :],
        "w_in": w_pad,
        "w_a": w_a.astype(BF16),
        "b_a": b_a,
        "mla_q_norm": mla_q_norm_w[l][None, :],
        "w_uq": _take_cols(mla_w_uq[l], uq_cols).astype(BF16),
        "w_uq_rot": _take_cols(mla_w_uq[l], uq_rot_cols).astype(BF16),
        "mla_kv_norm": mla_kv_norm_w[l][None, :],
        "w_uk": uk.astype(BF16),
        "w_uv": uv.astype(BF16),
        "gqa_q_norm": both(gqa_q_norm_w[l]),
        "gqa_q_norm_rot": both(gqa_q_norm_w[l][rs_g]),
        "gqa_k_norm": both(gqa_k_norm_w[l]),
        "gqa_k_norm_rot": both(gqa_k_norm_w[l][rs_g]),
        "gla_norm": jnp.tile(gla_norm_w[l], GLA_HEADS)[None, :],
        "w_out": w_out[l].astype(BF16),
    }


def kernel(x, c, ctx, c_ctx, norm_w, w_mod, b_mod, w_in, gla_w_a_fwd, gla_b_a_fwd, gla_w_a_bwd,
           gla_b_a_bwd, gla_norm_w, mla_q_norm_w, mla_w_uq, mla_kv_norm_w, mla_w_ukv,
           gqa_q_norm_w, gqa_k_norm_w, w_out, final_norm_w):
    bsz, n, _ = x.shape
    depth = w_in.shape[0]
    assert ctx.shape[1] == CTX_LEN and n % (2 * TM) == 0 and n % GRID_W == 0
    rows = n // GRID_W
    rope_m = _rope_tables(rows, MLA_ROPE, (MLA_NOPE,))
    rope_g = _rope_tables(rows, HEAD_DIM, (0, HEAD_DIM))

    mod_rows = -(-(bsz + 1) // 8) * 8
    cs = jnp.zeros((mod_rows, D_MODEL), F32).at[:bsz].set(c).at[bsz].set(c_ctx)
    mod_all = _modulation(cs, w_mod, b_mod)
    fnw = final_norm_w[None, :]

    xs = jnp.concatenate([ctx, x], axis=1)
    for l in range(depth):
        lw = _layer_weights(l, norm_w, w_in, gla_w_a_fwd, gla_b_a_fwd, gla_w_a_bwd, gla_b_a_bwd,
                            gla_norm_w, mla_q_norm_w, mla_w_uq, mla_kv_norm_w, mla_w_ukv,
                            gqa_q_norm_w, gqa_k_norm_w, w_out)
        m = mod_all[l]
        mod = jnp.stack([jnp.broadcast_to(m[bsz], (bsz, 3 * D_MODEL)), m[:bsz]], axis=1)
        mod = mod.reshape(2 * bsz, 1, 3 * D_MODEL)
        gq, gk, gv, gg, zg, mq, mk, mvt, cq, ck, cvt = _prep(xs, mod, lw, rope_m, rope_g)
        o_f, o_b = _gla(gq, gk, gv, gg)
        y_b = _attention(mq, mk, mvt, shared_kv=False)
        y_c = _attention(cq, ck, cvt, shared_kv=True)
        xs = _outproj(xs, o_f, o_b, y_b, y_c, zg, mod, lw, fnw, final=(l == depth - 1))
    return xs
```

```python
import functools

import numpy as np
import jax
import jax.numpy as jnp
from jax import lax
from jax.experimental import pallas as pl
from jax.experimental.pallas import tpu as pltpu

D_MODEL = 1024
GRID_W = 64
CTX_LEN = 256
HEAD_DIM = 64
ROPE_THETA = 10000.0
EPS = 1e-6

GLA_HEADS = 4
GLA_DK = 32
GLA_DV = 64
GLA_WIDTH = GLA_HEADS * GLA_DV
GLA_QK = GLA_HEADS * GLA_DK
GLA_GATE_RANK = 16
GLA_GATE_NORM = 16.0
GLA_CHUNK = 64

MLA_HEADS = 6
MLA_NOPE = 64
MLA_ROPE = 32
MLA_V = 64
MLA_Q_LORA = 256
MLA_KV_LORA = 256
MLA_WIDTH = MLA_HEADS * MLA_V

GQA_HEADS = 6
GQA_KV_HEADS = 2
GQA_GROUP = GQA_HEADS // GQA_KV_HEADS
GQA_WIDTH = GQA_HEADS * HEAD_DIM

D_MIX = GLA_WIDTH + MLA_WIDTH + GQA_WIDTH

LANES = 128
TM = CTX_LEN
CHUNKS = TM // GLA_CHUNK
SEG_A = 640
SEG_B = 768
SEG_C = 1920
W_IN_PAD = SEG_A + SEG_B + SEG_C + D_MIX
VMEM_LIMIT = 48 * 1024 * 1024
ATT_CK = 256
ATT_SUB = 32
LOG2E = 1.4426950408889634

F32 = jnp.float32
BF16 = jnp.bfloat16


def _cparams(sem):
    return pltpu.CompilerParams(dimension_semantics=sem, vmem_limit_bytes=VMEM_LIMIT)


def _dot(a, b):
    return jnp.dot(a, b, preferred_element_type=F32)


def _dot_nt(a, b):
    return lax.dot_general(a, b, (((1,), (1,)), ((), ())), preferred_element_type=F32)


def _dot_tn(a, b):
    return lax.dot_general(a, b, (((0,), (0,)), ((), ())), preferred_element_type=F32)


def _silu(v):
    return v / (1.0 + jnp.exp(-v))


def _mod_kernel(c_ref, w_ref, b_ref, o_ref):
    s = _silu(c_ref[...]).astype(BF16)
    o_ref[...] = _dot(s, w_ref[...].astype(BF16)) + b_ref[...]


def _modulation(cs, w_mod, b_mod):
    depth = w_mod.shape[0]
    rows = cs.shape[0]
    return pl.pallas_call(
        _mod_kernel,
        grid=(depth, 3),
        in_specs=[
            pl.BlockSpec((rows, D_MODEL), lambda l, j: (0, 0)),
            pl.BlockSpec((None, D_MODEL, D_MODEL), lambda l, j: (l, 0, j)),
            pl.BlockSpec((None, 1, D_MODEL), lambda l, j: (l, 0, j)),
        ],
        out_specs=pl.BlockSpec((None, rows, D_MODEL), lambda l, j: (l, 0, j)),
        out_shape=jax.ShapeDtypeStruct((depth, rows, 3 * D_MODEL), F32),
        compiler_params=_cparams(("arbitrary", "arbitrary")),
        name="modulation",
    )(cs, w_mod, b_mod.reshape(depth, 1, 3 * D_MODEL))


def _prep_kernel(x_ref, mod_ref, nw_ref, win_ref, wa_ref, ba_ref, mqn_ref, wuq_ref, wuqr_ref,
                 mkvn_ref, wuk_ref, wuv_ref, gqn_ref, gqnr_ref, gkn_ref, gknr_ref, rm_ref, rg_ref,
                 gq_ref, gk_ref, gv_ref, gg_ref, zg_ref, mq_ref, mk_ref, mvt_ref, cq_ref, ck_ref,
                 cvt_ref):
    x = x_ref[...]
    mod = mod_ref[...]
    shift, scale = mod[:, :D_MODEL], mod[:, D_MODEL:2 * D_MODEL]
    ms = jnp.mean(x * x, axis=-1, keepdims=True)
    h = (x * lax.rsqrt(ms + EPS) * nw_ref[...]) * (1.0 + scale) + shift
    hb = h.astype(BF16)

    uz = _dot(hb, win_ref[:, SEG_A + SEG_B + SEG_C:])
    zg_ref[...] = _silu(uz)

    ua = _dot(hb, win_ref[:, 0:SEG_A])
    gq_ref[...] = (ua[:, 0:GLA_QK] * GLA_DK ** -0.5).astype(BF16)
    gk_ref[...] = ua[:, GLA_QK:2 * GLA_QK].astype(BF16)
    gv_ref[...] = ua[:, 2 * GLA_QK:2 * GLA_QK + GLA_WIDTH].astype(BF16)
    a = ua[:, 2 * GLA_QK + GLA_WIDTH:SEG_A].astype(BF16)
    xg = _dot(a, wa_ref[...]) + ba_ref[...]
    log_sig = jnp.minimum(xg, 0.0) - jnp.log1p(jnp.exp(-jnp.abs(xg)))
    gg_ref[...] = log_sig * (1.0 / GLA_GATE_NORM)

    ub = _dot(hb, win_ref[:, SEG_A:SEG_A + SEG_B])
    cq = ub[:, 0:MLA_Q_LORA]
    ckv = ub[:, MLA_Q_LORA:MLA_Q_LORA + MLA_KV_LORA]
    kr = ub[:, MLA_Q_LORA + MLA_KV_LORA:MLA_Q_LORA + MLA_KV_LORA + LANES]
    kr_rot = ub[:, MLA_Q_LORA + MLA_KV_LORA + LANES:SEG_B]
    cqn = cq * lax.rsqrt(jnp.mean(cq * cq, axis=-1, keepdims=True) + EPS) * mqn_ref[...]
    ckvn = ckv * lax.rsqrt(jnp.mean(ckv * ckv, axis=-1, keepdims=True) + EPS) * mkvn_ref[...]
    cqb = cqn.astype(BF16)
    qm = _dot(cqb, wuq_ref[...])
    qm_rot = _dot(cqb, wuqr_ref[...])
    ckvb = ckvn.astype(BF16)
    kn = _dot(ckvb, wuk_ref[...])
    vm = _dot(ckvb, wuv_ref[...])
    cos_m, sin_m = rm_ref[:, :LANES], rm_ref[:, LANES:]
    k_rope = kr * cos_m + kr_rot * sin_m
    mla_scale = (MLA_NOPE + MLA_ROPE) ** -0.5 * LOG2E
    for hd in range(MLA_HEADS):
        sl = slice(hd * LANES, (hd + 1) * LANES)
        mq_ref[:, sl] = ((qm[:, sl] * cos_m + qm_rot[:, sl] * sin_m) * mla_scale).astype(BF16)
        mk_ref[:, sl] = (kn[:, sl] + k_rope).astype(BF16)
    for p in range(MLA_HEADS // 2):
        mvt_ref[p] = vm[:, p * LANES:(p + 1) * LANES].T.astype(BF16)

    uc = _dot(hb, win_ref[:, SEG_A + SEG_B:SEG_A + SEG_B + SEG_C])
    cos_g, sin_g = rg_ref[:, :LANES], rg_ref[:, LANES:]
    qa, qb = cos_g * gqn_ref[...], sin_g * gqnr_ref[...]
    gqa_scale = HEAD_DIM ** -0.5 * LOG2E
    nq = GQA_HEADS * LANES
    for hd in range(GQA_HEADS):
        sl = slice(hd * LANES, (hd + 1) * LANES)
        xh = uc[:, sl]
        xr = uc[:, nq + hd * LANES:nq + (hd + 1) * LANES]
        msq = jnp.sum(xh * xh, axis=-1, keepdims=True) * (1.0 / HEAD_DIM)
        cq_ref[:, sl] = ((xh * qa + xr * qb) * (lax.rsqrt(msq + EPS) * gqa_scale)).astype(BF16)
    k = uc[:, 2 * nq:2 * nq + LANES]
    k_rot = uc[:, 2 * nq + LANES:2 * nq + 2 * LANES]
    lo = lax.broadcasted_iota(jnp.int32, (TM, LANES), 1) < HEAD_DIM
    k2 = k * k
    ms_lo = jnp.sum(jnp.where(lo, k2, 0.0), axis=-1, keepdims=True) * (1.0 / HEAD_DIM)
    ms_hi = jnp.sum(jnp.where(lo, 0.0, k2), axis=-1, keepdims=True) * (1.0 / HEAD_DIM)
    inv = jnp.where(lo, lax.rsqrt(ms_lo + EPS), lax.rsqrt(ms_hi + EPS))
    ck_ref[...] = ((k * (cos_g * gkn_ref[...]) + k_rot * (sin_g * gknr_ref[...])) * inv).astype(BF16)
    cvt_ref[...] = uc[:, 2 * nq + 2 * LANES:SEG_C].T.astype(BF16)


def _prep(xs, mod, lw, rope_m, rope_g):
    bsz, t, _ = xs.shape
    nt = t // TM
    tok = lambda w: pl.BlockSpec((None, TM, w), lambda b, i: (b, i, 0))
    full = lambda a: pl.BlockSpec(a.shape, lambda b, i: (0,) * a.ndim)
    tab = pl.BlockSpec((TM, 2 * LANES), lambda b, i: (i, 0))
    weights = [lw[n] for n in ("norm_w", "w_in", "w_a", "b_a", "mla_q_norm", "w_uq", "w_uq_rot",
                               "mla_kv_norm", "w_uk", "w_uv", "gqa_q_norm", "gqa_q_norm_rot",
                               "gqa_k_norm", "gqa_k_norm_rot")]
    out_shape = [
        jax.ShapeDtypeStruct((bsz, t, GLA_QK), BF16),
        jax.ShapeDtypeStruct((bsz, t, GLA_QK), BF16),
        jax.ShapeDtypeStruct((bsz, t, GLA_WIDTH), BF16),
        jax.ShapeDtypeStruct((bsz, t, 2 * GLA_QK), F32),
        jax.ShapeDtypeStruct((bsz, t, D_MIX), F32),
        jax.ShapeDtypeStruct((bsz, t, MLA_HEADS * LANES), BF16),
        jax.ShapeDtypeStruct((bsz, t, MLA_HEADS * LANES), BF16),
        jax.ShapeDtypeStruct((bsz, MLA_HEADS // 2, LANES, t), BF16),
        jax.ShapeDtypeStruct((bsz, t, GQA_HEADS * LANES), BF16),
        jax.ShapeDtypeStruct((bsz, t, LANES), BF16),
        jax.ShapeDtypeStruct((bsz, LANES, t), BF16),
    ]
    out_specs = [
        tok(GLA_QK), tok(GLA_QK), tok(GLA_WIDTH), tok(2 * GLA_QK), tok(D_MIX),
        tok(MLA_HEADS * LANES), tok(MLA_HEADS * LANES),
        pl.BlockSpec((None, MLA_HEADS // 2, LANES, TM), lambda b, i: (b, 0, 0, i)),
        tok(GQA_HEADS * LANES), tok(LANES),
        pl.BlockSpec((None, LANES, TM), lambda b, i: (b, 0, i)),
    ]
    return pl.pallas_call(
        _prep_kernel,
        grid=(bsz, nt),
        in_specs=[tok(D_MODEL),
                  pl.BlockSpec((None, 1, 3 * D_MODEL), lambda b, i: (2 * b + jnp.minimum(i, 1), 0, 0))]
                 + [full(w) for w in weights] + [tab, tab],
        out_specs=out_specs,
        out_shape=out_shape,
        compiler_params=_cparams(("parallel", "arbitrary")),
        name="prep",
    )(xs, mod, *weights, rope_m, rope_g)


def _gla_direction(q_ref, k_ref, v_ref, g_ref, l_ref, cmask_ref, qmask, omask, smask, s_ref, o_ref,
                   reverse):
    q = q_ref[...].astype(F32)
    k = k_ref[...].astype(F32)
    g = g_ref[...]
    g_hi = g.astype(BF16)
    r1 = g - g_hi.astype(F32)
    g_mid = r1.astype(BF16)
    g_lo = (r1 - g_mid.astype(F32)).astype(BF16)
    tri = l_ref[...]
    cum = _dot(tri, g_hi) + _dot(tri, g_mid) + _dot(tri, g_lo)
    cmask = cmask_ref[...]
    order = range(CHUNKS - 1, -1, -1) if reverse else range(CHUNKS)
    for c in order:
        sl = slice(c * GLA_CHUNK, (c + 1) * GLA_CHUNK)
        cum_c = cum[sl]
        last = cum_c[0:1] if reverse else cum_c[GLA_CHUNK - 1:GLA_CHUNK]
        qd = q[sl] * jnp.exp(cum_c)
        ki = k[sl] * jnp.exp(-cum_c)
        kd = k[sl] * jnp.exp(last - cum_c)
        vc = v_ref[sl, :]
        qh = jnp.concatenate([qd] * GLA_HEADS, axis=0) * qmask
        att = _dot_nt(qh.astype(BF16), ki.astype(BF16)) * cmask
        o_all = _dot(att.astype(BF16), vc) * omask
        o_intra = o_all[0:GLA_CHUNK]
        for hd in range(1, GLA_HEADS):
            o_intra = o_intra + o_all[hd * GLA_CHUNK:(hd + 1) * GLA_CHUNK]
        st = s_ref[...]
        o_inter = _dot_nt(qd.astype(BF16), st.astype(BF16))
        o_ref[sl, :] = o_inter + o_intra
        ds = _dot_tn(vc, kd.astype(BF16)) * smask
        s_ref[...] = st * jnp.exp(last) + ds


def _gla_kernel(qf_ref, kf_ref, vf_ref, gf_ref, qb_ref, kb_ref, vb_ref, gb_ref, lf_ref, lb_ref,
                cmf_ref, cmb_ref, qmask_ref, omask_ref, smask_ref, of_ref, ob_ref, sf_ref, sb_ref):
    @pl.when(pl.program_id(1) == 0)
    def _():
        sf_ref[...] = jnp.zeros_like(sf_ref)
        sb_ref[...] = jnp.zeros_like(sb_ref)

    qmask, omask, smask = qmask_ref[...], omask_ref[...], smask_ref[...]
    _gla_direction(qf_ref, kf_ref, vf_ref, gf_ref, lf_ref, cmf_ref, qmask, omask, smask, sf_ref,
                   of_ref, False)
    _gla_direction(qb_ref, kb_ref, vb_ref, gb_ref, lb_ref, cmb_ref, qmask, omask, smask, sb_ref,
                   ob_ref, True)


def _gla_consts():
    r = np.arange(TM)
    same = (r[:, None] // GLA_CHUNK) == (r[None, :] // GLA_CHUNK)
    lf = (same & (r[None, :] <= r[:, None])).astype(np.float32)
    lb = (same & (r[None, :] >= r[:, None])).astype(np.float32)
    rows = np.arange(GLA_HEADS * GLA_CHUNK)
    j = np.arange(GLA_CHUNK)
    cmf = ((rows[:, None] % GLA_CHUNK) >= j[None, :]).astype(np.float32)
    cmb = ((rows[:, None] % GLA_CHUNK) <= j[None, :]).astype(np.float32)
    qmask = ((rows[:, None] // GLA_CHUNK) == (np.arange(GLA_QK)[None, :] // GLA_DK)).astype(np.float32)
    omask = ((rows[:, None] // GLA_CHUNK) == (np.arange(GLA_WIDTH)[None, :] // GLA_DV)).astype(np.float32)
    smask = ((np.arange(GLA_WIDTH)[:, None] // GLA_DV) == (np.arange(GLA_QK)[None, :] // GLA_DK)).astype(np.float32)
    return (jnp.asarray(lf, BF16), jnp.asarray(lb, BF16), jnp.asarray(cmf), jnp.asarray(cmb),
            jnp.asarray(qmask), jnp.asarray(omask), jnp.asarray(smask))


def _gla(gq, gk, gv, gg):
    bsz, t, _ = gq.shape
    nt = t // TM
    consts = _gla_consts()
    fwd = lambda b, i: (b, i, 0)
    rev_blk = lambda i: jnp.where(i == 0, 0, nt - i)
    bwd = lambda b, i: (b, rev_blk(i), 0)
    bwd_g = lambda b, i: (b, rev_blk(i), 1)
    spec = lambda w, im: pl.BlockSpec((None, TM, w), im)
    full = lambda a: pl.BlockSpec(a.shape, lambda b, i: (0,) * a.ndim)
    return pl.pallas_call(
        _gla_kernel,
        grid=(bsz, nt),
        in_specs=[spec(GLA_QK, fwd), spec(GLA_QK, fwd), spec(GLA_WIDTH, fwd), spec(GLA_QK, fwd),
                  spec(GLA_QK, bwd), spec(GLA_QK, bwd), spec(GLA_WIDTH, bwd), spec(GLA_QK, bwd_g)]
                 + [full(a) for a in consts],
        out_specs=[spec(GLA_WIDTH, fwd), spec(GLA_WIDTH, bwd)],
        out_shape=[jax.ShapeDtypeStruct((bsz, t, GLA_WIDTH), F32)] * 2,
        scratch_shapes=[pltpu.VMEM((GLA_WIDTH, GLA_QK), F32)] * 2,
        compiler_params=_cparams(("parallel", "arbitrary")),
        name="gla",
    )(gq, gk, gv, gg, gq, gk, gv, gg, *consts)


def _head_values(vt_ref, h, cols, shared_kv):
    if shared_kv:
        r0 = (h // GQA_GROUP) * HEAD_DIM
        return vt_ref[r0:r0 + HEAD_DIM, cols]
    r0 = (h % 2) * MLA_V
    return vt_ref[h // 2, r0:r0 + MLA_V, cols]


def _write_heads(o_ref, outs_t, pair0=0):
    for p in range(len(outs_t) // 2):
        pair = jnp.concatenate([outs_t[2 * p], outs_t[2 * p + 1]], axis=0)
        o_ref[:, (pair0 + p) * LANES:(pair0 + p + 1) * LANES] = pair.T


def _attn_ctx_kernel(q_ref, k_ref, vt_ref, o_ref, *, shared_kv):
    heads = q_ref.shape[1] // LANES
    outs_t = []
    for h in range(heads):
        kh = k_ref[...] if shared_kv else k_ref[:, h * LANES:(h + 1) * LANES]
        s = _dot_nt(kh, q_ref[:, h * LANES:(h + 1) * LANES])
        p = jnp.exp2(s - jnp.max(s, axis=0, keepdims=True))
        l = jnp.sum(p, axis=0, keepdims=True)
        outs_t.append(_dot(_head_values(vt_ref, h, slice(None), shared_kv), p.astype(BF16)) / l)
    _write_heads(o_ref, outs_t)


def _attn_kernel(z_ref, qa_ref, qb_ref, qna_ref, qnb_ref, k_ref, vt_ref, o_ref, s_ref, p_ref,
                 m_ref, *, shared_kv):
    t = k_ref.shape[0]
    heads = qa_ref.shape[1] // LANES
    chunks = [slice(a, a + ATT_CK) for a in range(0, t, ATT_CK)]
    z = z_ref[0]

    def dyn(start, size):
        return pl.ds(pl.multiple_of(z + start, ATT_SUB), size)

    def q_t(ra_ref, rb_ref, h):
        sl = slice(h * LANES, (h + 1) * LANES)
        qh = jnp.concatenate([ra_ref[:, sl], rb_ref[:, sl]], axis=0)
        return qh.astype(F32).T.astype(BF16)

    def fold8(v):
        return v.reshape(v.shape[0] // 8, 8, v.shape[1])

    def qk(h, qh, rows):
        kh = k_ref[rows, :] if shared_kv else k_ref[rows, h * LANES:(h + 1) * LANES]
        return _dot(kh, qh)

    def put_scores(s, rows, m8):
        s_ref[dyn(rows.start, ATT_CK), :] = s
        smax = jnp.max(fold8(s), axis=0)
        return smax if m8 is None else jnp.maximum(m8, smax)

    ones = jnp.ones((16, ATT_CK), BF16)

    def head_pass(h, m8_in, nxt):
        mb = jnp.broadcast_to(jnp.max(m8_in, axis=0, keepdims=True), (ATT_SUB, m8_in.shape[1]))
        m8, acc = None, None
        if nxt is not None:
            hn, qn = nxt
            s_next = qk(hn, qn, chunks[0])
        for c, rows in enumerate(chunks):
            if nxt is not None:
                s_cur = s_next
                if c + 1 < len(chunks):
                    s_next = qk(hn, qn, chunks[c + 1])
            vt1 = jnp.concatenate([_head_values(vt_ref, h, rows, shared_kv), ones], axis=0)
            p0 = (c % 2) * ATT_CK
            for r in range(0, ATT_CK, ATT_SUB):
                p = jnp.exp2(s_ref[dyn(rows.start + r, ATT_SUB), :] - mb)
                p_ref[p0 + r:p0 + r + ATT_SUB, :] = p.astype(BF16)
            if nxt is not None:
                m8 = put_scores(s_cur, rows, m8)
            pv = _dot(vt1, p_ref[p0:p0 + ATT_CK, :])
            acc = pv if acc is None else acc + pv
        return acc[0:HEAD_DIM] / acc[HEAD_DIM:HEAD_DIM + 1], m8

    step = pl.program_id(1)
    last_step = pl.num_programs(1) - 1

    @pl.when(step == 0)
    def _():
        m8 = None
        qh = q_t(qa_ref, qb_ref, 0)
        for rows in chunks:
            m8 = put_scores(qk(0, qh, rows), rows, m8)
        m_ref[...] = m8

    m8 = m_ref[...]
    outs_t = []
    for h in range(heads - 1):
        out_t, m8 = head_pass(h, m8, (h + 1, q_t(qa_ref, qb_ref, h + 1)))
        outs_t.append(out_t)
    _write_heads(o_ref, outs_t[:heads - 2], 0)

    @pl.when(step < last_step)
    def _():
        out_t, m8_next = head_pass(heads - 1, m8, (0, q_t(qna_ref, qnb_ref, 0)))
        m_ref[...] = m8_next
        _write_heads(o_ref, [outs_t[heads - 2], out_t], heads // 2 - 1)

    @pl.when(step == last_step)
    def _():
        out_t, _ = head_pass(heads - 1, m8, None)
        _write_heads(o_ref, [outs_t[heads - 2], out_t], heads // 2 - 1)


def _attention(q, k, vt, shared_kv):
    bsz, t, width = q.shape
    n2 = (t - CTX_LEN) // (2 * TM)
    kw = k.shape[2]
    name = "attn_gqa" if shared_kv else "attn_mla"
    if shared_kv:
        v_spec = pl.BlockSpec((None, LANES, t), lambda b, j: (b, 0, 0))
        vc_spec = pl.BlockSpec((None, LANES, CTX_LEN), lambda b: (b, 0, 0))
    else:
        v_spec = pl.BlockSpec((None, vt.shape[1], LANES, t), lambda b, j: (b, 0, 0, 0))
        vc_spec = pl.BlockSpec((None, vt.shape[1], LANES, CTX_LEN), lambda b: (b, 0, 0, 0))
    y_lat = pl.pallas_call(
        functools.partial(_attn_kernel, shared_kv=shared_kv),
        grid=(bsz, n2),
        in_specs=[pl.BlockSpec(memory_space=pltpu.SMEM),
                  pl.BlockSpec((None, TM, width), lambda b, j: (b, 2 * j + 1, 0)),
                  pl.BlockSpec((None, TM, width), lambda b, j: (b, 2 * j + 2, 0)),
                  pl.BlockSpec((None, TM, width), lambda b, j: (b, jnp.minimum(2 * j + 3, 2 * n2), 0)),
                  pl.BlockSpec((None, TM, width), lambda b, j: (b, jnp.minimum(2 * j + 4, 2 * n2), 0)),
                  pl.BlockSpec((None, t, kw), lambda b, j: (b, 0, 0)), v_spec],
        out_specs=pl.BlockSpec((None, 2 * TM, width // 2), lambda b, j: (b, j, 0)),
        out_shape=jax.ShapeDtypeStruct((bsz, t - CTX_LEN, width // 2), F32),
        scratch_shapes=[pltpu.VMEM((t, 2 * TM), F32), pltpu.VMEM((2 * ATT_CK, 2 * TM), BF16),
                        pltpu.VMEM((8, 2 * TM), F32)],
        compiler_params=_cparams(("parallel", "arbitrary")),
        name=name,
    )(jnp.zeros((1,), jnp.int32), q, q, q, q, k, vt)
    y_ctx = pl.pallas_call(
        functools.partial(_attn_ctx_kernel, shared_kv=shared_kv),
        grid=(bsz,),
        in_specs=[pl.BlockSpec((None, CTX_LEN, width), lambda b: (b, 0, 0)),
                  pl.BlockSpec((None, CTX_LEN, kw), lambda b: (b, 0, 0)), vc_spec],
        out_specs=pl.BlockSpec((None, CTX_LEN, width // 2), lambda b: (b, 0, 0)),
        out_shape=jax.ShapeDtypeStruct((bsz, CTX_LEN, width // 2), F32),
        compiler_params=_cparams(("parallel",)),
        name=name + "_ctx",
    )(q, k, vt)
    return y_lat, y_ctx


def _outproj_kernel(x_ref, of_ref, ob_ref, yb_ref, ybc_ref, yc_ref, ycc_ref, zg_ref, gnw_ref,
                    mod_ref, wout_ref, fnw_ref, o_ref, *, final):
    lo = lax.broadcasted_iota(jnp.int32, (TM, LANES), 1) < GLA_DV
    o = of_ref[...] + ob_ref[...]
    gnw = gnw_ref[...]
    zg = zg_ref[...]
    parts = []
    for j in range(GLA_WIDTH // LANES):
        sl = slice(j * LANES, (j + 1) * LANES)
        oj = o[:, sl]
        o2 = oj * oj
        ms_lo = jnp.sum(jnp.where(lo, o2, 0.0), axis=-1, keepdims=True) * (1.0 / GLA_DV)
        ms_hi = jnp.sum(jnp.where(lo, 0.0, o2), axis=-1, keepdims=True) * (1.0 / GLA_DV)
        inv = jnp.where(lo, lax.rsqrt(ms_lo + EPS), lax.rsqrt(ms_hi + EPS))
        parts.append(oj * inv * gnw[:, sl] * zg[:, sl])
    if final:
        yb, yc = yb_ref[...], yc_ref[...]
    else:
        is_ctx = pl.program_id(1) == 0
        yb = jnp.where(is_ctx, ybc_ref[...], yb_ref[...])
        yc = jnp.where(is_ctx, ycc_ref[...], yc_ref[...])
    parts.append(yb * zg[:, GLA_WIDTH:GLA_WIDTH + MLA_WIDTH])
    parts.append(yc * zg[:, GLA_WIDTH + MLA_WIDTH:])
    y = jnp.concatenate(parts, axis=-1).astype(BF16)
    gate = mod_ref[...]
    xn = x_ref[...] + gate * _dot(y, wout_ref[...])
    if final:
        xn = xn * lax.rsqrt(jnp.mean(xn * xn, axis=-1, keepdims=True) + EPS) * fnw_ref[...]
    o_ref[...] = xn


def _outproj(xs, o_f, o_b, y_b, y_c, zg, mod, lw, final_norm_w, final):
    bsz, t, _ = xs.shape
    off = 1 if final else 0
    nt = t // TM - off
    tok = lambda w: pl.BlockSpec((None, TM, w), lambda b, i: (b, i + off, 0))
    lat = lambda w: pl.BlockSpec((None, TM, w), lambda b, i: (b, jnp.maximum(i + off - 1, 0), 0))
    ctx = lambda w: pl.BlockSpec((None, CTX_LEN, w), lambda b, i: (b, 0, 0))
    full = lambda a: pl.BlockSpec(a.shape, lambda b, i: (0,) * a.ndim)
    return pl.pallas_call(
        functools.partial(_outproj_kernel, final=final),
        grid=(bsz, nt),
        in_specs=[tok(D_MODEL), tok(GLA_WIDTH), tok(GLA_WIDTH), lat(MLA_WIDTH), ctx(MLA_WIDTH),
                  lat(GQA_WIDTH), ctx(GQA_WIDTH), tok(D_MIX), full(lw["gla_norm"]),
                  pl.BlockSpec((None, 1, D_MODEL),
                               lambda b, i: (2 * b + jnp.minimum(i + off, 1), 0, 2)),
                  full(lw["w_out"]), full(final_norm_w)],
        out_specs=pl.BlockSpec((None, TM, D_MODEL), lambda b, i: (b, i, 0)),
        out_shape=jax.ShapeDtypeStruct((bsz, nt * TM, D_MODEL), F32),
        compiler_params=_cparams(("parallel", "arbitrary")),
        name="outproj",
    )(xs, o_f, o_b, *y_b, *y_c, zg, lw["gla_norm"], mod, lw["w_out"], final_norm_w)


def _rot_src(d_rot):
    quarter = d_rot // 4
    i = np.arange(d_rot)
    return np.where(i % (2 * quarter) < quarter, i + quarter, i - quarter)


def _rope_tables(rows, d_rot, lanes):
    quarter = d_rot // 4
    n = rows * GRID_W
    row = jnp.repeat(jnp.arange(rows), GRID_W).astype(F32)
    col = jnp.tile(jnp.arange(GRID_W), rows).astype(F32)
    freqs = ROPE_THETA ** (-jnp.arange(quarter, dtype=F32) / quarter)
    ang = jnp.stack([row[:, None] * freqs, col[:, None] * freqs], axis=1)
    cos, sin = jnp.cos(ang), jnp.sin(ang)
    cos_l = jnp.stack([cos, cos], axis=2).reshape(n, d_rot)
    sin_l = jnp.stack([-sin, sin], axis=2).reshape(n, d_rot)
    cos_t = jnp.ones((n, LANES), F32)
    sin_t = jnp.zeros((n, LANES), F32)
    for lane0 in lanes:
        cos_t = cos_t.at[:, lane0:lane0 + d_rot].set(cos_l)
        sin_t = sin_t.at[:, lane0:lane0 + d_rot].set(sin_l)
    ident = jnp.concatenate([jnp.ones((CTX_LEN, LANES), F32), jnp.zeros((CTX_LEN, LANES), F32)], 1)
    return jnp.concatenate([ident, jnp.concatenate([cos_t, sin_t], axis=1)], axis=0)


def _take_cols(w, idx):
    w_ext = jnp.concatenate([w, jnp.zeros((w.shape[0], 1), w.dtype)], axis=1)
    return w_ext[:, np.asarray(idx)]


def _w_in_columns():
    c0 = 2 * GLA_QK + 2 * GLA_WIDTH
    m0 = c0 + 2 * GLA_GATE_RANK
    kr0 = m0 + MLA_Q_LORA + MLA_KV_LORA
    g0 = kr0 + MLA_ROPE + MLA_WIDTH
    gk0 = g0 + GQA_WIDTH
    gv0 = gk0 + GQA_KV_HEADS * HEAD_DIM
    gz0 = gv0 + GQA_KV_HEADS * HEAD_DIM
    pad = lambda n: [-1] * n
    seg_a = list(range(0, 2 * GLA_QK + GLA_WIDTH)) + list(range(c0, m0)) + pad(LANES - 2 * GLA_GATE_RANK)
    rs_m, rs_g = _rot_src(MLA_ROPE), _rot_src(HEAD_DIM)
    tail = pad(LANES - MLA_NOPE - MLA_ROPE)
    seg_b = (list(range(m0, kr0)) + pad(MLA_NOPE) + list(range(kr0, kr0 + MLA_ROPE)) + tail
             + pad(MLA_NOPE) + [kr0 + int(j) for j in rs_m] + tail)
    q_pad, q_rot = [], []
    for h in range(GQA_HEADS):
        base = g0 + h * HEAD_DIM
        own = list(range(base, base + HEAD_DIM))
        rot = [base + int(j) for j in rs_g]
        half = h // GQA_GROUP
        q_pad += own + pad(HEAD_DIM) if half == 0 else pad(HEAD_DIM) + own
        q_rot += rot + pad(HEAD_DIM) if half == 0 else pad(HEAD_DIM) + rot
    k_nat = list(range(gk0, gv0))
    k_rot = [gk0 + kv * HEAD_DIM + int(j) for kv in range(GQA_KV_HEADS) for j in rs_g]
    seg_c = q_pad + q_rot + k_nat + k_rot + list(range(gv0, gz0))
    seg_z = (list(range(2 * GLA_QK + GLA_WIDTH, c0)) + list(range(kr0 + MLA_ROPE, g0))
             + list(range(gz0, gz0 + GQA_WIDTH)))
    cols = seg_a + seg_b + seg_c + seg_z
    assert len(seg_a) == SEG_A and len(seg_b) == SEG_B and len(seg_c) == SEG_C and len(cols) == W_IN_PAD
    return cols


def _layer_weights(l, norm_w, w_in, gla_w_a_fwd, gla_b_a_fwd, gla_w_a_bwd, gla_b_a_bwd, gla_norm_w,
                   mla_q_norm_w, mla_w_uq, mla_kv_norm_w, mla_w_ukv, gqa_q_norm_w, gqa_k_norm_w,
                   w_out):
    w_pad = _take_cols(w_in[l], _w_in_columns()).astype(BF16)

    w_a = jnp.zeros((LANES, 2 * GLA_QK), F32)
    w_a = w_a.at[0:GLA_GATE_RANK, 0:GLA_QK].set(gla_w_a_fwd[l])
    w_a = w_a.at[GLA_GATE_RANK:2 * GLA_GATE_RANK, GLA_QK:].set(gla_w_a_bwd[l])
    b_a = jnp.concatenate([gla_b_a_fwd[l], gla_b_a_bwd[l]])[None, :]

    dq = MLA_NOPE + MLA_ROPE
    rs_m, rs_g = _rot_src(MLA_ROPE), _rot_src(HEAD_DIM)
    uq_cols, uq_rot_cols = [], []
    for h in range(MLA_HEADS):
        uq_cols += list(range(h * dq, (h + 1) * dq)) + [-1] * (LANES - dq)
        uq_rot_cols += ([-1] * MLA_NOPE + [h * dq + MLA_NOPE + int(j) for j in rs_m]
                        + [-1] * (LANES - dq))
    ukv = mla_w_ukv[l].reshape(MLA_KV_LORA, MLA_HEADS, MLA_NOPE + MLA_V)
    uk = jnp.pad(ukv[:, :, :MLA_NOPE], ((0, 0), (0, 0), (0, LANES - MLA_NOPE)))
    uk = uk.reshape(MLA_KV_LORA, MLA_HEADS * LANES)
    uv = ukv[:, :, MLA_NOPE:].reshape(MLA_KV_LORA, MLA_WIDTH)

    both = lambda v: jnp.tile(v, LANES // HEAD_DIM)[None, :]
    return {
        "norm_w": norm_w[l][None, :],
        "w_in": w_pad,
        "w_a": w_a.astype(BF16),
        "b_a": b_a,
        "mla_q_norm": mla_q_norm_w[l][None, :],
        "w_uq": _take_cols(mla_w_uq[l], uq_cols).astype(BF16),
        "w_uq_rot": _take_cols(mla_w_uq[l], uq_rot_cols).astype(BF16),
        "mla_kv_norm": mla_kv_norm_w[l][None, :],
        "w_uk": uk.astype(BF16),
        "w_uv": uv.astype(BF16),
        "gqa_q_norm": both(gqa_q_norm_w[l]),
        "gqa_q_norm_rot": both(gqa_q_norm_w[l][rs_g]),
        "gqa_k_norm": both(gqa_k_norm_w[l]),
        "gqa_k_norm_rot": both(gqa_k_norm_w[l][rs_g]),
        "gla_norm": jnp.tile(gla_norm_w[l], GLA_HEADS)[None, :],
        "w_out": w_out[l].astype(BF16),
    }


def kernel(x, c, ctx, c_ctx, norm_w, w_mod, b_mod, w_in, gla_w_a_fwd, gla_b_a_fwd, gla_w_a_bwd,
           gla_b_a_bwd, gla_norm_w, mla_q_norm_w, mla_w_uq, mla_kv_norm_w, mla_w_ukv,
           gqa_q_norm_w, gqa_k_norm_w, w_out, final_norm_w):
    bsz, n, _ = x.shape
    depth = w_in.shape[0]
    assert ctx.shape[1] == CTX_LEN and n % (2 * TM) == 0 and n % GRID_W == 0
    rows = n // GRID_W
    rope_m = _rope_tables(rows, MLA_ROPE, (MLA_NOPE,))
    rope_g = _rope_tables(rows, HEAD_DIM, (0, HEAD_DIM))

    mod_rows = -(-(bsz + 1) // 8) * 8
    cs = jnp.zeros((mod_rows, D_MODEL), F32).at[:bsz].set(c).at[bsz].set(c_ctx)
    mod_all = _modulation(cs, w_mod, b_mod)
    fnw = final_norm_w[None, :]

    xs = jnp.concatenate([ctx, x], axis=1)
    for l in range(depth):
        lw = _layer_weights(l, norm_w, w_in, gla_w_a_fwd, gla_b_a_fwd, gla_w_a_bwd, gla_b_a_bwd,
                            gla_norm_w, mla_q_norm_w, mla_w_uq, mla_kv_norm_w, mla_w_ukv,
                            gqa_q_norm_w, gqa_k_norm_w, w_out)
        m = mod_all[l]
        mod = jnp.stack([jnp.broadcast_to(m[bsz], (bsz, 3 * D_MODEL)), m[:bsz]], axis=1)
        mod = mod.reshape(2 * bsz, 1, 3 * D_MODEL)
        gq, gk, gv, gg, zg, mq, mk, mvt, cq, ck, cvt = _prep(xs, mod, lw, rope_m, rope_g)
        o_f, o_b = _gla(gq, gk, gv, gg)
        y_b = _attention(mq, mk, mvt, shared_kv=False)
        y_c = _attention(cq, ck, cvt, shared_kv=True)
        xs = _outproj(xs, o_f, o_b, y_b, y_c, zg, mod, lw, fnw, final=(l == depth - 1))
    return xs
```

```python
import functools

import numpy as np
import jax
import jax.numpy as jnp
from jax import lax
from jax.experimental import pallas as pl
from jax.experimental.pallas import tpu as pltpu

D_MODEL = 1024
GRID_W = 64
CTX_LEN = 256
HEAD_DIM = 64
ROPE_THETA = 10000.0
EPS = 1e-6

GLA_HEADS = 4
GLA_DK = 32
GLA_DV = 64
GLA_WIDTH = GLA_HEADS * GLA_DV
GLA_QK = GLA_HEADS * GLA_DK
GLA_GATE_RANK = 16
GLA_GATE_NORM = 16.0
GLA_CHUNK = 64

MLA_HEADS = 6
MLA_NOPE = 64
MLA_ROPE = 32
MLA_V = 64
MLA_Q_LORA = 256
MLA_KV_LORA = 256
MLA_WIDTH = MLA_HEADS * MLA_V

GQA_HEADS = 6
GQA_KV_HEADS = 2
GQA_GROUP = GQA_HEADS // GQA_KV_HEADS
GQA_WIDTH = GQA_HEADS * HEAD_DIM

D_MIX = GLA_WIDTH + MLA_WIDTH + GQA_WIDTH

LANES = 128
TM = CTX_LEN
CHUNKS = TM // GLA_CHUNK
SEG_A = 640
SEG_B = 768
SEG_C = 1920
W_IN_PAD = SEG_A + SEG_B + SEG_C + D_MIX
VMEM_LIMIT = 48 * 1024 * 1024
ATT_CK = 256
ATT_SUB = 32
LOG2E = 1.4426950408889634

F32 = jnp.float32
BF16 = jnp.bfloat16


def _cparams(sem):
    return pltpu.CompilerParams(dimension_semantics=sem, vmem_limit_bytes=VMEM_LIMIT)


def _dot(a, b):
    return jnp.dot(a, b, preferred_element_type=F32)


def _dot_nt(a, b):
    return lax.dot_general(a, b, (((1,), (1,)), ((), ())), preferred_element_type=F32)


def _dot_tn(a, b):
    return lax.dot_general(a, b, (((0,), (0,)), ((), ())), preferred_element_type=F32)


def _silu(v):
    return v / (1.0 + jnp.exp(-v))


def _mod_kernel(c_ref, w_ref, b_ref, o_ref):
    s = _silu(c_ref[...]).astype(BF16)
    o_ref[...] = _dot(s, w_ref[...].astype(BF16)) + b_ref[...]


def _modulation(cs, w_mod, b_mod):
    depth = w_mod.shape[0]
    rows = cs.shape[0]
    return pl.pallas_call(
        _mod_kernel,
        grid=(depth, 3),
        in_specs=[
            pl.BlockSpec((rows, D_MODEL), lambda l, j: (0, 0)),
            pl.BlockSpec((None, D_MODEL, D_MODEL), lambda l, j: (l, 0, j)),
            pl.BlockSpec((None, 1, D_MODEL), lambda l, j: (l, 0, j)),
        ],
        out_specs=pl.BlockSpec((None, rows, D_MODEL), lambda l, j: (l, 0, j)),
        out_shape=jax.ShapeDtypeStruct((depth, rows, 3 * D_MODEL), F32),
        compiler_params=_cparams(("arbitrary", "arbitrary")),
        name="modulation",
    )(cs, w_mod, b_mod.reshape(depth, 1, 3 * D_MODEL))


def _prep_kernel(x_ref, mod_ref, nw_ref, win_ref, wa_ref, ba_ref, mqn_ref, wuq_ref, wuqr_ref,
                 mkvn_ref, wuk_ref, wuv_ref, gqn_ref, gqnr_ref, gkn_ref, gknr_ref, rm_ref, rg_ref,
                 gq_ref, gk_ref, gv_ref, gg_ref, zg_ref, mq_ref, mk_ref, mvt_ref, cq_ref, ck_ref,
                 cvt_ref):
    x = x_ref[...]
    mod = mod_ref[...]
    shift, scale = mod[:, :D_MODEL], mod[:, D_MODEL:2 * D_MODEL]
    ms = jnp.mean(x * x, axis=-1, keepdims=True)
    h = (x * lax.rsqrt(ms + EPS) * nw_ref[...]) * (1.0 + scale) + shift
    hb = h.astype(BF16)

    uz = _dot(hb, win_ref[:, SEG_A + SEG_B + SEG_C:])
    zg_ref[...] = _silu(uz).astype(BF16)

    ua = _dot(hb, win_ref[:, 0:SEG_A])
    gq_ref[...] = (ua[:, 0:GLA_QK] * GLA_DK ** -0.5).astype(BF16)
    gk_ref[...] = ua[:, GLA_QK:2 * GLA_QK].astype(BF16)
    gv_ref[...] = ua[:, 2 * GLA_QK:2 * GLA_QK + GLA_WIDTH].astype(BF16)
    a = ua[:, 2 * GLA_QK + GLA_WIDTH:SEG_A].astype(BF16)
    xg = _dot(a, wa_ref[...]) + ba_ref[...]
    log_sig = jnp.minimum(xg, 0.0) - jnp.log1p(jnp.exp(-jnp.abs(xg)))
    gg_ref[...] = log_sig * (1.0 / GLA_GATE_NORM)

    ub = _dot(hb, win_ref[:, SEG_A:SEG_A + SEG_B])
    cq = ub[:, 0:MLA_Q_LORA]
    ckv = ub[:, MLA_Q_LORA:MLA_Q_LORA + MLA_KV_LORA]
    kr = ub[:, MLA_Q_LORA + MLA_KV_LORA:MLA_Q_LORA + MLA_KV_LORA + LANES]
    kr_rot = ub[:, MLA_Q_LORA + MLA_KV_LORA + LANES:SEG_B]
    cqn = cq * lax.rsqrt(jnp.mean(cq * cq, axis=-1, keepdims=True) + EPS) * mqn_ref[...]
    ckvn = ckv * lax.rsqrt(jnp.mean(ckv * ckv, axis=-1, keepdims=True) + EPS) * mkvn_ref[...]
    cqb = cqn.astype(BF16)
    qm = _dot(cqb, wuq_ref[...])
    qm_rot = _dot(cqb, wuqr_ref[...])
    ckvb = ckvn.astype(BF16)
    kn = _dot(ckvb, wuk_ref[...])
    vm = _dot(ckvb, wuv_ref[...])
    cos_m, sin_m = rm_ref[:, :LANES], rm_ref[:, LANES:]
    k_rope = kr * cos_m + kr_rot * sin_m
    mla_scale = (MLA_NOPE + MLA_ROPE) ** -0.5 * LOG2E
    for hd in range(MLA_HEADS):
        sl = slice(hd * LANES, (hd + 1) * LANES)
        mq_ref[:, sl] = ((qm[:, sl] * cos_m + qm_rot[:, sl] * sin_m) * mla_scale).astype(BF16)
        mk_ref[:, sl] = (kn[:, sl] + k_rope).astype(BF16)
    for p in range(MLA_HEADS // 2):
        mvt_ref[p] = vm[:, p * LANES:(p + 1) * LANES].T.astype(BF16)

    uc = _dot(hb, win_ref[:, SEG_A + SEG_B:SEG_A + SEG_B + SEG_C])
    cos_g, sin_g = rg_ref[:, :LANES], rg_ref[:, LANES:]
    qa, qb = cos_g * gqn_ref[...], sin_g * gqnr_ref[...]
    gqa_scale = HEAD_DIM ** -0.5 * LOG2E
    nq = GQA_HEADS * LANES
    for hd in range(GQA_HEADS):
        sl = slice(hd * LANES, (hd + 1) * LANES)
        xh = uc[:, sl]
        xr = uc[:, nq + hd * LANES:nq + (hd + 1) * LANES]
        msq = jnp.sum(xh * xh, axis=-1, keepdims=True) * (1.0 / HEAD_DIM)
        cq_ref[:, sl] = ((xh * qa + xr * qb) * (lax.rsqrt(msq + EPS) * gqa_scale)).astype(BF16)
    k = uc[:, 2 * nq:2 * nq + LANES]
    k_rot = uc[:, 2 * nq + LANES:2 * nq + 2 * LANES]
    lo = lax.broadcasted_iota(jnp.int32, (TM, LANES), 1) < HEAD_DIM
    k2 = k * k
    ms_lo = jnp.sum(jnp.where(lo, k2, 0.0), axis=-1, keepdims=True) * (1.0 / HEAD_DIM)
    ms_hi = jnp.sum(jnp.where(lo, 0.0, k2), axis=-1, keepdims=True) * (1.0 / HEAD_DIM)
    inv = jnp.where(lo, lax.rsqrt(ms_lo + EPS), lax.rsqrt(ms_hi + EPS))
    ck_ref[...] = ((k * (cos_g * gkn_ref[...]) + k_rot * (sin_g * gknr_ref[...])) * inv).astype(BF16)
    cvt_ref[...] = uc[:, 2 * nq + 2 * LANES:SEG_C].T.astype(BF16)


def _prep(xs, mod, lw, rope_m, rope_g):
    bsz, t, _ = xs.shape
    nt = t // TM
    tok = lambda w: pl.BlockSpec((None, TM, w), lambda b, i: (b, i, 0))
    full = lambda a: pl.BlockSpec(a.shape, lambda b, i: (0,) * a.ndim)
    tab = pl.BlockSpec((TM, 2 * LANES), lambda b, i: (i, 0))
    weights = [lw[n] for n in ("norm_w", "w_in", "w_a", "b_a", "mla_q_norm", "w_uq", "w_uq_rot",
                               "mla_kv_norm", "w_uk", "w_uv", "gqa_q_norm", "gqa_q_norm_rot",
                               "gqa_k_norm", "gqa_k_norm_rot")]
    out_shape = [
        jax.ShapeDtypeStruct((bsz, t, GLA_QK), BF16),
        jax.ShapeDtypeStruct((bsz, t, GLA_QK), BF16),
        jax.ShapeDtypeStruct((bsz, t, GLA_WIDTH), BF16),
        jax.ShapeDtypeStruct((bsz, t, 2 * GLA_QK), F32),
        jax.ShapeDtypeStruct((bsz, t, D_MIX), BF16),
        jax.ShapeDtypeStruct((bsz, t, MLA_HEADS * LANES), BF16),
        jax.ShapeDtypeStruct((bsz, t, MLA_HEADS * LANES), BF16),
        jax.ShapeDtypeStruct((bsz, MLA_HEADS // 2, LANES, t), BF16),
        jax.ShapeDtypeStruct((bsz, t, GQA_HEADS * LANES), BF16),
        jax.ShapeDtypeStruct((bsz, t, LANES), BF16),
        jax.ShapeDtypeStruct((bsz, LANES, t), BF16),
    ]
    out_specs = [
        tok(GLA_QK), tok(GLA_QK), tok(GLA_WIDTH), tok(2 * GLA_QK), tok(D_MIX),
        tok(MLA_HEADS * LANES), tok(MLA_HEADS * LANES),
        pl.BlockSpec((None, MLA_HEADS // 2, LANES, TM), lambda b, i: (b, 0, 0, i)),
        tok(GQA_HEADS * LANES), tok(LANES),
        pl.BlockSpec((None, LANES, TM), lambda b, i: (b, 0, i)),
    ]
    return pl.pallas_call(
        _prep_kernel,
        grid=(bsz, nt),
        in_specs=[tok(D_MODEL),
                  pl.BlockSpec((None, 1, 3 * D_MODEL), lambda b, i: (2 * b + jnp.minimum(i, 1), 0, 0))]
                 + [full(w) for w in weights] + [tab, tab],
        out_specs=out_specs,
        out_shape=out_shape,
        compiler_params=_cparams(("parallel", "arbitrary")),
        name="prep",
    )(xs, mod, *weights, rope_m, rope_g)


def _gla_direction(q_ref, k_ref, v_ref, g_ref, l_ref, cmask_ref, qmask, omask, smask, s_ref, o_ref,
                   reverse):
    q = q_ref[...].astype(F32)
    k = k_ref[...].astype(F32)
    g = g_ref[...]
    g_hi = g.astype(BF16)
    r1 = g - g_hi.astype(F32)
    g_mid = r1.astype(BF16)
    g_lo = (r1 - g_mid.astype(F32)).astype(BF16)
    tri = l_ref[...]
    cum = _dot(tri, g_hi) + _dot(tri, g_mid) + _dot(tri, g_lo)
    cmask = cmask_ref[...]
    order = range(CHUNKS - 1, -1, -1) if reverse else range(CHUNKS)
    for c in order:
        sl = slice(c * GLA_CHUNK, (c + 1) * GLA_CHUNK)
        cum_c = cum[sl]
        last = cum_c[0:1] if reverse else cum_c[GLA_CHUNK - 1:GLA_CHUNK]
        qd = q[sl] * jnp.exp(cum_c)
        ki = k[sl] * jnp.exp(-cum_c)
        kd = k[sl] * jnp.exp(last - cum_c)
        vc = v_ref[sl, :]
        qh = jnp.concatenate([qd] * GLA_HEADS, axis=0) * qmask
        att = _dot_nt(qh.astype(BF16), ki.astype(BF16)) * cmask
        o_all = _dot(att.astype(BF16), vc) * omask
        o_intra = o_all[0:GLA_CHUNK]
        for hd in range(1, GLA_HEADS):
            o_intra = o_intra + o_all[hd * GLA_CHUNK:(hd + 1) * GLA_CHUNK]
        st = s_ref[...]
        o_inter = _dot_nt(qd.astype(BF16), st.astype(BF16))
        o_ref[sl, :] = o_inter + o_intra
        ds = _dot_tn(vc, kd.astype(BF16)) * smask
        s_ref[...] = st * jnp.exp(last) + ds


def _gla_kernel(qf_ref, kf_ref, vf_ref, gf_ref, qb_ref, kb_ref, vb_ref, gb_ref, lf_ref, lb_ref,
                cmf_ref, cmb_ref, qmask_ref, omask_ref, smask_ref, of_ref, ob_ref, sf_ref, sb_ref):
    @pl.when(pl.program_id(1) == 0)
    def _():
        sf_ref[...] = jnp.zeros_like(sf_ref)
        sb_ref[...] = jnp.zeros_like(sb_ref)

    qmask, omask, smask = qmask_ref[...], omask_ref[...], smask_ref[...]
    _gla_direction(qf_ref, kf_ref, vf_ref, gf_ref, lf_ref, cmf_ref, qmask, omask, smask, sf_ref,
                   of_ref, False)
    _gla_direction(qb_ref, kb_ref, vb_ref, gb_ref, lb_ref, cmb_ref, qmask, omask, smask, sb_ref,
                   ob_ref, True)


def _gla_consts():
    r = np.arange(TM)
    same = (r[:, None] // GLA_CHUNK) == (r[None, :] // GLA_CHUNK)
    lf = (same & (r[None, :] <= r[:, None])).astype(np.float32)
    lb = (same & (r[None, :] >= r[:, None])).astype(np.float32)
    rows = np.arange(GLA_HEADS * GLA_CHUNK)
    j = np.arange(GLA_CHUNK)
    cmf = ((rows[:, None] % GLA_CHUNK) >= j[None, :]).astype(np.float32)
    cmb = ((rows[:, None] % GLA_CHUNK) <= j[None, :]).astype(np.float32)
    qmask = ((rows[:, None] // GLA_CHUNK) == (np.arange(GLA_QK)[None, :] // GLA_DK)).astype(np.float32)
    omask = ((rows[:, None] // GLA_CHUNK) == (np.arange(GLA_WIDTH)[None, :] // GLA_DV)).astype(np.float32)
    smask = ((np.arange(GLA_WIDTH)[:, None] // GLA_DV) == (np.arange(GLA_QK)[None, :] // GLA_DK)).astype(np.float32)
    return (jnp.asarray(lf, BF16), jnp.asarray(lb, BF16), jnp.asarray(cmf), jnp.asarray(cmb),
            jnp.asarray(qmask), jnp.asarray(omask), jnp.asarray(smask))


def _gla(gq, gk, gv, gg):
    bsz, t, _ = gq.shape
    nt = t // TM
    consts = _gla_consts()
    fwd = lambda b, i: (b, i, 0)
    rev_blk = lambda i: jnp.where(i == 0, 0, nt - i)
    bwd = lambda b, i: (b, rev_blk(i), 0)
    bwd_g = lambda b, i: (b, rev_blk(i), 1)
    spec = lambda w, im: pl.BlockSpec((None, TM, w), im)
    full = lambda a: pl.BlockSpec(a.shape, lambda b, i: (0,) * a.ndim)
    return pl.pallas_call(
        _gla_kernel,
        grid=(bsz, nt),
        in_specs=[spec(GLA_QK, fwd), spec(GLA_QK, fwd), spec(GLA_WIDTH, fwd), spec(GLA_QK, fwd),
                  spec(GLA_QK, bwd), spec(GLA_QK, bwd), spec(GLA_WIDTH, bwd), spec(GLA_QK, bwd_g)]
                 + [full(a) for a in consts],
        out_specs=[spec(GLA_WIDTH, fwd), spec(GLA_WIDTH, bwd)],
        out_shape=[jax.ShapeDtypeStruct((bsz, t, GLA_WIDTH), F32)] * 2,
        scratch_shapes=[pltpu.VMEM((GLA_WIDTH, GLA_QK), F32)] * 2,
        compiler_params=_cparams(("parallel", "arbitrary")),
        name="gla",
    )(gq, gk, gv, gg, gq, gk, gv, gg, *consts)


def _head_values(vt_ref, h, cols, shared_kv):
    if shared_kv:
        r0 = (h // GQA_GROUP) * HEAD_DIM
        return vt_ref[r0:r0 + HEAD_DIM, cols]
    r0 = (h % 2) * MLA_V
    return vt_ref[h // 2, r0:r0 + MLA_V, cols]


def _write_heads(o_ref, outs_t):
    for p in range(len(outs_t) // 2):
        pair = jnp.concatenate([outs_t[2 * p], outs_t[2 * p + 1]], axis=0)
        o_ref[:, p * LANES:(p + 1) * LANES] = pair.T.astype(o_ref.dtype)


def _attn_ctx_kernel(q_ref, k_ref, vt_ref, o_ref, *, shared_kv):
    heads = q_ref.shape[1] // LANES
    outs_t = []
    for h in range(heads):
        kh = k_ref[...] if shared_kv else k_ref[:, h * LANES:(h + 1) * LANES]
        s = _dot_nt(kh, q_ref[:, h * LANES:(h + 1) * LANES])
        p = jnp.exp2(s - jnp.max(s, axis=0, keepdims=True))
        l = jnp.sum(p, axis=0, keepdims=True)
        outs_t.append(_dot(_head_values(vt_ref, h, slice(None), shared_kv), p.astype(BF16)) / l)
    _write_heads(o_ref, outs_t)


def _attn_kernel(z_ref, qa_ref, qb_ref, k_ref, vt_ref, o_ref, s_ref, p_ref, *, shared_kv):
    t = k_ref.shape[0]
    heads = qa_ref.shape[1] // LANES
    chunks = [slice(a, a + ATT_CK) for a in range(0, t, ATT_CK)]
    z = z_ref[0]

    def dyn(start, size):
        return pl.ds(pl.multiple_of(z + start, ATT_SUB), size)

    def q_of(h):
        sl = slice(h * LANES, (h + 1) * LANES)
        qh = jnp.concatenate([qa_ref[:, sl], qb_ref[:, sl]], axis=0)
        return qh.astype(F32).T.astype(BF16)

    def fold8(v):
        return v.reshape(v.shape[0] // 8, 8, v.shape[1])

    def qk(h, qh, rows):
        kh = k_ref[rows, :] if shared_kv else k_ref[rows, h * LANES:(h + 1) * LANES]
        return _dot(kh, qh)

    def put_scores(s, rows, m8):
        s_ref[dyn(rows.start, ATT_CK), :] = s
        smax = jnp.max(fold8(s), axis=0)
        return smax if m8 is None else jnp.maximum(m8, smax)

    ones = jnp.ones((16, ATT_CK), BF16)
    m8 = None
    qh = q_of(0)
    for rows in chunks:
        m8 = put_scores(qk(0, qh, rows), rows, m8)
    outs_t = []
    for h in range(heads):
        mb = jnp.broadcast_to(jnp.max(m8, axis=0, keepdims=True), (ATT_SUB, m8.shape[1]))
        m8, acc = None, None
        more = h + 1 < heads
        if more:
            qh = q_of(h + 1)
            s_next = qk(h + 1, qh, chunks[0])
        for c, rows in enumerate(chunks):
            if more:
                s_cur = s_next
                if c + 1 < len(chunks):
                    s_next = qk(h + 1, qh, chunks[c + 1])
            vt1 = jnp.concatenate([_head_values(vt_ref, h, rows, shared_kv), ones], axis=0)
            p0 = (c % 2) * ATT_CK
            for r in range(0, ATT_CK, ATT_SUB):
                p = jnp.exp2(s_ref[dyn(rows.start + r, ATT_SUB), :] - mb)
                p_ref[p0 + r:p0 + r + ATT_SUB, :] = p.astype(BF16)
            if more:
                m8 = put_scores(s_cur, rows, m8)
            pv = _dot(vt1, p_ref[p0:p0 + ATT_CK, :])
            acc = pv if acc is None else acc + pv
        outs_t.append(acc[0:HEAD_DIM] / acc[HEAD_DIM:HEAD_DIM + 1])
    _write_heads(o_ref, outs_t)


def _attention(q, k, vt, shared_kv):
    bsz, t, width = q.shape
    n2 = (t - CTX_LEN) // (2 * TM)
    kw = k.shape[2]
    name = "attn_gqa" if shared_kv else "attn_mla"
    if shared_kv:
        v_spec = pl.BlockSpec((None, LANES, t), lambda b, j: (b, 0, 0))
        vc_spec = pl.BlockSpec((None, LANES, CTX_LEN), lambda b: (b, 0, 0))
    else:
        v_spec = pl.BlockSpec((None, vt.shape[1], LANES, t), lambda b, j: (b, 0, 0, 0))
        vc_spec = pl.BlockSpec((None, vt.shape[1], LANES, CTX_LEN), lambda b: (b, 0, 0, 0))
    y_lat = pl.pallas_call(
        functools.partial(_attn_kernel, shared_kv=shared_kv),
        grid=(bsz, n2),
        in_specs=[pl.BlockSpec(memory_space=pltpu.SMEM),
                  pl.BlockSpec((None, TM, width), lambda b, j: (b, 2 * j + 1, 0)),
                  pl.BlockSpec((None, TM, width), lambda b, j: (b, 2 * j + 2, 0)),
                  pl.BlockSpec((None, t, kw), lambda b, j: (b, 0, 0)), v_spec],
        out_specs=pl.BlockSpec((None, 2 * TM, width // 2), lambda b, j: (b, j, 0)),
        out_shape=jax.ShapeDtypeStruct((bsz, t - CTX_LEN, width // 2), BF16),
        scratch_shapes=[pltpu.VMEM((t, 2 * TM), F32), pltpu.VMEM((2 * ATT_CK, 2 * TM), BF16)],
        compiler_params=_cparams(("parallel", "arbitrary")),
        name=name,
    )(jnp.zeros((1,), jnp.int32), q, q, k, vt)
    y_ctx = pl.pallas_call(
        functools.partial(_attn_ctx_kernel, shared_kv=shared_kv),
        grid=(bsz,),
        in_specs=[pl.BlockSpec((None, CTX_LEN, width), lambda b: (b, 0, 0)),
                  pl.BlockSpec((None, CTX_LEN, kw), lambda b: (b, 0, 0)), vc_spec],
        out_specs=pl.BlockSpec((None, CTX_LEN, width // 2), lambda b: (b, 0, 0)),
        out_shape=jax.ShapeDtypeStruct((bsz, CTX_LEN, width // 2), BF16),
        compiler_params=_cparams(("parallel",)),
        name=name + "_ctx",
    )(q, k, vt)
    return y_lat, y_ctx


def _outproj_kernel(x_ref, of_ref, ob_ref, yb_ref, ybc_ref, yc_ref, ycc_ref, zg_ref, gnw_ref,
                    mod_ref, wout_ref, fnw_ref, o_ref, *, final):
    lo = lax.broadcasted_iota(jnp.int32, (TM, LANES), 1) < GLA_DV
    o = of_ref[...] + ob_ref[...]
    gnw = gnw_ref[...]
    zg = zg_ref[...]
    parts = []
    for j in range(GLA_WIDTH // LANES):
        sl = slice(j * LANES, (j + 1) * LANES)
        oj = o[:, sl]
        o2 = oj * oj
        ms_lo = jnp.sum(jnp.where(lo, o2, 0.0), axis=-1, keepdims=True) * (1.0 / GLA_DV)
        ms_hi = jnp.sum(jnp.where(lo, 0.0, o2), axis=-1, keepdims=True) * (1.0 / GLA_DV)
        inv = jnp.where(lo, lax.rsqrt(ms_lo + EPS), lax.rsqrt(ms_hi + EPS))
        parts.append(oj * inv * gnw[:, sl] * zg[:, sl])
    if final:
        yb, yc = yb_ref[...], yc_ref[...]
    else:
        is_ctx = pl.program_id(1) == 0
        yb = jnp.where(is_ctx, ybc_ref[...], yb_ref[...])
        yc = jnp.where(is_ctx, ycc_ref[...], yc_ref[...])
    parts.append(yb * zg[:, GLA_WIDTH:GLA_WIDTH + MLA_WIDTH])
    parts.append(yc * zg[:, GLA_WIDTH + MLA_WIDTH:])
    y = jnp.concatenate(parts, axis=-1).astype(BF16)
    gate = mod_ref[...]
    xn = x_ref[...] + gate * _dot(y, wout_ref[...])
    if final:
        xn = xn * lax.rsqrt(jnp.mean(xn * xn, axis=-1, keepdims=True) + EPS) * fnw_ref[...]
    o_ref[...] = xn


def _outproj(xs, o_f, o_b, y_b, y_c, zg, mod, lw, final_norm_w, final):
    bsz, t, _ = xs.shape
    off = 1 if final else 0
    nt = t // TM - off
    tok = lambda w: pl.BlockSpec((None, TM, w), lambda b, i: (b, i + off, 0))
    lat = lambda w: pl.BlockSpec((None, TM, w), lambda b, i: (b, jnp.maximum(i + off - 1, 0), 0))
    ctx = lambda w: pl.BlockSpec((None, CTX_LEN, w), lambda b, i: (b, 0, 0))
    full = lambda a: pl.BlockSpec(a.shape, lambda b, i: (0,) * a.ndim)
    return pl.pallas_call(
        functools.partial(_outproj_kernel, final=final),
        grid=(bsz, nt),
        in_specs=[tok(D_MODEL), tok(GLA_WIDTH), tok(GLA_WIDTH), lat(MLA_WIDTH), ctx(MLA_WIDTH),
                  lat(GQA_WIDTH), ctx(GQA_WIDTH), tok(D_MIX), full(lw["gla_norm"]),
                  pl.BlockSpec((None, 1, D_MODEL),
                               lambda b, i: (2 * b + jnp.minimum(i + off, 1), 0, 2)),
                  full(lw["w_out"]), full(final_norm_w)],
        out_specs=pl.BlockSpec((None, TM, D_MODEL), lambda b, i: (b, i, 0)),
        out_shape=jax.ShapeDtypeStruct((bsz, nt * TM, D_MODEL), F32),
        compiler_params=_cparams(("parallel", "arbitrary")),
        name="outproj",
    )(xs, o_f, o_b, *y_b, *y_c, zg, lw["gla_norm"], mod, lw["w_out"], final_norm_w)


def _rot_src(d_rot):
    quarter = d_rot // 4
    i = np.arange(d_rot)
    return np.where(i % (2 * quarter) < quarter, i + quarter, i - quarter)


def _rope_tables(rows, d_rot, lanes):
    quarter = d_rot // 4
    n = rows * GRID_W
    row = jnp.repeat(jnp.arange(rows), GRID_W).astype(F32)
    col = jnp.tile(jnp.arange(GRID_W), rows).astype(F32)
    freqs = ROPE_THETA ** (-jnp.arange(quarter, dtype=F32) / quarter)
    ang = jnp.stack([row[:, None] * freqs, col[:, None] * freqs], axis=1)
    cos, sin = jnp.cos(ang), jnp.sin(ang)
    cos_l = jnp.stack([cos, cos], axis=2).reshape(n, d_rot)
    sin_l = jnp.stack([-sin, sin], axis=2).reshape(n, d_rot)
    cos_t = jnp.ones((n, LANES), F32)
    sin_t = jnp.zeros((n, LANES), F32)
    for lane0 in lanes:
        cos_t = cos_t.at[:, lane0:lane0 + d_rot].set(cos_l)
        sin_t = sin_t.at[:, lane0:lane0 + d_rot].set(sin_l)
    ident = jnp.concatenate([jnp.ones((CTX_LEN, LANES), F32), jnp.zeros((CTX_LEN, LANES), F32)], 1)
    return jnp.concatenate([ident, jnp.concatenate([cos_t, sin_t], axis=1)], axis=0)


def _take_cols(w, idx):
    w_ext = jnp.concatenate([w, jnp.zeros((w.shape[0], 1), w.dtype)], axis=1)
    return w_ext[:, np.asarray(idx)]


def _w_in_columns():
    c0 = 2 * GLA_QK + 2 * GLA_WIDTH
    m0 = c0 + 2 * GLA_GATE_RANK
    kr0 = m0 + MLA_Q_LORA + MLA_KV_LORA
    g0 = kr0 + MLA_ROPE + MLA_WIDTH
    gk0 = g0 + GQA_WIDTH
    gv0 = gk0 + GQA_KV_HEADS * HEAD_DIM
    gz0 = gv0 + GQA_KV_HEADS * HEAD_DIM
    pad = lambda n: [-1] * n
    seg_a = list(range(0, 2 * GLA_QK + GLA_WIDTH)) + list(range(c0, m0)) + pad(LANES - 2 * GLA_GATE_RANK)
    rs_m, rs_g = _rot_src(MLA_ROPE), _rot_src(HEAD_DIM)
    tail = pad(LANES - MLA_NOPE - MLA_ROPE)
    seg_b = (list(range(m0, kr0)) + pad(MLA_NOPE) + list(range(kr0, kr0 + MLA_ROPE)) + tail
             + pad(MLA_NOPE) + [kr0 + int(j) for j in rs_m] + tail)
    q_pad, q_rot = [], []
    for h in range(GQA_HEADS):
        base = g0 + h * HEAD_DIM
        own = list(range(base, base + HEAD_DIM))
        rot = [base + int(j) for j in rs_g]
        half = h // GQA_GROUP
        q_pad += own + pad(HEAD_DIM) if half == 0 else pad(HEAD_DIM) + own
        q_rot += rot + pad(HEAD_DIM) if half == 0 else pad(HEAD_DIM) + rot
    k_nat = list(range(gk0, gv0))
    k_rot = [gk0 + kv * HEAD_DIM + int(j) for kv in range(GQA_KV_HEADS) for j in rs_g]
    seg_c = q_pad + q_rot + k_nat + k_rot + list(range(gv0, gz0))
    seg_z = (list(range(2 * GLA_QK + GLA_WIDTH, c0)) + list(range(kr0 + MLA_ROPE, g0))
             + list(range(gz0, gz0 + GQA_WIDTH)))
    cols = seg_a + seg_b + seg_c + seg_z
    assert len(seg_a) == SEG_A and len(seg_b) == SEG_B and len(seg_c) == SEG_C and len(cols) == W_IN_PAD
    return cols


def _layer_weights(l, norm_w, w_in, gla_w_a_fwd, gla_b_a_fwd, gla_w_a_bwd, gla_b_a_bwd, gla_norm_w,
                   mla_q_norm_w, mla_w_uq, mla_kv_norm_w, mla_w_ukv, gqa_q_norm_w, gqa_k_norm_w,
                   w_out):
    w_pad = _take_cols(w_in[l], _w_in_columns()).astype(BF16)

    w_a = jnp.zeros((LANES, 2 * GLA_QK), F32)
    w_a = w_a.at[0:GLA_GATE_RANK, 0:GLA_QK].set(gla_w_a_fwd[l])
    w_a = w_a.at[GLA_GATE_RANK:2 * GLA_GATE_RANK, GLA_QK:].set(gla_w_a_bwd[l])
    b_a = jnp.concatenate([gla_b_a_fwd[l], gla_b_a_bwd[l]])[None, :]

    dq = MLA_NOPE + MLA_ROPE
    rs_m, rs_g = _rot_src(MLA_ROPE), _rot_src(HEAD_DIM)
    uq_cols, uq_rot_cols = [], []
    for h in range(MLA_HEADS):
        uq_cols += list(range(h * dq, (h + 1) * dq)) + [-1] * (LANES - dq)
        uq_rot_cols += ([-1] * MLA_NOPE + [h * dq + MLA_NOPE + int(j) for j in rs_m]
                        + [-1] * (LANES - dq))
    ukv = mla_w_ukv[l].reshape(MLA_KV_LORA, MLA_HEADS, MLA_NOPE + MLA_V)
    uk = jnp.pad(ukv[:, :, :MLA_NOPE], ((0, 0), (0, 0), (0, LANES - MLA_NOPE)))
    uk = uk.reshape(MLA_KV_LORA, MLA_HEADS * LANES)
    uv = ukv[:, :, MLA_NOPE:].reshape(MLA_KV_LORA, MLA_WIDTH)

    both = lambda v: jnp.tile(v, LANES // HEAD_DIM)[None, :]
    return {
        "norm_w": norm_w[l][None, :],
        "w_in": w_pad,
        "w_a": w_a.astype(BF16),
        "b_a": b_a,
        "mla_q_norm": mla_q_norm_w[l][None, :],
        "w_uq": _take_cols(mla_w_uq[l], uq_cols).astype(BF16),
        "w_uq_rot": _take_cols(mla_w_uq[l], uq_rot_cols).astype(BF16),
        "mla_kv_norm": mla_kv_norm_w[l][None, :],
        "w_uk": uk.astype(BF16),
        "w_uv": uv.astype(BF16),
        "gqa_q_norm": both(gqa_q_norm_w[l]),
        "gqa_q_norm_rot": both(gqa_q_norm_w[l][rs_g]),
        "gqa_k_norm": both(gqa_k_norm_w[l]),
        "gqa_k_norm_rot": both(gqa_k_norm_w[l][rs_g]),
        "gla_norm": jnp.tile(gla_norm_w[l], GLA_HEADS)[None, :],
        "w_out": w_out[l].astype(BF16),
    }


def kernel(x, c, ctx, c_ctx, norm_w, w_mod, b_mod, w_in, gla_w_a_fwd, gla_b_a_fwd, gla_w_a_bwd,
           gla_b_a_bwd, gla_norm_w, mla_q_norm_w, mla_w_uq, mla_kv_norm_w, mla_w_ukv,
           gqa_q_norm_w, gqa_k_norm_w, w_out, final_norm_w):
    bsz, n, _ = x.shape
    depth = w_in.shape[0]
    assert ctx.shape[1] == CTX_LEN and n % (2 * TM) == 0 and n % GRID_W == 0
    rows = n // GRID_W
    rope_m = _rope_tables(rows, MLA_ROPE, (MLA_NOPE,))
    rope_g = _rope_tables(rows, HEAD_DIM, (0, HEAD_DIM))

    mod_rows = -(-(bsz + 1) // 8) * 8
    cs = jnp.zeros((mod_rows, D_MODEL), F32).at[:bsz].set(c).at[bsz].set(c_ctx)
    mod_all = _modulation(cs, w_mod, b_mod)
    fnw = final_norm_w[None, :]

    xs = jnp.concatenate([ctx, x], axis=1)
    for l in range(depth):
        lw = _layer_weights(l, norm_w, w_in, gla_w_a_fwd, gla_b_a_fwd, gla_w_a_bwd, gla_b_a_bwd,
                            gla_norm_w, mla_q_norm_w, mla_w_uq, mla_kv_norm_w, mla_w_ukv,
                            gqa_q_norm_w, gqa_k_norm_w, w_out)
        m = mod_all[l]
        mod = jnp.stack([jnp.broadcast_to(m[bsz], (bsz, 3 * D_MODEL)), m[:bsz]], axis=1)
        mod = mod.reshape(2 * bsz, 1, 3 * D_MODEL)
        gq, gk, gv, gg, zg, mq, mk, mvt, cq, ck, cvt = _prep(xs, mod, lw, rope_m, rope_g)
        o_f, o_b = _gla(gq, gk, gv, gg)
        y_b = _attention(mq, mk, mvt, shared_kv=False)
        y_c = _attention(cq, ck, cvt, shared_kv=True)
        xs = _outproj(xs, o_f, o_b, y_b, y_c, zg, mod, lw, fnw, final=(l == depth - 1))
    return xs
```

```python
import functools

import numpy as np
import jax
import jax.numpy as jnp
from jax import lax
from jax.experimental import pallas as pl
from jax.experimental.pallas import tpu as pltpu

D_MODEL = 1024
GRID_W = 64
CTX_LEN = 256
HEAD_DIM = 64
ROPE_THETA = 10000.0
EPS = 1e-6

GLA_HEADS = 4
GLA_DK = 32
GLA_DV = 64
GLA_WIDTH = GLA_HEADS * GLA_DV
GLA_QK = GLA_HEADS * GLA_DK
GLA_GATE_RANK = 16
GLA_GATE_NORM = 16.0
GLA_CHUNK = 64

MLA_HEADS = 6
MLA_NOPE = 64
MLA_ROPE = 32
MLA_V = 64
MLA_Q_LORA = 256
MLA_KV_LORA = 256
MLA_WIDTH = MLA_HEADS * MLA_V

GQA_HEADS = 6
GQA_KV_HEADS = 2
GQA_GROUP = GQA_HEADS // GQA_KV_HEADS
GQA_WIDTH = GQA_HEADS * HEAD_DIM

D_MIX = GLA_WIDTH + MLA_WIDTH + GQA_WIDTH

LANES = 128
TM = CTX_LEN
CHUNKS = TM // GLA_CHUNK
SEG_A = 640
SEG_B = 768
SEG_C = 1920
W_IN_PAD = SEG_A + SEG_B + SEG_C + D_MIX
VMEM_LIMIT = 48 * 1024 * 1024
ATT_CK = 256
ATT_SUB = 256
LOG2E = 1.4426950408889634

F32 = jnp.float32
BF16 = jnp.bfloat16


def _cparams(sem):
    return pltpu.CompilerParams(dimension_semantics=sem, vmem_limit_bytes=VMEM_LIMIT)


def _dot(a, b):
    return jnp.dot(a, b, preferred_element_type=F32)


def _dot_nt(a, b):
    return lax.dot_general(a, b, (((1,), (1,)), ((), ())), preferred_element_type=F32)


def _dot_tn(a, b):
    return lax.dot_general(a, b, (((0,), (0,)), ((), ())), preferred_element_type=F32)


def _silu(v):
    return v / (1.0 + jnp.exp(-v))


def _mod_kernel(c_ref, w_ref, b_ref, o_ref):
    s = _silu(c_ref[...]).astype(BF16)
    o_ref[...] = _dot(s, w_ref[...].astype(BF16)) + b_ref[...]


def _modulation(cs, w_mod, b_mod):
    depth = w_mod.shape[0]
    rows = cs.shape[0]
    return pl.pallas_call(
        _mod_kernel,
        grid=(depth, 3),
        in_specs=[
            pl.BlockSpec((rows, D_MODEL), lambda l, j: (0, 0)),
            pl.BlockSpec((None, D_MODEL, D_MODEL), lambda l, j: (l, 0, j)),
            pl.BlockSpec((None, 1, D_MODEL), lambda l, j: (l, 0, j)),
        ],
        out_specs=pl.BlockSpec((None, rows, D_MODEL), lambda l, j: (l, 0, j)),
        out_shape=jax.ShapeDtypeStruct((depth, rows, 3 * D_MODEL), F32),
        compiler_params=_cparams(("arbitrary", "arbitrary")),
        name="modulation",
    )(cs, w_mod, b_mod.reshape(depth, 1, 3 * D_MODEL))


def _prep_kernel(x_ref, mod_ref, nw_ref, win_ref, wa_ref, ba_ref, mqn_ref, wuq_ref, wuqr_ref,
                 mkvn_ref, wuk_ref, wuv_ref, gqn_ref, gqnr_ref, gkn_ref, gknr_ref, rm_ref, rg_ref,
                 gq_ref, gk_ref, gv_ref, gg_ref, zg_ref, mq_ref, mk_ref, mvt_ref, cq_ref, ck_ref,
                 cvt_ref):
    x = x_ref[...]
    mod = mod_ref[...]
    shift, scale = mod[:, :D_MODEL], mod[:, D_MODEL:2 * D_MODEL]
    ms = jnp.mean(x * x, axis=-1, keepdims=True)
    h = (x * lax.rsqrt(ms + EPS) * nw_ref[...]) * (1.0 + scale) + shift
    hb = h.astype(BF16)

    uz = _dot(hb, win_ref[:, SEG_A + SEG_B + SEG_C:])
    zg_ref[...] = _silu(uz).astype(BF16)

    ua = _dot(hb, win_ref[:, 0:SEG_A])
    gq_ref[...] = (ua[:, 0:GLA_QK] * GLA_DK ** -0.5).astype(BF16)
    gk_ref[...] = ua[:, GLA_QK:2 * GLA_QK].astype(BF16)
    gv_ref[...] = ua[:, 2 * GLA_QK:2 * GLA_QK + GLA_WIDTH].astype(BF16)
    a = ua[:, 2 * GLA_QK + GLA_WIDTH:SEG_A].astype(BF16)
    xg = _dot(a, wa_ref[...]) + ba_ref[...]
    log_sig = jnp.minimum(xg, 0.0) - jnp.log1p(jnp.exp(-jnp.abs(xg)))
    gg_ref[...] = log_sig * (1.0 / GLA_GATE_NORM)

    ub = _dot(hb, win_ref[:, SEG_A:SEG_A + SEG_B])
    cq = ub[:, 0:MLA_Q_LORA]
    ckv = ub[:, MLA_Q_LORA:MLA_Q_LORA + MLA_KV_LORA]
    kr = ub[:, MLA_Q_LORA + MLA_KV_LORA:MLA_Q_LORA + MLA_KV_LORA + LANES]
    kr_rot = ub[:, MLA_Q_LORA + MLA_KV_LORA + LANES:SEG_B]
    cqn = cq * lax.rsqrt(jnp.mean(cq * cq, axis=-1, keepdims=True) + EPS) * mqn_ref[...]
    ckvn = ckv * lax.rsqrt(jnp.mean(ckv * ckv, axis=-1, keepdims=True) + EPS) * mkvn_ref[...]
    cqb = cqn.astype(BF16)
    qm = _dot(cqb, wuq_ref[...])
    qm_rot = _dot(cqb, wuqr_ref[...])
    ckvb = ckvn.astype(BF16)
    kn = _dot(ckvb, wuk_ref[...])
    vm = _dot(ckvb, wuv_ref[...])
    cos_m, sin_m = rm_ref[:, :LANES], rm_ref[:, LANES:]
    k_rope = kr * cos_m + kr_rot * sin_m
    mla_scale = (MLA_NOPE + MLA_ROPE) ** -0.5 * LOG2E
    for hd in range(MLA_HEADS):
        sl = slice(hd * LANES, (hd + 1) * LANES)
        mq_ref[:, sl] = ((qm[:, sl] * cos_m + qm_rot[:, sl] * sin_m) * mla_scale).astype(BF16)
        mk_ref[:, sl] = (kn[:, sl] + k_rope).astype(BF16)
    for p in range(MLA_HEADS // 2):
        mvt_ref[p] = vm[:, p * LANES:(p + 1) * LANES].T.astype(BF16)

    uc = _dot(hb, win_ref[:, SEG_A + SEG_B:SEG_A + SEG_B + SEG_C])
    cos_g, sin_g = rg_ref[:, :LANES], rg_ref[:, LANES:]
    qa, qb = cos_g * gqn_ref[...], sin_g * gqnr_ref[...]
    gqa_scale = HEAD_DIM ** -0.5 * LOG2E
    nq = GQA_HEADS * LANES
    for hd in range(GQA_HEADS):
        sl = slice(hd * LANES, (hd + 1) * LANES)
        xh = uc[:, sl]
        xr = uc[:, nq + hd * LANES:nq + (hd + 1) * LANES]
        msq = jnp.sum(xh * xh, axis=-1, keepdims=True) * (1.0 / HEAD_DIM)
        cq_ref[:, sl] = ((xh * qa + xr * qb) * (lax.rsqrt(msq + EPS) * gqa_scale)).astype(BF16)
    k = uc[:, 2 * nq:2 * nq + LANES]
    k_rot = uc[:, 2 * nq + LANES:2 * nq + 2 * LANES]
    lo = lax.broadcasted_iota(jnp.int32, (TM, LANES), 1) < HEAD_DIM
    k2 = k * k
    ms_lo = jnp.sum(jnp.where(lo, k2, 0.0), axis=-1, keepdims=True) * (1.0 / HEAD_DIM)
    ms_hi = jnp.sum(jnp.where(lo, 0.0, k2), axis=-1, keepdims=True) * (1.0 / HEAD_DIM)
    inv = jnp.where(lo, lax.rsqrt(ms_lo + EPS), lax.rsqrt(ms_hi + EPS))
    ck_ref[...] = ((k * (cos_g * gkn_ref[...]) + k_rot * (sin_g * gknr_ref[...])) * inv).astype(BF16)
    cvt_ref[...] = uc[:, 2 * nq + 2 * LANES:SEG_C].T.astype(BF16)


def _prep(xs, mod, lw, rope_m, rope_g):
    bsz, t, _ = xs.shape
    nt = t // TM
    tok = lambda w: pl.BlockSpec((None, TM, w), lambda b, i: (b, i, 0))
    full = lambda a: pl.BlockSpec(a.shape, lambda b, i: (0,) * a.ndim)
    tab = pl.BlockSpec((TM, 2 * LANES), lambda b, i: (i, 0))
    weights = [lw[n] for n in ("norm_w", "w_in", "w_a", "b_a", "mla_q_norm", "w_uq", "w_uq_rot",
                               "mla_kv_norm", "w_uk", "w_uv", "gqa_q_norm", "gqa_q_norm_rot",
                               "gqa_k_norm", "gqa_k_norm_rot")]
    out_shape = [
        jax.ShapeDtypeStruct((bsz, t, GLA_QK), BF16),
        jax.ShapeDtypeStruct((bsz, t, GLA_QK), BF16),
        jax.ShapeDtypeStruct((bsz, t, GLA_WIDTH), BF16),
        jax.ShapeDtypeStruct((bsz, t, 2 * GLA_QK), F32),
        jax.ShapeDtypeStruct((bsz, t, D_MIX), BF16),
        jax.ShapeDtypeStruct((bsz, t, MLA_HEADS * LANES), BF16),
        jax.ShapeDtypeStruct((bsz, t, MLA_HEADS * LANES), BF16),
        jax.ShapeDtypeStruct((bsz, MLA_HEADS // 2, LANES, t), BF16),
        jax.ShapeDtypeStruct((bsz, t, GQA_HEADS * LANES), BF16),
        jax.ShapeDtypeStruct((bsz, t, LANES), BF16),
        jax.ShapeDtypeStruct((bsz, LANES, t), BF16),
    ]
    out_specs = [
        tok(GLA_QK), tok(GLA_QK), tok(GLA_WIDTH), tok(2 * GLA_QK), tok(D_MIX),
        tok(MLA_HEADS * LANES), tok(MLA_HEADS * LANES),
        pl.BlockSpec((None, MLA_HEADS // 2, LANES, TM), lambda b, i: (b, 0, 0, i)),
        tok(GQA_HEADS * LANES), tok(LANES),
        pl.BlockSpec((None, LANES, TM), lambda b, i: (b, 0, i)),
    ]
    return pl.pallas_call(
        _prep_kernel,
        grid=(bsz, nt),
        in_specs=[tok(D_MODEL),
                  pl.BlockSpec((None, 1, 3 * D_MODEL), lambda b, i: (2 * b + jnp.minimum(i, 1), 0, 0))]
                 + [full(w) for w in weights] + [tab, tab],
        out_specs=out_specs,
        out_shape=out_shape,
        compiler_params=_cparams(("parallel", "arbitrary")),
        name="prep",
    )(xs, mod, *weights, rope_m, rope_g)


def _gla_direction(q_ref, k_ref, v_ref, g_ref, l_ref, cmask_ref, qmask, omask, smask, s_ref, o_ref,
                   reverse):
    q = q_ref[...].astype(F32)
    k = k_ref[...].astype(F32)
    g = g_ref[...]
    g_hi = g.astype(BF16)
    r1 = g - g_hi.astype(F32)
    g_mid = r1.astype(BF16)
    g_lo = (r1 - g_mid.astype(F32)).astype(BF16)
    tri = l_ref[...]
    cum = _dot(tri, g_hi) + _dot(tri, g_mid) + _dot(tri, g_lo)
    cmask = cmask_ref[...]
    order = range(CHUNKS - 1, -1, -1) if reverse else range(CHUNKS)
    for c in order:
        sl = slice(c * GLA_CHUNK, (c + 1) * GLA_CHUNK)
        cum_c = cum[sl]
        last = cum_c[0:1] if reverse else cum_c[GLA_CHUNK - 1:GLA_CHUNK]
        qd = q[sl] * jnp.exp(cum_c)
        ki = k[sl] * jnp.exp(-cum_c)
        kd = k[sl] * jnp.exp(last - cum_c)
        vc = v_ref[sl, :]
        qh = jnp.concatenate([qd] * GLA_HEADS, axis=0) * qmask
        att = _dot_nt(qh.astype(BF16), ki.astype(BF16)) * cmask
        o_all = _dot(att.astype(BF16), vc) * omask
        o_intra = o_all[0:GLA_CHUNK]
        for hd in range(1, GLA_HEADS):
            o_intra = o_intra + o_all[hd * GLA_CHUNK:(hd + 1) * GLA_CHUNK]
        st = s_ref[...]
        o_inter = _dot_nt(qd.astype(BF16), st.astype(BF16))
        o_ref[sl, :] = o_inter + o_intra
        ds = _dot_tn(vc, kd.astype(BF16)) * smask
        s_ref[...] = st * jnp.exp(last) + ds


def _gla_kernel(qf_ref, kf_ref, vf_ref, gf_ref, qb_ref, kb_ref, vb_ref, gb_ref, lf_ref, lb_ref,
                cmf_ref, cmb_ref, qmask_ref, omask_ref, smask_ref, of_ref, ob_ref, sf_ref, sb_ref):
    @pl.when(pl.program_id(1) == 0)
    def _():
        sf_ref[...] = jnp.zeros_like(sf_ref)
        sb_ref[...] = jnp.zeros_like(sb_ref)

    qmask, omask, smask = qmask_ref[...], omask_ref[...], smask_ref[...]
    _gla_direction(qf_ref, kf_ref, vf_ref, gf_ref, lf_ref, cmf_ref, qmask, omask, smask, sf_ref,
                   of_ref, False)
    _gla_direction(qb_ref, kb_ref, vb_ref, gb_ref, lb_ref, cmb_ref, qmask, omask, smask, sb_ref,
                   ob_ref, True)


def _gla_consts():
    r = np.arange(TM)
    same = (r[:, None] // GLA_CHUNK) == (r[None, :] // GLA_CHUNK)
    lf = (same & (r[None, :] <= r[:, None])).astype(np.float32)
    lb = (same & (r[None, :] >= r[:, None])).astype(np.float32)
    rows = np.arange(GLA_HEADS * GLA_CHUNK)
    j = np.arange(GLA_CHUNK)
    cmf = ((rows[:, None] % GLA_CHUNK) >= j[None, :]).astype(np.float32)
    cmb = ((rows[:, None] % GLA_CHUNK) <= j[None, :]).astype(np.float32)
    qmask = ((rows[:, None] // GLA_CHUNK) == (np.arange(GLA_QK)[None, :] // GLA_DK)).astype(np.float32)
    omask = ((rows[:, None] // GLA_CHUNK) == (np.arange(GLA_WIDTH)[None, :] // GLA_DV)).astype(np.float32)
    smask = ((np.arange(GLA_WIDTH)[:, None] // GLA_DV) == (np.arange(GLA_QK)[None, :] // GLA_DK)).astype(np.float32)
    return (jnp.asarray(lf, BF16), jnp.asarray(lb, BF16), jnp.asarray(cmf), jnp.asarray(cmb),
            jnp.asarray(qmask), jnp.asarray(omask), jnp.asarray(smask))


def _gla(gq, gk, gv, gg):
    bsz, t, _ = gq.shape
    nt = t // TM
    consts = _gla_consts()
    fwd = lambda b, i: (b, i, 0)
    rev_blk = lambda i: jnp.where(i == 0, 0, nt - i)
    bwd = lambda b, i: (b, rev_blk(i), 0)
    bwd_g = lambda b, i: (b, rev_blk(i), 1)
    spec = lambda w, im: pl.BlockSpec((None, TM, w), im)
    full = lambda a: pl.BlockSpec(a.shape, lambda b, i: (0,) * a.ndim)
    return pl.pallas_call(
        _gla_kernel,
        grid=(bsz, nt),
        in_specs=[spec(GLA_QK, fwd), spec(GLA_QK, fwd), spec(GLA_WIDTH, fwd), spec(GLA_QK, fwd),
                  spec(GLA_QK, bwd), spec(GLA_QK, bwd), spec(GLA_WIDTH, bwd), spec(GLA_QK, bwd_g)]
                 + [full(a) for a in consts],
        out_specs=[spec(GLA_WIDTH, fwd), spec(GLA_WIDTH, bwd)],
        out_shape=[jax.ShapeDtypeStruct((bsz, t, GLA_WIDTH), F32)] * 2,
        scratch_shapes=[pltpu.VMEM((GLA_WIDTH, GLA_QK), F32)] * 2,
        compiler_params=_cparams(("parallel", "arbitrary")),
        name="gla",
    )(gq, gk, gv, gg, gq, gk, gv, gg, *consts)


def _head_values(vt_ref, h, cols, shared_kv):
    if shared_kv:
        r0 = (h // GQA_GROUP) * HEAD_DIM
        return vt_ref[r0:r0 + HEAD_DIM, cols]
    r0 = (h % 2) * MLA_V
    return vt_ref[h // 2, r0:r0 + MLA_V, cols]


def _write_heads(o_ref, outs_t):
    for p in range(len(outs_t) // 2):
        pair = jnp.concatenate([outs_t[2 * p], outs_t[2 * p + 1]], axis=0)
        o_ref[:, p * LANES:(p + 1) * LANES] = pair.T.astype(o_ref.dtype)


def _attn_ctx_kernel(q_ref, k_ref, vt_ref, o_ref, *, shared_kv):
    heads = q_ref.shape[1] // LANES
    outs_t = []
    for h in range(heads):
        kh = k_ref[...] if shared_kv else k_ref[:, h * LANES:(h + 1) * LANES]
        s = _dot_nt(kh, q_ref[:, h * LANES:(h + 1) * LANES])
        p = jnp.exp2(s - jnp.max(s, axis=0, keepdims=True))
        l = jnp.sum(p, axis=0, keepdims=True)
        outs_t.append(_dot(_head_values(vt_ref, h, slice(None), shared_kv), p.astype(BF16)) / l)
    _write_heads(o_ref, outs_t)


def _attn_kernel(z_ref, qa_ref, qb_ref, k_ref, vt_ref, o_ref, s_ref, p_ref, *, shared_kv):
    t = k_ref.shape[0]
    heads = qa_ref.shape[1] // LANES
    chunks = [slice(a, a + ATT_CK) for a in range(0, t, ATT_CK)]
    z = z_ref[0]

    def dyn(start, size):
        return pl.ds(pl.multiple_of(z + start, ATT_SUB), size)

    def q_of(h):
        sl = slice(h * LANES, (h + 1) * LANES)
        qh = jnp.concatenate([qa_ref[:, sl], qb_ref[:, sl]], axis=0)
        return qh.astype(F32).T.astype(BF16)

    def fold8(v):
        return v.reshape(v.shape[0] // 8, 8, v.shape[1])

    def qk(h, qh, rows):
        kh = k_ref[rows, :] if shared_kv else k_ref[rows, h * LANES:(h + 1) * LANES]
        return _dot(kh, qh)

    def put_scores(s, rows, m8):
        s_ref[dyn(rows.start, ATT_CK), :] = s
        smax = jnp.max(fold8(s), axis=0)
        return smax if m8 is None else jnp.maximum(m8, smax)

    ones = jnp.ones((16, ATT_CK), BF16)
    m8 = None
    qh = q_of(0)
    for rows in chunks:
        m8 = put_scores(qk(0, qh, rows), rows, m8)
    outs_t = []
    for h in range(heads):
        mb = jnp.broadcast_to(jnp.max(m8, axis=0, keepdims=True), (ATT_SUB, m8.shape[1]))
        m8, acc = None, None
        more = h + 1 < heads
        if more:
            qh = q_of(h + 1)
            s_next = qk(h + 1, qh, chunks[0])
        for c, rows in enumerate(chunks):
            if more:
                s_cur = s_next
                if c + 1 < len(chunks):
                    s_next = qk(h + 1, qh, chunks[c + 1])
            vt1 = jnp.concatenate([_head_values(vt_ref, h, rows, shared_kv), ones], axis=0)
            p0 = (c % 2) * ATT_CK
            for r in range(0, ATT_CK, ATT_SUB):
                p = jnp.exp2(s_ref[dyn(rows.start + r, ATT_SUB), :] - mb)
                p_ref[p0 + r:p0 + r + ATT_SUB, :] = p.astype(BF16)
            if more:
                m8 = put_scores(s_cur, rows, m8)
            pv = _dot(vt1, p_ref[p0:p0 + ATT_CK, :])
            acc = pv if acc is None else acc + pv
        outs_t.append(acc[0:HEAD_DIM] / acc[HEAD_DIM:HEAD_DIM + 1])
    _write_heads(o_ref, outs_t)


def _attention(q, k, vt, shared_kv):
    bsz, t, width = q.shape
    n2 = (t - CTX_LEN) // (2 * TM)
    kw = k.shape[2]
    name = "attn_gqa" if shared_kv else "attn_mla"
    if shared_kv:
        v_spec = pl.BlockSpec((None, LANES, t), lambda b, j: (b, 0, 0))
        vc_spec = pl.BlockSpec((None, LANES, CTX_LEN), lambda b: (b, 0, 0))
    else:
        v_spec = pl.BlockSpec((None, vt.shape[1], LANES, t), lambda b, j: (b, 0, 0, 0))
        vc_spec = pl.BlockSpec((None, vt.shape[1], LANES, CTX_LEN), lambda b: (b, 0, 0, 0))
    y_lat = pl.pallas_call(
        functools.partial(_attn_kernel, shared_kv=shared_kv),
        grid=(bsz, n2),
        in_specs=[pl.BlockSpec(memory_space=pltpu.SMEM),
                  pl.BlockSpec((None, TM, width), lambda b, j: (b, 2 * j + 1, 0)),
                  pl.BlockSpec((None, TM, width), lambda b, j: (b, 2 * j + 2, 0)),
                  pl.BlockSpec((None, t, kw), lambda b, j: (b, 0, 0)), v_spec],
        out_specs=pl.BlockSpec((None, 2 * TM, width // 2), lambda b, j: (b, j, 0)),
        out_shape=jax.ShapeDtypeStruct((bsz, t - CTX_LEN, width // 2), BF16),
        scratch_shapes=[pltpu.VMEM((t, 2 * TM), F32), pltpu.VMEM((2 * ATT_CK, 2 * TM), BF16)],
        compiler_params=_cparams(("parallel", "arbitrary")),
        name=name,
    )(jnp.zeros((1,), jnp.int32), q, q, k, vt)
    y_ctx = pl.pallas_call(
        functools.partial(_attn_ctx_kernel, shared_kv=shared_kv),
        grid=(bsz,),
        in_specs=[pl.BlockSpec((None, CTX_LEN, width), lambda b: (b, 0, 0)),
                  pl.BlockSpec((None, CTX_LEN, kw), lambda b: (b, 0, 0)), vc_spec],
        out_specs=pl.BlockSpec((None, CTX_LEN, width // 2), lambda b: (b, 0, 0)),
        out_shape=jax.ShapeDtypeStruct((bsz, CTX_LEN, width // 2), BF16),
        compiler_params=_cparams(("parallel",)),
        name=name + "_ctx",
    )(q, k, vt)
    return y_lat, y_ctx


def _outproj_kernel(x_ref, of_ref, ob_ref, yb_ref, ybc_ref, yc_ref, ycc_ref, zg_ref, gnw_ref,
                    mod_ref, wout_ref, fnw_ref, o_ref, *, final):
    lo = lax.broadcasted_iota(jnp.int32, (TM, LANES), 1) < GLA_DV
    o = of_ref[...] + ob_ref[...]
    gnw = gnw_ref[...]
    zg = zg_ref[...]
    parts = []
    for j in range(GLA_WIDTH // LANES):
        sl = slice(j * LANES, (j + 1) * LANES)
        oj = o[:, sl]
        o2 = oj * oj
        ms_lo = jnp.sum(jnp.where(lo, o2, 0.0), axis=-1, keepdims=True) * (1.0 / GLA_DV)
        ms_hi = jnp.sum(jnp.where(lo, 0.0, o2), axis=-1, keepdims=True) * (1.0 / GLA_DV)
        inv = jnp.where(lo, lax.rsqrt(ms_lo + EPS), lax.rsqrt(ms_hi + EPS))
        parts.append(oj * inv * gnw[:, sl] * zg[:, sl])
    if final:
        yb, yc = yb_ref[...], yc_ref[...]
    else:
        is_ctx = pl.program_id(1) == 0
        yb = jnp.where(is_ctx, ybc_ref[...], yb_ref[...])
        yc = jnp.where(is_ctx, ycc_ref[...], yc_ref[...])
    parts.append(yb * zg[:, GLA_WIDTH:GLA_WIDTH + MLA_WIDTH])
    parts.append(yc * zg[:, GLA_WIDTH + MLA_WIDTH:])
    y = jnp.concatenate(parts, axis=-1).astype(BF16)
    gate = mod_ref[...]
    xn = x_ref[...] + gate * _dot(y, wout_ref[...])
    if final:
        xn = xn * lax.rsqrt(jnp.mean(xn * xn, axis=-1, keepdims=True) + EPS) * fnw_ref[...]
    o_ref[...] = xn


def _outproj(xs, o_f, o_b, y_b, y_c, zg, mod, lw, final_norm_w, final):
    bsz, t, _ = xs.shape
    off = 1 if final else 0
    nt = t // TM - off
    tok = lambda w: pl.BlockSpec((None, TM, w), lambda b, i: (b, i + off, 0))
    lat = lambda w: pl.BlockSpec((None, TM, w), lambda b, i: (b, jnp.maximum(i + off - 1, 0), 0))
    ctx = lambda w: pl.BlockSpec((None, CTX_LEN, w), lambda b, i: (b, 0, 0))
    full = lambda a: pl.BlockSpec(a.shape, lambda b, i: (0,) * a.ndim)
    return pl.pallas_call(
        functools.partial(_outproj_kernel, final=final),
        grid=(bsz, nt),
        in_specs=[tok(D_MODEL), tok(GLA_WIDTH), tok(GLA_WIDTH), lat(MLA_WIDTH), ctx(MLA_WIDTH),
                  lat(GQA_WIDTH), ctx(GQA_WIDTH), tok(D_MIX), full(lw["gla_norm"]),
                  pl.BlockSpec((None, 1, D_MODEL),
                               lambda b, i: (2 * b + jnp.minimum(i + off, 1), 0, 2)),
                  full(lw["w_out"]), full(final_norm_w)],
        out_specs=pl.BlockSpec((None, TM, D_MODEL), lambda b, i: (b, i, 0)),
        out_shape=jax.ShapeDtypeStruct((bsz, nt * TM, D_MODEL), F32),
        compiler_params=_cparams(("parallel", "arbitrary")),
        name="outproj",
    )(xs, o_f, o_b, *y_b, *y_c, zg, lw["gla_norm"], mod, lw["w_out"], final_norm_w)


def _rot_src(d_rot):
    quarter = d_rot // 4
    i = np.arange(d_rot)
    return np.where(i % (2 * quarter) < quarter, i + quarter, i - quarter)


def _rope_tables(rows, d_rot, lanes):
    quarter = d_rot // 4
    n = rows * GRID_W
    row = jnp.repeat(jnp.arange(rows), GRID_W).astype(F32)
    col = jnp.tile(jnp.arange(GRID_W), rows).astype(F32)
    freqs = ROPE_THETA ** (-jnp.arange(quarter, dtype=F32) / quarter)
    ang = jnp.stack([row[:, None] * freqs, col[:, None] * freqs], axis=1)
    cos, sin = jnp.cos(ang), jnp.sin(ang)
    cos_l = jnp.stack([cos, cos], axis=2).reshape(n, d_rot)
    sin_l = jnp.stack([-sin, sin], axis=2).reshape(n, d_rot)
    cos_t = jnp.ones((n, LANES), F32)
    sin_t = jnp.zeros((n, LANES), F32)
    for lane0 in lanes:
        cos_t = cos_t.at[:, lane0:lane0 + d_rot].set(cos_l)
        sin_t = sin_t.at[:, lane0:lane0 + d_rot].set(sin_l)
    ident = jnp.concatenate([jnp.ones((CTX_LEN, LANES), F32), jnp.zeros((CTX_LEN, LANES), F32)], 1)
    return jnp.concatenate([ident, jnp.concatenate([cos_t, sin_t], axis=1)], axis=0)


def _take_cols(w, idx):
    w_ext = jnp.concatenate([w, jnp.zeros((w.shape[0], 1), w.dtype)], axis=1)
    return w_ext[:, np.asarray(idx)]


def _w_in_columns():
    c0 = 2 * GLA_QK + 2 * GLA_WIDTH
    m0 = c0 + 2 * GLA_GATE_RANK
    kr0 = m0 + MLA_Q_LORA + MLA_KV_LORA
    g0 = kr0 + MLA_ROPE + MLA_WIDTH
    gk0 = g0 + GQA_WIDTH
    gv0 = gk0 + GQA_KV_HEADS * HEAD_DIM
    gz0 = gv0 + GQA_KV_HEADS * HEAD_DIM
    pad = lambda n: [-1] * n
    seg_a = list(range(0, 2 * GLA_QK + GLA_WIDTH)) + list(range(c0, m0)) + pad(LANES - 2 * GLA_GATE_RANK)
    rs_m, rs_g = _rot_src(MLA_ROPE), _rot_src(HEAD_DIM)
    tail = pad(LANES - MLA_NOPE - MLA_ROPE)
    seg_b = (list(range(m0, kr0)) + pad(MLA_NOPE) + list(range(kr0, kr0 + MLA_ROPE)) + tail
             + pad(MLA_NOPE) + [kr0 + int(j) for j in rs_m] + tail)
    q_pad, q_rot = [], []
    for h in range(GQA_HEADS):
        base = g0 + h * HEAD_DIM
        own = list(range(base, base + HEAD_DIM))
        rot = [base + int(j) for j in rs_g]
        half = h // GQA_GROUP
        q_pad += own + pad(HEAD_DIM) if half == 0 else pad(HEAD_DIM) + own
        q_rot += rot + pad(HEAD_DIM) if half == 0 else pad(HEAD_DIM) + rot
    k_nat = list(range(gk0, gv0))
    k_rot = [gk0 + kv * HEAD_DIM + int(j) for kv in range(GQA_KV_HEADS) for j in rs_g]
    seg_c = q_pad + q_rot + k_nat + k_rot + list(range(gv0, gz0))
    seg_z = (list(range(2 * GLA_QK + GLA_WIDTH, c0)) + list(range(kr0 + MLA_ROPE, g0))
             + list(range(gz0, gz0 + GQA_WIDTH)))
    cols = seg_a + seg_b + seg_c + seg_z
    assert len(seg_a) == SEG_A and len(seg_b) == SEG_B and len(seg_c) == SEG_C and len(cols) == W_IN_PAD
    return cols


def _layer_weights(l, norm_w, w_in, gla_w_a_fwd, gla_b_a_fwd, gla_w_a_bwd, gla_b_a_bwd, gla_norm_w,
                   mla_q_norm_w, mla_w_uq, mla_kv_norm_w, mla_w_ukv, gqa_q_norm_w, gqa_k_norm_w,
                   w_out):
    w_pad = _take_cols(w_in[l], _w_in_columns()).astype(BF16)

    w_a = jnp.zeros((LANES, 2 * GLA_QK), F32)
    w_a = w_a.at[0:GLA_GATE_RANK, 0:GLA_QK].set(gla_w_a_fwd[l])
    w_a = w_a.at[GLA_GATE_RANK:2 * GLA_GATE_RANK, GLA_QK:].set(gla_w_a_bwd[l])
    b_a = jnp.concatenate([gla_b_a_fwd[l], gla_b_a_bwd[l]])[None, :]

    dq = MLA_NOPE + MLA_ROPE
    rs_m, rs_g = _rot_src(MLA_ROPE), _rot_src(HEAD_DIM)
    uq_cols, uq_rot_cols = [], []
    for h in range(MLA_HEADS):
        uq_cols += list(range(h * dq, (h + 1) * dq)) + [-1] * (LANES - dq)
        uq_rot_cols += ([-1] * MLA_NOPE + [h * dq + MLA_NOPE + int(j) for j in rs_m]
                        + [-1] * (LANES - dq))
    ukv = mla_w_ukv[l].reshape(MLA_KV_LORA, MLA_HEADS, MLA_NOPE + MLA_V)
    uk = jnp.pad(ukv[:, :, :MLA_NOPE], ((0, 0), (0, 0), (0, LANES - MLA_NOPE)))
    uk = uk.reshape(MLA_KV_LORA, MLA_HEADS * LANES)
    uv = ukv[:, :, MLA_NOPE:].reshape(MLA_KV_LORA, MLA_WIDTH)

    both = lambda v: jnp.tile(v, LANES // HEAD_DIM)[None, :]
    return {
        "norm_w": norm_w[l][None, :],
        "w_in": w_pad,
        "w_a": w_a.astype(BF16),
        "b_a": b_a,
        "mla_q_norm": mla_q_norm_w[l][None, :],
        "w_uq": _take_cols(mla_w_uq[l], uq_cols).astype(BF16),
        "w_uq_rot": _take_cols(mla_w_uq[l], uq_rot_cols).astype(BF16),
        "mla_kv_norm": mla_kv_norm_w[l][None, :],
        "w_uk": uk.astype(BF16),
        "w_uv": uv.astype(BF16),
        "gqa_q_norm": both(gqa_q_norm_w[l]),
        "gqa_q_norm_rot": both(gqa_q_norm_w[l][rs_g]),
        "gqa_k_norm": both(gqa_k_norm_w[l]),
        "gqa_k_norm_rot": both(gqa_k_norm_w[l][rs_g]),
        "gla_norm": jnp.tile(gla_norm_w[l], GLA_HEADS)[None, :],
        "w_out": w_out[l].astype(BF16),
    }


def kernel(x, c, ctx, c_ctx, norm_w, w_mod, b_mod, w_in, gla_w_a_fwd, gla_b_a_fwd, gla_w_a_bwd,
           gla_b_a_bwd, gla_norm_w, mla_q_norm_w, mla_w_uq, mla_kv_norm_w, mla_w_ukv,
           gqa_q_norm_w, gqa_k_norm_w, w_out, final_norm_w):
    bsz, n, _ = x.shape
    depth = w_in.shape[0]
    assert ctx.shape[1] == CTX_LEN and n % (2 * TM) == 0 and n % GRID_W == 0
    rows = n // GRID_W
    rope_m = _rope_tables(rows, MLA_ROPE, (MLA_NOPE,))
    rope_g = _rope_tables(rows, HEAD_DIM, (0, HEAD_DIM))

    mod_rows = -(-(bsz + 1) // 8) * 8
    cs = jnp.zeros((mod_rows, D_MODEL), F32).at[:bsz].set(c).at[bsz].set(c_ctx)
    mod_all = _modulation(cs, w_mod, b_mod)
    fnw = final_norm_w[None, :]

    xs = jnp.concatenate([ctx, x], axis=1)
    for l in range(depth):
        lw = _layer_weights(l, norm_w, w_in, gla_w_a_fwd, gla_b_a_fwd, gla_w_a_bwd, gla_b_a_bwd,
                            gla_norm_w, mla_q_norm_w, mla_w_uq, mla_kv_norm_w, mla_w_ukv,
                            gqa_q_norm_w, gqa_k_norm_w, w_out)
        m = mod_all[l]
        mod = jnp.stack([jnp.broadcast_to(m[bsz], (bsz, 3 * D_MODEL)), m[:bsz]], axis=1)
        mod = mod.reshape(2 * bsz, 1, 3 * D_MODEL)
        gq, gk, gv, gg, zg, mq, mk, mvt, cq, ck, cvt = _prep(xs, mod, lw, rope_m, rope_g)
        o_f, o_b = _gla(gq, gk, gv, gg)
        y_b = _attention(mq, mk, mvt, shared_kv=False)
        y_c = _attention(cq, ck, cvt, shared_kv=True)
        xs = _outproj(xs, o_f, o_b, y_b, y_c, zg, mod, lw, fnw, final=(l == depth - 1))
    return xs
```

```python
import functools

import numpy as np
import jax
import jax.numpy as jnp
from jax import lax
from jax.experimental import pallas as pl
from jax.experimental.pallas import tpu as pltpu

D_MODEL = 1024
GRID_W = 64
CTX_LEN = 256
HEAD_DIM = 64
ROPE_THETA = 10000.0
EPS = 1e-6

GLA_HEADS = 4
GLA_DK = 32
GLA_DV = 64
GLA_WIDTH = GLA_HEADS * GLA_DV
GLA_QK = GLA_HEADS * GLA_DK
GLA_GATE_RANK = 16
GLA_GATE_NORM = 16.0
GLA_CHUNK = 64

MLA_HEADS = 6
MLA_NOPE = 64
MLA_ROPE = 32
MLA_V = 64
MLA_Q_LORA = 256
MLA_KV_LORA = 256
MLA_WIDTH = MLA_HEADS * MLA_V

GQA_HEADS = 6
GQA_KV_HEADS = 2
GQA_GROUP = GQA_HEADS // GQA_KV_HEADS
GQA_WIDTH = GQA_HEADS * HEAD_DIM

D_MIX = GLA_WIDTH + MLA_WIDTH + GQA_WIDTH

LANES = 128
TM = CTX_LEN
CHUNKS = TM // GLA_CHUNK
SEG_A = 640
SEG_B = 768
SEG_C = 1920
W_IN_PAD = SEG_A + SEG_B + SEG_C + D_MIX
VMEM_LIMIT = 48 * 1024 * 1024
GLA_BATCH = 2
STEP_BATCH = 2
ATT_CK = 256
ATT_AHEAD = 1
LOG2E = 1.4426950408889634

F32 = jnp.float32
BF16 = jnp.bfloat16


def _cparams(sem):
    return pltpu.CompilerParams(dimension_semantics=sem, vmem_limit_bytes=VMEM_LIMIT)


def _dot(a, b):
    return jnp.dot(a, b, preferred_element_type=F32)


def _dot_nt(a, b):
    return lax.dot_general(a, b, (((1,), (1,)), ((), ())), preferred_element_type=F32)


def _dot_tn(a, b):
    return lax.dot_general(a, b, (((0,), (0,)), ((), ())), preferred_element_type=F32)


def _silu(v):
    return v / (1.0 + jnp.exp(-v))


def _mod_kernel(c_ref, w_ref, b_ref, o_ref):
    s = _silu(c_ref[...]).astype(BF16)
    o_ref[...] = _dot(s, w_ref[...].astype(BF16)) + b_ref[...]


def _modulation(cs, w_mod, b_mod):
    depth = w_mod.shape[0]
    rows = cs.shape[0]
    return pl.pallas_call(
        _mod_kernel,
        grid=(depth, 3),
        in_specs=[
            pl.BlockSpec((rows, D_MODEL), lambda l, j: (0, 0)),
            pl.BlockSpec((None, D_MODEL, D_MODEL), lambda l, j: (l, 0, j)),
            pl.BlockSpec((None, 1, D_MODEL), lambda l, j: (l, 0, j)),
        ],
        out_specs=pl.BlockSpec((None, rows, D_MODEL), lambda l, j: (l, 0, j)),
        out_shape=jax.ShapeDtypeStruct((depth, rows, 3 * D_MODEL), F32),
        compiler_params=_cparams(("arbitrary", "arbitrary")),
        name="modulation",
    )(cs, w_mod, b_mod.reshape(depth, 1, 3 * D_MODEL))


def _prep_kernel(x_ref, mod_ref, nw_ref, win_ref, wa_ref, ba_ref, mqn_ref, wuq_ref, wuqr_ref,
                 mkvn_ref, wuk_ref, wuv_ref, gqn_ref, gqnr_ref, gkn_ref, gknr_ref, rm_ref, rg_ref,
                 gq_ref, gk_ref, gv_ref, gg_ref, zg_ref, mq_ref, mk_ref, mvt_ref, cq_ref, ck_ref,
                 cvt_ref):
    x = x_ref[...]
    mod = mod_ref[...]
    shift, scale = mod[:, :D_MODEL], mod[:, D_MODEL:2 * D_MODEL]
    ms = jnp.mean(x * x, axis=-1, keepdims=True)
    h = (x * lax.rsqrt(ms + EPS) * nw_ref[...]) * (1.0 + scale) + shift
    hb = h.astype(BF16)

    uz = _dot(hb, win_ref[:, SEG_A + SEG_B + SEG_C:])
    zg_ref[...] = _silu(uz).astype(BF16)

    ua = _dot(hb, win_ref[:, 0:SEG_A])
    gq_ref[...] = (ua[:, 0:GLA_QK] * GLA_DK ** -0.5).astype(BF16)
    gk_ref[...] = ua[:, GLA_QK:2 * GLA_QK].astype(BF16)
    gv_ref[...] = ua[:, 2 * GLA_QK:2 * GLA_QK + GLA_WIDTH].astype(BF16)
    a = ua[:, 2 * GLA_QK + GLA_WIDTH:SEG_A].astype(BF16)
    xg = _dot(a, wa_ref[...]) + ba_ref[...]
    log_sig = jnp.minimum(xg, 0.0) - jnp.log1p(jnp.exp(-jnp.abs(xg)))
    gg_ref[...] = log_sig * (1.0 / GLA_GATE_NORM)

    ub = _dot(hb, win_ref[:, SEG_A:SEG_A + SEG_B])
    cq = ub[:, 0:MLA_Q_LORA]
    ckv = ub[:, MLA_Q_LORA:MLA_Q_LORA + MLA_KV_LORA]
    kr = ub[:, MLA_Q_LORA + MLA_KV_LORA:MLA_Q_LORA + MLA_KV_LORA + LANES]
    kr_rot = ub[:, MLA_Q_LORA + MLA_KV_LORA + LANES:SEG_B]
    cqn = cq * lax.rsqrt(jnp.mean(cq * cq, axis=-1, keepdims=True) + EPS) * mqn_ref[...]
    ckvn = ckv * lax.rsqrt(jnp.mean(ckv * ckv, axis=-1, keepdims=True) + EPS) * mkvn_ref[...]
    cqb = cqn.astype(BF16)
    qm = _dot(cqb, wuq_ref[...])
    qm_rot = _dot(cqb, wuqr_ref[...])
    ckvb = ckvn.astype(BF16)
    kn = _dot(ckvb, wuk_ref[...])
    vm = _dot(ckvb, wuv_ref[...])
    cos_m, sin_m = rm_ref[:, :LANES], rm_ref[:, LANES:]
    k_rope = kr * cos_m + kr_rot * sin_m
    mla_scale = (MLA_NOPE + MLA_ROPE) ** -0.5 * LOG2E
    for hd in range(MLA_HEADS):
        sl = slice(hd * LANES, (hd + 1) * LANES)
        mq_ref[:, sl] = ((qm[:, sl] * cos_m + qm_rot[:, sl] * sin_m) * mla_scale).astype(BF16)
        mk_ref[:, sl] = (kn[:, sl] + k_rope).astype(BF16)
    for p in range(MLA_HEADS // 2):
        mvt_ref[p] = vm[:, p * LANES:(p + 1) * LANES].T.astype(BF16)

    uc = _dot(hb, win_ref[:, SEG_A + SEG_B:SEG_A + SEG_B + SEG_C])
    cos_g, sin_g = rg_ref[:, :LANES], rg_ref[:, LANES:]
    qa, qb = cos_g * gqn_ref[...], sin_g * gqnr_ref[...]
    gqa_scale = HEAD_DIM ** -0.5 * LOG2E
    nq = GQA_HEADS * LANES
    for hd in range(GQA_HEADS):
        sl = slice(hd * LANES, (hd + 1) * LANES)
        xh = uc[:, sl]
        xr = uc[:, nq + hd * LANES:nq + (hd + 1) * LANES]
        msq = jnp.sum(xh * xh, axis=-1, keepdims=True) * (1.0 / HEAD_DIM)
        cq_ref[:, sl] = ((xh * qa + xr * qb) * (lax.rsqrt(msq + EPS) * gqa_scale)).astype(BF16)
    k = uc[:, 2 * nq:2 * nq + LANES]
    k_rot = uc[:, 2 * nq + LANES:2 * nq + 2 * LANES]
    lo = lax.broadcasted_iota(jnp.int32, (TM, LANES), 1) < HEAD_DIM
    k2 = k * k
    ms_lo = jnp.sum(jnp.where(lo, k2, 0.0), axis=-1, keepdims=True) * (1.0 / HEAD_DIM)
    ms_hi = jnp.sum(jnp.where(lo, 0.0, k2), axis=-1, keepdims=True) * (1.0 / HEAD_DIM)
    inv = jnp.where(lo, lax.rsqrt(ms_lo + EPS), lax.rsqrt(ms_hi + EPS))
    ck_ref[...] = ((k * (cos_g * gkn_ref[...]) + k_rot * (sin_g * gknr_ref[...])) * inv).astype(BF16)
    cvt_ref[...] = uc[:, 2 * nq + 2 * LANES:SEG_C].T.astype(BF16)


def _prep_multi_kernel(x_ref, mod_ref, *refs):
    shared, outs = refs[:16], refs[16:]
    for e in range(x_ref.shape[0]):
        _prep_kernel(x_ref.at[e], mod_ref.at[e], *shared, *[o.at[e] for o in outs])


def _prep(xs, mod, lw, rope_m, rope_g):
    bsz, t, _ = xs.shape
    nt = t // TM
    nb = STEP_BATCH if bsz % STEP_BATCH == 0 else 1
    tok = lambda w: pl.BlockSpec((nb, TM, w), lambda b, i: (b, i, 0))
    full = lambda a: pl.BlockSpec(a.shape, lambda b, i: (0,) * a.ndim)
    tab = pl.BlockSpec((TM, 2 * LANES), lambda b, i: (i, 0))
    weights = [lw[n] for n in ("norm_w", "w_in", "w_a", "b_a", "mla_q_norm", "w_uq", "w_uq_rot",
                               "mla_kv_norm", "w_uk", "w_uv", "gqa_q_norm", "gqa_q_norm_rot",
                               "gqa_k_norm", "gqa_k_norm_rot")]
    out_shape = [
        jax.ShapeDtypeStruct((bsz, t, GLA_QK), BF16),
        jax.ShapeDtypeStruct((bsz, t, GLA_QK), BF16),
        jax.ShapeDtypeStruct((bsz, t, GLA_WIDTH), BF16),
        jax.ShapeDtypeStruct((bsz, t, 2 * GLA_QK), F32),
        jax.ShapeDtypeStruct((bsz, t, D_MIX), BF16),
        jax.ShapeDtypeStruct((bsz, t, MLA_HEADS * LANES), BF16),
        jax.ShapeDtypeStruct((bsz, t, MLA_HEADS * LANES), BF16),
        jax.ShapeDtypeStruct((bsz, MLA_HEADS // 2, LANES, t), BF16),
        jax.ShapeDtypeStruct((bsz, t, GQA_HEADS * LANES), BF16),
        jax.ShapeDtypeStruct((bsz, t, LANES), BF16),
        jax.ShapeDtypeStruct((bsz, LANES, t), BF16),
    ]
    out_specs = [
        tok(GLA_QK), tok(GLA_QK), tok(GLA_WIDTH), tok(2 * GLA_QK), tok(D_MIX),
        tok(MLA_HEADS * LANES), tok(MLA_HEADS * LANES),
        pl.BlockSpec((nb, MLA_HEADS // 2, LANES, TM), lambda b, i: (b, 0, 0, i)),
        tok(GQA_HEADS * LANES), tok(LANES),
        pl.BlockSpec((nb, LANES, TM), lambda b, i: (b, 0, i)),
    ]
    return pl.pallas_call(
        _prep_multi_kernel,
        grid=(bsz // nb, nt),
        in_specs=[tok(D_MODEL),
                  pl.BlockSpec((nb, None, 1, 3 * D_MODEL), lambda b, i: (b, jnp.minimum(i, 1), 0, 0))]
                 + [full(w) for w in weights] + [tab, tab],
        out_specs=out_specs,
        out_shape=out_shape,
        compiler_params=_cparams(("parallel", "arbitrary")),
        name="prep",
    )(xs, mod, *weights, rope_m, rope_g)


def _gla_direction(q_ref, k_ref, v_ref, g_ref, l_ref, cmask_ref, qmask, omask, smask, s_ref, o_ref,
                   reverse):
    q = q_ref[...].astype(F32)
    k = k_ref[...].astype(F32)
    g = g_ref[...]
    g_hi = g.astype(BF16)
    r1 = g - g_hi.astype(F32)
    g_mid = r1.astype(BF16)
    g_lo = (r1 - g_mid.astype(F32)).astype(BF16)
    tri = l_ref[...]
    cum = _dot(tri, g_hi) + _dot(tri, g_mid) + _dot(tri, g_lo)
    cmask = cmask_ref[...]
    order = range(CHUNKS - 1, -1, -1) if reverse else range(CHUNKS)
    for c in order:
        sl = slice(c * GLA_CHUNK, (c + 1) * GLA_CHUNK)
        cum_c = cum[sl]
        last = cum_c[0:1] if reverse else cum_c[GLA_CHUNK - 1:GLA_CHUNK]
        qd = q[sl] * jnp.exp(cum_c)
        ki = k[sl] * jnp.exp(-cum_c)
        kd = k[sl] * jnp.exp(last - cum_c)
        vc = v_ref[sl, :]
        qh = jnp.concatenate([qd] * GLA_HEADS, axis=0) * qmask
        att = _dot_nt(qh.astype(BF16), ki.astype(BF16)) * cmask
        o_all = _dot(att.astype(BF16), vc) * omask
        o_intra = o_all[0:GLA_CHUNK]
        for hd in range(1, GLA_HEADS):
            o_intra = o_intra + o_all[hd * GLA_CHUNK:(hd + 1) * GLA_CHUNK]
        st = s_ref[...]
        o_inter = _dot_nt(qd.astype(BF16), st.astype(BF16))
        o_ref[sl, :] = o_inter + o_intra
        ds = _dot_tn(vc, kd.astype(BF16)) * smask
        s_ref[...] = st * jnp.exp(last) + ds


def _gla_kernel(qf_ref, kf_ref, vf_ref, gf_ref, qb_ref, kb_ref, vb_ref, gb_ref, lf_ref, lb_ref,
                cmf_ref, cmb_ref, qmask_ref, omask_ref, smask_ref, of_ref, ob_ref, sf_ref, sb_ref):
    @pl.when(pl.program_id(1) == 0)
    def _():
        sf_ref[...] = jnp.zeros_like(sf_ref)
        sb_ref[...] = jnp.zeros_like(sb_ref)

    qmask, omask, smask = qmask_ref[...], omask_ref[...], smask_ref[...]
    for e in range(qf_ref.shape[0]):
        _gla_direction(qf_ref.at[e], kf_ref.at[e], vf_ref.at[e], gf_ref.at[e], lf_ref, cmf_ref,
                       qmask, omask, smask, sf_ref.at[e], of_ref.at[e], False)
        _gla_direction(qb_ref.at[e], kb_ref.at[e], vb_ref.at[e], gb_ref.at[e], lb_ref, cmb_ref,
                       qmask, omask, smask, sb_ref.at[e], ob_ref.at[e], True)


def _gla_consts():
    r = np.arange(TM)
    same = (r[:, None] // GLA_CHUNK) == (r[None, :] // GLA_CHUNK)
    lf = (same & (r[None, :] <= r[:, None])).astype(np.float32)
    lb = (same & (r[None, :] >= r[:, None])).astype(np.float32)
    rows = np.arange(GLA_HEADS * GLA_CHUNK)
    j = np.arange(GLA_CHUNK)
    cmf = ((rows[:, None] % GLA_CHUNK) >= j[None, :]).astype(np.float32)
    cmb = ((rows[:, None] % GLA_CHUNK) <= j[None, :]).astype(np.float32)
    qmask = ((rows[:, None] // GLA_CHUNK) == (np.arange(GLA_QK)[None, :] // GLA_DK)).astype(np.float32)
    omask = ((rows[:, None] // GLA_CHUNK) == (np.arange(GLA_WIDTH)[None, :] // GLA_DV)).astype(np.float32)
    smask = ((np.arange(GLA_WIDTH)[:, None] // GLA_DV) == (np.arange(GLA_QK)[None, :] // GLA_DK)).astype(np.float32)
    return (jnp.asarray(lf, BF16), jnp.asarray(lb, BF16), jnp.asarray(cmf), jnp.asarray(cmb),
            jnp.asarray(qmask), jnp.asarray(omask), jnp.asarray(smask))


def _gla(gq, gk, gv, gg):
    bsz, t, _ = gq.shape
    nt = t // TM
    consts = _gla_consts()
    fwd = lambda b, i: (b, i, 0)
    rev_blk = lambda i: jnp.where(i == 0, 0, nt - i)
    bwd = lambda b, i: (b, rev_blk(i), 0)
    bwd_g = lambda b, i: (b, rev_blk(i), 1)
    nb = GLA_BATCH if bsz % GLA_BATCH == 0 else 1
    spec = lambda w, im: pl.BlockSpec((nb, TM, w), im)
    full = lambda a: pl.BlockSpec(a.shape, lambda b, i: (0,) * a.ndim)
    return pl.pallas_call(
        _gla_kernel,
        grid=(bsz // nb, nt),
        in_specs=[spec(GLA_QK, fwd), spec(GLA_QK, fwd), spec(GLA_WIDTH, fwd), spec(GLA_QK, fwd),
                  spec(GLA_QK, bwd), spec(GLA_QK, bwd), spec(GLA_WIDTH, bwd), spec(GLA_QK, bwd_g)]
                 + [full(a) for a in consts],
        out_specs=[spec(GLA_WIDTH, fwd), spec(GLA_WIDTH, bwd)],
        out_shape=[jax.ShapeDtypeStruct((bsz, t, GLA_WIDTH), F32)] * 2,
        scratch_shapes=[pltpu.VMEM((nb, GLA_WIDTH, GLA_QK), F32)] * 2,
        compiler_params=_cparams(("parallel", "arbitrary")),
        name="gla",
    )(gq, gk, gv, gg, gq, gk, gv, gg, *consts)


def _head_values(vt_ref, h, cols, shared_kv):
    if shared_kv:
        r0 = (h // GQA_GROUP) * HEAD_DIM
        return vt_ref[r0:r0 + HEAD_DIM, cols]
    r0 = (h % 2) * MLA_V
    return vt_ref[h // 2, r0:r0 + MLA_V, cols]


def _write_heads(o_ref, outs_t):
    for p in range(len(outs_t) // 2):
        pair = jnp.concatenate([outs_t[2 * p], outs_t[2 * p + 1]], axis=0)
        o_ref[:, p * LANES:(p + 1) * LANES] = pair.T.astype(o_ref.dtype)


def _attn_ctx_kernel(q_ref, k_ref, vt_ref, o_ref, *, shared_kv):
    heads = q_ref.shape[1] // LANES
    outs_t = []
    for h in range(heads):
        kh = k_ref[...] if shared_kv else k_ref[:, h * LANES:(h + 1) * LANES]
        s = _dot_nt(kh, q_ref[:, h * LANES:(h + 1) * LANES])
        p = jnp.exp2(s - jnp.max(s, axis=0, keepdims=True))
        l = jnp.sum(p, axis=0, keepdims=True)
        outs_t.append(_dot(_head_values(vt_ref, h, slice(None), shared_kv), p.astype(BF16)) / l)
    _write_heads(o_ref, outs_t)


def _attn_kernel(z_ref, qa_ref, qb_ref, k_ref, vt_ref, o_ref, s_ref, *, shared_kv):
    t = k_ref.shape[0]
    heads = qa_ref.shape[1] // LANES
    chunks = [slice(a, a + ATT_CK) for a in range(0, t, ATT_CK)]
    z = z_ref[0]

    def dyn(start, size):
        return pl.ds(pl.multiple_of(z + start, ATT_CK), size)

    def q_of(h):
        sl = slice(h * LANES, (h + 1) * LANES)
        qh = jnp.concatenate([qa_ref[:, sl], qb_ref[:, sl]], axis=0)
        return qh.astype(F32).T.astype(BF16)

    def fold8(v):
        return v.reshape(v.shape[0] // 8, 8, v.shape[1])

    def qk(h, qh, rows):
        kh = k_ref[rows, :] if shared_kv else k_ref[rows, h * LANES:(h + 1) * LANES]
        return _dot(kh, qh)

    def put_scores(s, rows, m8):
        s_ref[dyn(rows.start, ATT_CK), :] = s
        smax = jnp.max(fold8(s), axis=0)
        return smax if m8 is None else jnp.maximum(m8, smax)

    ones = jnp.ones((16, ATT_CK), BF16)
    m8 = None
    q_all = [q_of(h) for h in range(heads)]
    qh = q_all[0]
    for rows in chunks:
        m8 = put_scores(qk(0, qh, rows), rows, m8)
    outs_t = []
    for h in range(heads):
        m = jnp.max(m8, axis=0, keepdims=True)
        m8, acc = None, None
        more = h + 1 < heads
        if more:
            qh = q_all[h + 1]
            ahead = [qk(h + 1, qh, chunks[i]) for i in range(ATT_AHEAD)]
        for c, rows in enumerate(chunks):
            if more:
                s_cur = ahead.pop(0)
                if c + ATT_AHEAD < len(chunks):
                    ahead.append(qk(h + 1, qh, chunks[c + ATT_AHEAD]))
            vt1 = jnp.concatenate([_head_values(vt_ref, h, rows, shared_kv), ones], axis=0)
            p = jnp.exp2(s_ref[dyn(rows.start, ATT_CK), :] - m).astype(BF16)
            if more:
                m8 = put_scores(s_cur, rows, m8)
            pv = _dot(vt1, p)
            acc = pv if acc is None else acc + pv
        outs_t.append(acc[0:HEAD_DIM] / acc[HEAD_DIM:HEAD_DIM + 1])
    _write_heads(o_ref, outs_t)


def _attention(q, k, vt, shared_kv):
    bsz, t, width = q.shape
    n2 = (t - CTX_LEN) // (2 * TM)
    kw = k.shape[2]
    name = "attn_gqa" if shared_kv else "attn_mla"
    if shared_kv:
        v_spec = pl.BlockSpec((None, LANES, t), lambda b, j: (b, 0, 0))
        vc_spec = pl.BlockSpec((None, LANES, CTX_LEN), lambda b: (b, 0, 0))
    else:
        v_spec = pl.BlockSpec((None, vt.shape[1], LANES, t), lambda b, j: (b, 0, 0, 0))
        vc_spec = pl.BlockSpec((None, vt.shape[1], LANES, CTX_LEN), lambda b: (b, 0, 0, 0))
    y_lat = pl.pallas_call(
        functools.partial(_attn_kernel, shared_kv=shared_kv),
        grid=(bsz, n2),
        in_specs=[pl.BlockSpec(memory_space=pltpu.SMEM),
                  pl.BlockSpec((None, TM, width), lambda b, j: (b, 2 * j + 1, 0)),
                  pl.BlockSpec((None, TM, width), lambda b, j: (b, 2 * j + 2, 0)),
                  pl.BlockSpec((None, t, kw), lambda b, j: (b, 0, 0)), v_spec],
        out_specs=pl.BlockSpec((None, 2 * TM, width // 2), lambda b, j: (b, j, 0)),
        out_shape=jax.ShapeDtypeStruct((bsz, t - CTX_LEN, width // 2), BF16),
        scratch_shapes=[pltpu.VMEM((t, 2 * TM), F32)],
        compiler_params=_cparams(("parallel", "arbitrary")),
        name=name,
    )(jnp.zeros((1,), jnp.int32), q, q, k, vt)
    y_ctx = pl.pallas_call(
        functools.partial(_attn_ctx_kernel, shared_kv=shared_kv),
        grid=(bsz,),
        in_specs=[pl.BlockSpec((None, CTX_LEN, width), lambda b: (b, 0, 0)),
                  pl.BlockSpec((None, CTX_LEN, kw), lambda b: (b, 0, 0)), vc_spec],
        out_specs=pl.BlockSpec((None, CTX_LEN, width // 2), lambda b: (b, 0, 0)),
        out_shape=jax.ShapeDtypeStruct((bsz, CTX_LEN, width // 2), BF16),
        compiler_params=_cparams(("parallel",)),
        name=name + "_ctx",
    )(q, k, vt)
    return y_lat, y_ctx


def _outproj_kernel(x_ref, of_ref, ob_ref, yb_ref, ybc_ref, yc_ref, ycc_ref, zg_ref, gnw_ref,
                    mod_ref, wout_ref, fnw_ref, o_ref, *, final):
    lo = lax.broadcasted_iota(jnp.int32, (TM, LANES), 1) < GLA_DV
    o = of_ref[...] + ob_ref[...]
    gnw = gnw_ref[...]
    zg = zg_ref[...]
    parts = []
    for j in range(GLA_WIDTH // LANES):
        sl = slice(j * LANES, (j + 1) * LANES)
        oj = o[:, sl]
        o2 = oj * oj
        ms_lo = jnp.sum(jnp.where(lo, o2, 0.0), axis=-1, keepdims=True) * (1.0 / GLA_DV)
        ms_hi = jnp.sum(jnp.where(lo, 0.0, o2), axis=-1, keepdims=True) * (1.0 / GLA_DV)
        inv = jnp.where(lo, lax.rsqrt(ms_lo + EPS), lax.rsqrt(ms_hi + EPS))
        parts.append(oj * inv * gnw[:, sl] * zg[:, sl])
    if final:
        yb, yc = yb_ref[...], yc_ref[...]
    else:
        is_ctx = pl.program_id(1) == 0
        yb = jnp.where(is_ctx, ybc_ref[...], yb_ref[...])
        yc = jnp.where(is_ctx, ycc_ref[...], yc_ref[...])
    parts.append(yb * zg[:, GLA_WIDTH:GLA_WIDTH + MLA_WIDTH])
    parts.append(yc * zg[:, GLA_WIDTH + MLA_WIDTH:])
    y = jnp.concatenate(parts, axis=-1).astype(BF16)
    gate = mod_ref[...]
    xn = x_ref[...] + gate * _dot(y, wout_ref[...])
    if final:
        xn = xn * lax.rsqrt(jnp.mean(xn * xn, axis=-1, keepdims=True) + EPS) * fnw_ref[...]
    o_ref[...] = xn


def _outproj_multi_kernel(*refs, final):
    per, gnw_ref, mod_ref, wout_ref, fnw_ref, o_ref = refs[:8], refs[8], refs[9], refs[10], refs[11], refs[12]
    for e in range(o_ref.shape[0]):
        _outproj_kernel(*[r.at[e] for r in per], gnw_ref, mod_ref.at[e], wout_ref, fnw_ref,
                        o_ref.at[e], final=final)


def _outproj(xs, o_f, o_b, y_b, y_c, zg, mod, lw, final_norm_w, final):
    bsz, t, _ = xs.shape
    off = 1 if final else 0
    nt = t // TM - off
    nb = STEP_BATCH if bsz % STEP_BATCH == 0 else 1
    tok = lambda w: pl.BlockSpec((nb, TM, w), lambda b, i: (b, i + off, 0))
    lat = lambda w: pl.BlockSpec((nb, TM, w), lambda b, i: (b, jnp.maximum(i + off - 1, 0), 0))
    ctx = lambda w: pl.BlockSpec((nb, CTX_LEN, w), lambda b, i: (b, 0, 0))
    full = lambda a: pl.BlockSpec(a.shape, lambda b, i: (0,) * a.ndim)
    return pl.pallas_call(
        functools.partial(_outproj_multi_kernel, final=final),
        grid=(bsz // nb, nt),
        in_specs=[tok(D_MODEL), tok(GLA_WIDTH), tok(GLA_WIDTH), lat(MLA_WIDTH), ctx(MLA_WIDTH),
                  lat(GQA_WIDTH), ctx(GQA_WIDTH), tok(D_MIX), full(lw["gla_norm"]),
                  pl.BlockSpec((nb, None, 1, D_MODEL),
                               lambda b, i: (b, jnp.minimum(i + off, 1), 0, 2)),
                  full(lw["w_out"]), full(final_norm_w)],
        out_specs=pl.BlockSpec((nb, TM, D_MODEL), lambda b, i: (b, i, 0)),
        out_shape=jax.ShapeDtypeStruct((bsz, nt * TM, D_MODEL), F32),
        compiler_params=_cparams(("parallel", "arbitrary")),
        name="outproj",
    )(xs, o_f, o_b, *y_b, *y_c, zg, lw["gla_norm"], mod, lw["w_out"], final_norm_w)


def _rot_src(d_rot):
    quarter = d_rot // 4
    i = np.arange(d_rot)
    return np.where(i % (2 * quarter) < quarter, i + quarter, i - quarter)


def _rope_tables(rows, d_rot, lanes):
    quarter = d_rot // 4
    n = rows * GRID_W
    row = jnp.repeat(jnp.arange(rows), GRID_W).astype(F32)
    col = jnp.tile(jnp.arange(GRID_W), rows).astype(F32)
    freqs = ROPE_THETA ** (-jnp.arange(quarter, dtype=F32) / quarter)
    ang = jnp.stack([row[:, None] * freqs, col[:, None] * freqs], axis=1)
    cos, sin = jnp.cos(ang), jnp.sin(ang)
    cos_l = jnp.stack([cos, cos], axis=2).reshape(n, d_rot)
    sin_l = jnp.stack([-sin, sin], axis=2).reshape(n, d_rot)
    cos_t = jnp.ones((n, LANES), F32)
    sin_t = jnp.zeros((n, LANES), F32)
    for lane0 in lanes:
        cos_t = cos_t.at[:, lane0:lane0 + d_rot].set(cos_l)
        sin_t = sin_t.at[:, lane0:lane0 + d_rot].set(sin_l)
    ident = jnp.concatenate([jnp.ones((CTX_LEN, LANES), F32), jnp.zeros((CTX_LEN, LANES), F32)], 1)
    return jnp.concatenate([ident, jnp.concatenate([cos_t, sin_t], axis=1)], axis=0)


def _take_cols(w, idx):
    w_ext = jnp.concatenate([w, jnp.zeros((w.shape[0], 1), w.dtype)], axis=1)
    return w_ext[:, np.asarray(idx)]


def _w_in_columns():
    c0 = 2 * GLA_QK + 2 * GLA_WIDTH
    m0 = c0 + 2 * GLA_GATE_RANK
    kr0 = m0 + MLA_Q_LORA + MLA_KV_LORA
    g0 = kr0 + MLA_ROPE + MLA_WIDTH
    gk0 = g0 + GQA_WIDTH
    gv0 = gk0 + GQA_KV_HEADS * HEAD_DIM
    gz0 = gv0 + GQA_KV_HEADS * HEAD_DIM
    pad = lambda n: [-1] * n
    seg_a = list(range(0, 2 * GLA_QK + GLA_WIDTH)) + list(range(c0, m0)) + pad(LANES - 2 * GLA_GATE_RANK)
    rs_m, rs_g = _rot_src(MLA_ROPE), _rot_src(HEAD_DIM)
    tail = pad(LANES - MLA_NOPE - MLA_ROPE)
    seg_b = (list(range(m0, kr0)) + pad(MLA_NOPE) + list(range(kr0, kr0 + MLA_ROPE)) + tail
             + pad(MLA_NOPE) + [kr0 + int(j) for j in rs_m] + tail)
    q_pad, q_rot = [], []
    for h in range(GQA_HEADS):
        base = g0 + h * HEAD_DIM
        own = list(range(base, base + HEAD_DIM))
        rot = [base + int(j) for j in rs_g]
        half = h // GQA_GROUP
        q_pad += own + pad(HEAD_DIM) if half == 0 else pad(HEAD_DIM) + own
        q_rot += rot + pad(HEAD_DIM) if half == 0 else pad(HEAD_DIM) + rot
    k_nat = list(range(gk0, gv0))
    k_rot = [gk0 + kv * HEAD_DIM + int(j) for kv in range(GQA_KV_HEADS) for j in rs_g]
    seg_c = q_pad + q_rot + k_nat + k_rot + list(range(gv0, gz0))
    seg_z = (list(range(2 * GLA_QK + GLA_WIDTH, c0)) + list(range(kr0 + MLA_ROPE, g0))
             + list(range(gz0, gz0 + GQA_WIDTH)))
    cols = seg_a + seg_b + seg_c + seg_z
    assert len(seg_a) == SEG_A and len(seg_b) == SEG_B and len(seg_c) == SEG_C and len(cols) == W_IN_PAD
    return cols


def _layer_weights(l, norm_w, w_in, gla_w_a_fwd, gla_b_a_fwd, gla_w_a_bwd, gla_b_a_bwd, gla_norm_w,
                   mla_q_norm_w, mla_w_uq, mla_kv_norm_w, mla_w_ukv, gqa_q_norm_w, gqa_k_norm_w,
                   w_out):
    w_pad = _take_cols(w_in[l], _w_in_columns()).astype(BF16)

    w_a = jnp.zeros((LANES, 2 * GLA_QK), F32)
    w_a = w_a.at[0:GLA_GATE_RANK, 0:GLA_QK].set(gla_w_a_fwd[l])
    w_a = w_a.at[GLA_GATE_RANK:2 * GLA_GATE_RANK, GLA_QK:].set(gla_w_a_bwd[l])
    b_a = jnp.concatenate([gla_b_a_fwd[l], gla_b_a_bwd[l]])[None, :]

    dq = MLA_NOPE + MLA_ROPE
    rs_m, rs_g = _rot_src(MLA_ROPE), _rot_src(HEAD_DIM)
    uq_cols, uq_rot_cols = [], []
    for h in range(MLA_HEADS):
        uq_cols += list(range(h * dq, (h + 1) * dq)) + [-1] * (LANES - dq)
        uq_rot_cols += ([-1] * MLA_NOPE + [h * dq + MLA_NOPE + int(j) for j in rs_m]
                        + [-1] * (LANES - dq))
    ukv = mla_w_ukv[l].reshape(MLA_KV_LORA, MLA_HEADS, MLA_NOPE + MLA_V)
    uk = jnp.pad(ukv[:, :, :MLA_NOPE], ((0, 0), (0, 0), (0, LANES - MLA_NOPE)))
    uk = uk.reshape(MLA_KV_LORA, MLA_HEADS * LANES)
    uv = ukv[:, :, MLA_NOPE:].reshape(MLA_KV_LORA, MLA_WIDTH)

    both = lambda v: jnp.tile(v, LANES // HEAD_DIM)[None, :]
    return {
        "norm_w": norm_w[l][None, :],
        "w_in": w_pad,
        "w_a": w_a.astype(BF16),
        "b_a": b_a,
        "mla_q_norm": mla_q_norm_w[l][None, :],
        "w_uq": _take_cols(mla_w_uq[l], uq_cols).astype(BF16),
        "w_uq_rot": _take_cols(mla_w_uq[l], uq_rot_cols).astype(BF16),
        "mla_kv_norm": mla_kv_norm_w[l][None, :],
        "w_uk": uk.astype(BF16),
        "w_uv": uv.astype(BF16),
        "gqa_q_norm": both(gqa_q_norm_w[l]),
        "gqa_q_norm_rot": both(gqa_q_norm_w[l][rs_g]),
        "gqa_k_norm": both(gqa_k_norm_w[l]),
        "gqa_k_norm_rot": both(gqa_k_norm_w[l][rs_g]),
        "gla_norm": jnp.tile(gla_norm_w[l], GLA_HEADS)[None, :],
        "w_out": w_out[l].astype(BF16),
    }


def kernel(x, c, ctx, c_ctx, norm_w, w_mod, b_mod, w_in, gla_w_a_fwd, gla_b_a_fwd, gla_w_a_bwd,
           gla_b_a_bwd, gla_norm_w, mla_q_norm_w, mla_w_uq, mla_kv_norm_w, mla_w_ukv,
           gqa_q_norm_w, gqa_k_norm_w, w_out, final_norm_w):
    bsz, n, _ = x.shape
    depth = w_in.shape[0]
    assert ctx.shape[1] == CTX_LEN and n % (2 * TM) == 0 and n % GRID_W == 0
    rows = n // GRID_W
    rope_m = _rope_tables(rows, MLA_ROPE, (MLA_NOPE,))
    rope_g = _rope_tables(rows, HEAD_DIM, (0, HEAD_DIM))

    mod_rows = -(-(bsz + 1) // 8) * 8
    cs = jnp.zeros((mod_rows, D_MODEL), F32).at[:bsz].set(c).at[bsz].set(c_ctx)
    mod_all = _modulation(cs, w_mod, b_mod)
    fnw = final_norm_w[None, :]

    xs = jnp.concatenate([ctx, x], axis=1)
    for l in range(depth):
        lw = _layer_weights(l, norm_w, w_in, gla_w_a_fwd, gla_b_a_fwd, gla_w_a_bwd, gla_b_a_bwd,
                            gla_norm_w, mla_q_norm_w, mla_w_uq, mla_kv_norm_w, mla_w_ukv,
                            gqa_q_norm_w, gqa_k_norm_w, w_out)
        m = mod_all[l]
        mod = jnp.stack([jnp.broadcast_to(m[bsz], (bsz, 3 * D_MODEL)), m[:bsz]], axis=1)
        mod = mod.reshape(bsz, 2, 1, 3 * D_MODEL)
        gq, gk, gv, gg, zg, mq, mk, mvt, cq, ck, cvt = _prep(xs, mod, lw, rope_m, rope_g)
        o_f, o_b = _gla(gq, gk, gv, gg)
        y_b = _attention(mq, mk, mvt, shared_kv=False)
        y_c = _attention(cq, ck, cvt, shared_kv=True)
        xs = _outproj(xs, o_f, o_b, y_b, y_c, zg, mod, lw, fnw, final=(l == depth - 1))
    return xs
```

```python
import functools

import numpy as np
import jax
import jax.numpy as jnp
from jax import lax
from jax.experimental import pallas as pl
from jax.experimental.pallas import tpu as pltpu

D_MODEL = 1024
GRID_W = 64
CTX_LEN = 256
HEAD_DIM = 64
ROPE_THETA = 10000.0
EPS = 1e-6

GLA_HEADS = 4
GLA_DK = 32
GLA_DV = 64
GLA_WIDTH = GLA_HEADS * GLA_DV
GLA_QK = GLA_HEADS * GLA_DK
GLA_GATE_RANK = 16
GLA_GATE_NORM = 16.0
GLA_CHUNK = 64

MLA_HEADS = 6
MLA_NOPE = 64
MLA_ROPE = 32
MLA_V = 64
MLA_Q_LORA = 256
MLA_KV_LORA = 256
MLA_WIDTH = MLA_HEADS * MLA_V

GQA_HEADS = 6
GQA_KV_HEADS = 2
GQA_GROUP = GQA_HEADS // GQA_KV_HEADS
GQA_WIDTH = GQA_HEADS * HEAD_DIM

D_MIX = GLA_WIDTH + MLA_WIDTH + GQA_WIDTH

LANES = 128
TM = CTX_LEN
CHUNKS = TM // GLA_CHUNK
SEG_A = 640
SEG_B = 768
SEG_C = 1920
W_IN_PAD = SEG_A + SEG_B + SEG_C + D_MIX
VMEM_LIMIT = 48 * 1024 * 1024
GLA_BATCH = 4
STEP_BATCH = 2
OUT_BATCH = 4
ATT_CK = 256
ATT_AHEAD = 1
LOG2E = 1.4426950408889634

F32 = jnp.float32
BF16 = jnp.bfloat16


def _cparams(sem):
    return pltpu.CompilerParams(dimension_semantics=sem, vmem_limit_bytes=VMEM_LIMIT)


def _dot(a, b):
    return jnp.dot(a, b, preferred_element_type=F32)


def _dot_nt(a, b):
    return lax.dot_general(a, b, (((1,), (1,)), ((), ())), preferred_element_type=F32)


def _dot_tn(a, b):
    return lax.dot_general(a, b, (((0,), (0,)), ((), ())), preferred_element_type=F32)


def _silu(v):
    return v / (1.0 + jnp.exp(-v))


def _mod_kernel(c_ref, w_ref, b_ref, o_ref):
    s = _silu(c_ref[...]).astype(BF16)
    o_ref[...] = _dot(s, w_ref[...].astype(BF16)) + b_ref[...]


def _modulation(cs, w_mod, b_mod):
    depth = w_mod.shape[0]
    rows = cs.shape[0]
    return pl.pallas_call(
        _mod_kernel,
        grid=(depth, 3),
        in_specs=[
            pl.BlockSpec((rows, D_MODEL), lambda l, j: (0, 0)),
            pl.BlockSpec((None, D_MODEL, D_MODEL), lambda l, j: (l, 0, j)),
            pl.BlockSpec((None, 1, D_MODEL), lambda l, j: (l, 0, j)),
        ],
        out_specs=pl.BlockSpec((None, rows, D_MODEL), lambda l, j: (l, 0, j)),
        out_shape=jax.ShapeDtypeStruct((depth, rows, 3 * D_MODEL), F32),
        compiler_params=_cparams(("arbitrary", "arbitrary")),
        name="modulation",
    )(cs, w_mod, b_mod.reshape(depth, 1, 3 * D_MODEL))


def _prep_kernel(x_ref, mod_ref, nw_ref, win_ref, wa_ref, ba_ref, mqn_ref, wuq_ref, wuqr_ref,
                 mkvn_ref, wuk_ref, wuv_ref, gqn_ref, gqnr_ref, gkn_ref, gknr_ref, rm_ref, rg_ref,
                 gq_ref, gk_ref, gv_ref, gg_ref, zg_ref, mq_ref, mk_ref, mvt_ref, cq_ref, ck_ref,
                 cvt_ref):
    x = x_ref[...]
    mod = mod_ref[...]
    shift, scale = mod[:, :D_MODEL], mod[:, D_MODEL:2 * D_MODEL]
    ms = jnp.mean(x * x, axis=-1, keepdims=True)
    h = (x * lax.rsqrt(ms + EPS) * nw_ref[...]) * (1.0 + scale) + shift
    hb = h.astype(BF16)

    uz = _dot(hb, win_ref[:, SEG_A + SEG_B + SEG_C:])
    zg_ref[...] = _silu(uz).astype(BF16)

    ua = _dot(hb, win_ref[:, 0:SEG_A])
    gq_ref[...] = (ua[:, 0:GLA_QK] * GLA_DK ** -0.5).astype(BF16)
    gk_ref[...] = ua[:, GLA_QK:2 * GLA_QK].astype(BF16)
    gv_ref[...] = ua[:, 2 * GLA_QK:2 * GLA_QK + GLA_WIDTH].astype(BF16)
    a = ua[:, 2 * GLA_QK + GLA_WIDTH:SEG_A].astype(BF16)
    xg = _dot(a, wa_ref[...]) + ba_ref[...]
    log_sig = jnp.minimum(xg, 0.0) - jnp.log1p(jnp.exp(-jnp.abs(xg)))
    gg_ref[...] = log_sig * (1.0 / GLA_GATE_NORM)

    ub = _dot(hb, win_ref[:, SEG_A:SEG_A + SEG_B])
    cq = ub[:, 0:MLA_Q_LORA]
    ckv = ub[:, MLA_Q_LORA:MLA_Q_LORA + MLA_KV_LORA]
    kr = ub[:, MLA_Q_LORA + MLA_KV_LORA:MLA_Q_LORA + MLA_KV_LORA + LANES]
    kr_rot = ub[:, MLA_Q_LORA + MLA_KV_LORA + LANES:SEG_B]
    cqn = cq * lax.rsqrt(jnp.mean(cq * cq, axis=-1, keepdims=True) + EPS) * mqn_ref[...]
    ckvn = ckv * lax.rsqrt(jnp.mean(ckv * ckv, axis=-1, keepdims=True) + EPS) * mkvn_ref[...]
    cqb = cqn.astype(BF16)
    qm = _dot(cqb, wuq_ref[...])
    qm_rot = _dot(cqb, wuqr_ref[...])
    ckvb = ckvn.astype(BF16)
    kn = _dot(ckvb, wuk_ref[...])
    vm = _dot(ckvb, wuv_ref[...])
    cos_m, sin_m = rm_ref[:, :LANES], rm_ref[:, LANES:]
    k_rope = kr * cos_m + kr_rot * sin_m
    mla_scale = (MLA_NOPE + MLA_ROPE) ** -0.5 * LOG2E
    for hd in range(MLA_HEADS):
        sl = slice(hd * LANES, (hd + 1) * LANES)
        mq_ref[:, sl] = ((qm[:, sl] * cos_m + qm_rot[:, sl] * sin_m) * mla_scale).astype(BF16)
        mk_ref[:, sl] = (kn[:, sl] + k_rope).astype(BF16)
    for p in range(MLA_HEADS // 2):
        mvt_ref[p] = vm[:, p * LANES:(p + 1) * LANES].T.astype(BF16)

    uc = _dot(hb, win_ref[:, SEG_A + SEG_B:SEG_A + SEG_B + SEG_C])
    cos_g, sin_g = rg_ref[:, :LANES], rg_ref[:, LANES:]
    qa, qb = cos_g * gqn_ref[...], sin_g * gqnr_ref[...]
    gqa_scale = HEAD_DIM ** -0.5 * LOG2E
    nq = GQA_HEADS * LANES
    for hd in range(GQA_HEADS):
        sl = slice(hd * LANES, (hd + 1) * LANES)
        xh = uc[:, sl]
        xr = uc[:, nq + hd * LANES:nq + (hd + 1) * LANES]
        msq = jnp.sum(xh * xh, axis=-1, keepdims=True) * (1.0 / HEAD_DIM)
        cq_ref[:, sl] = ((xh * qa + xr * qb) * (lax.rsqrt(msq + EPS) * gqa_scale)).astype(BF16)
    k = uc[:, 2 * nq:2 * nq + LANES]
    k_rot = uc[:, 2 * nq + LANES:2 * nq + 2 * LANES]
    lo = lax.broadcasted_iota(jnp.int32, (TM, LANES), 1) < HEAD_DIM
    k2 = k * k
    ms_lo = jnp.sum(jnp.where(lo, k2, 0.0), axis=-1, keepdims=True) * (1.0 / HEAD_DIM)
    ms_hi = jnp.sum(jnp.where(lo, 0.0, k2), axis=-1, keepdims=True) * (1.0 / HEAD_DIM)
    inv = jnp.where(lo, lax.rsqrt(ms_lo + EPS), lax.rsqrt(ms_hi + EPS))
    ck_ref[...] = ((k * (cos_g * gkn_ref[...]) + k_rot * (sin_g * gknr_ref[...])) * inv).astype(BF16)
    cvt_ref[...] = uc[:, 2 * nq + 2 * LANES:SEG_C].T.astype(BF16)


def _prep_multi_kernel(x_ref, mod_ref, *refs):
    shared, outs = refs[:16], refs[16:]
    for e in range(x_ref.shape[0]):
        _prep_kernel(x_ref.at[e], mod_ref.at[e], *shared, *[o.at[e] for o in outs])


def _prep(xs, mod, lw, rope_m, rope_g):
    bsz, t, _ = xs.shape
    nt = t // TM
    nb = STEP_BATCH if bsz % STEP_BATCH == 0 else 1
    tok = lambda w: pl.BlockSpec((nb, TM, w), lambda b, i: (b, i, 0))
    full = lambda a: pl.BlockSpec(a.shape, lambda b, i: (0,) * a.ndim)
    tab = pl.BlockSpec((TM, 2 * LANES), lambda b, i: (i, 0))
    weights = [lw[n] for n in ("norm_w", "w_in", "w_a", "b_a", "mla_q_norm", "w_uq", "w_uq_rot",
                               "mla_kv_norm", "w_uk", "w_uv", "gqa_q_norm", "gqa_q_norm_rot",
                               "gqa_k_norm", "gqa_k_norm_rot")]
    out_shape = [
        jax.ShapeDtypeStruct((bsz, t, GLA_QK), BF16),
        jax.ShapeDtypeStruct((bsz, t, GLA_QK), BF16),
        jax.ShapeDtypeStruct((bsz, t, GLA_WIDTH), BF16),
        jax.ShapeDtypeStruct((bsz, t, 2 * GLA_QK), F32),
        jax.ShapeDtypeStruct((bsz, t, D_MIX), BF16),
        jax.ShapeDtypeStruct((bsz, t, MLA_HEADS * LANES), BF16),
        jax.ShapeDtypeStruct((bsz, t, MLA_HEADS * LANES), BF16),
        jax.ShapeDtypeStruct((bsz, MLA_HEADS // 2, LANES, t), BF16),
        jax.ShapeDtypeStruct((bsz, t, GQA_HEADS * LANES), BF16),
        jax.ShapeDtypeStruct((bsz, t, LANES), BF16),
        jax.ShapeDtypeStruct((bsz, LANES, t), BF16),
    ]
    out_specs = [
        tok(GLA_QK), tok(GLA_QK), tok(GLA_WIDTH), tok(2 * GLA_QK), tok(D_MIX),
        tok(MLA_HEADS * LANES), tok(MLA_HEADS * LANES),
        pl.BlockSpec((nb, MLA_HEADS // 2, LANES, TM), lambda b, i: (b, 0, 0, i)),
        tok(GQA_HEADS * LANES), tok(LANES),
        pl.BlockSpec((nb, LANES, TM), lambda b, i: (b, 0, i)),
    ]
    return pl.pallas_call(
        _prep_multi_kernel,
        grid=(bsz // nb, nt),
        in_specs=[tok(D_MODEL),
                  pl.BlockSpec((nb, None, 1, 3 * D_MODEL), lambda b, i: (b, jnp.minimum(i, 1), 0, 0))]
                 + [full(w) for w in weights] + [tab, tab],
        out_specs=out_specs,
        out_shape=out_shape,
        compiler_params=_cparams(("parallel", "arbitrary")),
        name="prep",
    )(xs, mod, *weights, rope_m, rope_g)


def _gla_direction(q_ref, k_ref, v_ref, g_ref, l_ref, cmask_ref, qmask, omask, smask, s_ref, o_ref,
                   reverse):
    q = q_ref[...].astype(F32)
    k = k_ref[...].astype(F32)
    g = g_ref[...]
    g_hi = g.astype(BF16)
    r1 = g - g_hi.astype(F32)
    g_mid = r1.astype(BF16)
    g_lo = (r1 - g_mid.astype(F32)).astype(BF16)
    tri = l_ref[...]
    cum = _dot(tri, g_hi) + _dot(tri, g_mid) + _dot(tri, g_lo)
    cmask = cmask_ref[...]
    order = range(CHUNKS - 1, -1, -1) if reverse else range(CHUNKS)
    for c in order:
        sl = slice(c * GLA_CHUNK, (c + 1) * GLA_CHUNK)
        cum_c = cum[sl]
        last = cum_c[0:1] if reverse else cum_c[GLA_CHUNK - 1:GLA_CHUNK]
        qd = q[sl] * jnp.exp(cum_c)
        ki = k[sl] * jnp.exp(-cum_c)
        kd = k[sl] * jnp.exp(last - cum_c)
        vc = v_ref[sl, :]
        qh = jnp.concatenate([qd] * GLA_HEADS, axis=0) * qmask
        att = _dot_nt(qh.astype(BF16), ki.astype(BF16)) * cmask
        o_all = _dot(att.astype(BF16), vc) * omask
        o_intra = o_all[0:GLA_CHUNK]
        for hd in range(1, GLA_HEADS):
            o_intra = o_intra + o_all[hd * GLA_CHUNK:(hd + 1) * GLA_CHUNK]
        st = s_ref[...]
        o_inter = _dot_nt(qd.astype(BF16), st.astype(BF16))
        o_ref[sl, :] = o_inter + o_intra
        ds = _dot_tn(vc, kd.astype(BF16)) * smask
        s_ref[...] = st * jnp.exp(last) + ds


def _gla_kernel(qf_ref, kf_ref, vf_ref, gf_ref, qb_ref, kb_ref, vb_ref, gb_ref, lf_ref, lb_ref,
                cmf_ref, cmb_ref, qmask_ref, omask_ref, smask_ref, of_ref, ob_ref, sf_ref, sb_ref):
    @pl.when(pl.program_id(1) == 0)
    def _():
        sf_ref[...] = jnp.zeros_like(sf_ref)
        sb_ref[...] = jnp.zeros_like(sb_ref)

    qmask, omask, smask = qmask_ref[...], omask_ref[...], smask_ref[...]
    for e in range(qf_ref.shape[0]):
        _gla_direction(qf_ref.at[e], kf_ref.at[e], vf_ref.at[e], gf_ref.at[e], lf_ref, cmf_ref,
                       qmask, omask, smask, sf_ref.at[e], of_ref.at[e], False)
        _gla_direction(qb_ref.at[e], kb_ref.at[e], vb_ref.at[e], gb_ref.at[e], lb_ref, cmb_ref,
                       qmask, omask, smask, sb_ref.at[e], ob_ref.at[e], True)


def _gla_consts():
    r = np.arange(TM)
    same = (r[:, None] // GLA_CHUNK) == (r[None, :] // GLA_CHUNK)
    lf = (same & (r[None, :] <= r[:, None])).astype(np.float32)
    lb = (same & (r[None, :] >= r[:, None])).astype(np.float32)
    rows = np.arange(GLA_HEADS * GLA_CHUNK)
    j = np.arange(GLA_CHUNK)
    cmf = ((rows[:, None] % GLA_CHUNK) >= j[None, :]).astype(np.float32)
    cmb = ((rows[:, None] % GLA_CHUNK) <= j[None, :]).astype(np.float32)
    qmask = ((rows[:, None] // GLA_CHUNK) == (np.arange(GLA_QK)[None, :] // GLA_DK)).astype(np.float32)
    omask = ((rows[:, None] // GLA_CHUNK) == (np.arange(GLA_WIDTH)[None, :] // GLA_DV)).astype(np.float32)
    smask = ((np.arange(GLA_WIDTH)[:, None] // GLA_DV) == (np.arange(GLA_QK)[None, :] // GLA_DK)).astype(np.float32)
    return (jnp.asarray(lf, BF16), jnp.asarray(lb, BF16), jnp.asarray(cmf), jnp.asarray(cmb),
            jnp.asarray(qmask), jnp.asarray(omask), jnp.asarray(smask))


def _gla(gq, gk, gv, gg):
    bsz, t, _ = gq.shape
    nt = t // TM
    consts = _gla_consts()
    fwd = lambda b, i: (b, i, 0)
    rev_blk = lambda i: jnp.where(i == 0, 0, nt - i)
    bwd = lambda b, i: (b, rev_blk(i), 0)
    bwd_g = lambda b, i: (b, rev_blk(i), 1)
    nb = GLA_BATCH if bsz % GLA_BATCH == 0 else 1
    spec = lambda w, im: pl.BlockSpec((nb, TM, w), im)
    full = lambda a: pl.BlockSpec(a.shape, lambda b, i: (0,) * a.ndim)
    return pl.pallas_call(
        _gla_kernel,
        grid=(bsz // nb, nt),
        in_specs=[spec(GLA_QK, fwd), spec(GLA_QK, fwd), spec(GLA_WIDTH, fwd), spec(GLA_QK, fwd),
                  spec(GLA_QK, bwd), spec(GLA_QK, bwd), spec(GLA_WIDTH, bwd), spec(GLA_QK, bwd_g)]
                 + [full(a) for a in consts],
        out_specs=[spec(GLA_WIDTH, fwd), spec(GLA_WIDTH, bwd)],
        out_shape=[jax.ShapeDtypeStruct((bsz, t, GLA_WIDTH), F32)] * 2,
        scratch_shapes=[pltpu.VMEM((nb, GLA_WIDTH, GLA_QK), F32)] * 2,
        compiler_params=_cparams(("parallel", "arbitrary")),
        name="gla",
    )(gq, gk, gv, gg, gq, gk, gv, gg, *consts)


def _head_values(vt_ref, h, cols, shared_kv):
    if shared_kv:
        r0 = (h // GQA_GROUP) * HEAD_DIM
        return vt_ref[r0:r0 + HEAD_DIM, cols]
    r0 = (h % 2) * MLA_V
    return vt_ref[h // 2, r0:r0 + MLA_V, cols]


def _write_heads(o_ref, outs_t):
    for p in range(len(outs_t) // 2):
        pair = jnp.concatenate([outs_t[2 * p], outs_t[2 * p + 1]], axis=0)
        o_ref[:, p * LANES:(p + 1) * LANES] = pair.T.astype(o_ref.dtype)


def _attn_ctx_kernel(q_ref, k_ref, vt_ref, o_ref, *, shared_kv):
    heads = q_ref.shape[1] // LANES
    outs_t = []
    for h in range(heads):
        kh = k_ref[...] if shared_kv else k_ref[:, h * LANES:(h + 1) * LANES]
        s = _dot_nt(kh, q_ref[:, h * LANES:(h + 1) * LANES])
        p = jnp.exp2(s - jnp.max(s, axis=0, keepdims=True))
        l = jnp.sum(p, axis=0, keepdims=True)
        outs_t.append(_dot(_head_values(vt_ref, h, slice(None), shared_kv), p.astype(BF16)) / l)
    _write_heads(o_ref, outs_t)


def _attn_kernel(z_ref, qa_ref, qb_ref, k_ref, vt_ref, o_ref, s_ref, *, shared_kv):
    t = k_ref.shape[0]
    heads = qa_ref.shape[1] // LANES
    chunks = [slice(a, a + ATT_CK) for a in range(0, t, ATT_CK)]
    z = z_ref[0]

    def dyn(start, size):
        return pl.ds(pl.multiple_of(z + start, ATT_CK), size)

    def q_of(h):
        sl = slice(h * LANES, (h + 1) * LANES)
        qh = jnp.concatenate([qa_ref[:, sl], qb_ref[:, sl]], axis=0)
        return qh.astype(F32).T.astype(BF16)

    def fold8(v):
        return v.reshape(v.shape[0] // 8, 8, v.shape[1])

    def qk(h, qh, rows):
        kh = k_ref[rows, :] if shared_kv else k_ref[rows, h * LANES:(h + 1) * LANES]
        return _dot(kh, qh)

    def put_scores(s, rows, m8):
        s_ref[dyn(rows.start, ATT_CK), :] = s
        smax = jnp.max(fold8(s), axis=0)
        return smax if m8 is None else jnp.maximum(m8, smax)

    ones = jnp.ones((16, ATT_CK), BF16)
    m8 = None
    q_all = [q_of(h) for h in range(heads)]
    qh = q_all[0]
    for rows in chunks:
        m8 = put_scores(qk(0, qh, rows), rows, m8)
    outs_t = []
    for h in range(heads):
        m = jnp.max(m8, axis=0, keepdims=True)
        m8, acc = None, None
        more = h + 1 < heads
        if more:
            qh = q_all[h + 1]
            ahead = [qk(h + 1, qh, chunks[i]) for i in range(ATT_AHEAD)]
        for c, rows in enumerate(chunks):
            if more:
                s_cur = ahead.pop(0)
                if c + ATT_AHEAD < len(chunks):
                    ahead.append(qk(h + 1, qh, chunks[c + ATT_AHEAD]))
            vt1 = jnp.concatenate([_head_values(vt_ref, h, rows, shared_kv), ones], axis=0)
            p = jnp.exp2(s_ref[dyn(rows.start, ATT_CK), :] - m).astype(BF16)
            if more:
                m8 = put_scores(s_cur, rows, m8)
            pv = _dot(vt1, p)
            acc = pv if acc is None else acc + pv
        outs_t.append(acc[0:HEAD_DIM] / acc[HEAD_DIM:HEAD_DIM + 1])
    _write_heads(o_ref, outs_t)


def _attention(q, k, vt, shared_kv):
    bsz, t, width = q.shape
    n2 = (t - CTX_LEN) // (2 * TM)
    kw = k.shape[2]
    name = "attn_gqa" if shared_kv else "attn_mla"
    if shared_kv:
        v_spec = pl.BlockSpec((None, LANES, t), lambda b, j: (b, 0, 0))
        vc_spec = pl.BlockSpec((None, LANES, CTX_LEN), lambda b: (b, 0, 0))
    else:
        v_spec = pl.BlockSpec((None, vt.shape[1], LANES, t), lambda b, j: (b, 0, 0, 0))
        vc_spec = pl.BlockSpec((None, vt.shape[1], LANES, CTX_LEN), lambda b: (b, 0, 0, 0))
    y_lat = pl.pallas_call(
        functools.partial(_attn_kernel, shared_kv=shared_kv),
        grid=(bsz, n2),
        in_specs=[pl.BlockSpec(memory_space=pltpu.SMEM),
                  pl.BlockSpec((None, TM, width), lambda b, j: (b, 2 * j + 1, 0)),
                  pl.BlockSpec((None, TM, width), lambda b, j: (b, 2 * j + 2, 0)),
                  pl.BlockSpec((None, t, kw), lambda b, j: (b, 0, 0)), v_spec],
        out_specs=pl.BlockSpec((None, 2 * TM, width // 2), lambda b, j: (b, j, 0)),
        out_shape=jax.ShapeDtypeStruct((bsz, t - CTX_LEN, width // 2), BF16),
        scratch_shapes=[pltpu.VMEM((t, 2 * TM), F32)],
        compiler_params=_cparams(("parallel", "arbitrary")),
        name=name,
    )(jnp.zeros((1,), jnp.int32), q, q, k, vt)
    y_ctx = pl.pallas_call(
        functools.partial(_attn_ctx_kernel, shared_kv=shared_kv),
        grid=(bsz,),
        in_specs=[pl.BlockSpec((None, CTX_LEN, width), lambda b: (b, 0, 0)),
                  pl.BlockSpec((None, CTX_LEN, kw), lambda b: (b, 0, 0)), vc_spec],
        out_specs=pl.BlockSpec((None, CTX_LEN, width // 2), lambda b: (b, 0, 0)),
        out_shape=jax.ShapeDtypeStruct((bsz, CTX_LEN, width // 2), BF16),
        compiler_params=_cparams(("parallel",)),
        name=name + "_ctx",
    )(q, k, vt)
    return y_lat, y_ctx


def _outproj_kernel(x_ref, of_ref, ob_ref, yb_ref, ybc_ref, yc_ref, ycc_ref, zg_ref, gnw_ref,
                    mod_ref, wout_ref, fnw_ref, o_ref, *, final):
    lo = lax.broadcasted_iota(jnp.int32, (TM, LANES), 1) < GLA_DV
    o = of_ref[...] + ob_ref[...]
    gnw = gnw_ref[...]
    zg = zg_ref[...]
    parts = []
    for j in range(GLA_WIDTH // LANES):
        sl = slice(j * LANES, (j + 1) * LANES)
        oj = o[:, sl]
        o2 = oj * oj
        ms_lo = jnp.sum(jnp.where(lo, o2, 0.0), axis=-1, keepdims=True) * (1.0 / GLA_DV)
        ms_hi = jnp.sum(jnp.where(lo, 0.0, o2), axis=-1, keepdims=True) * (1.0 / GLA_DV)
        inv = jnp.where(lo, lax.rsqrt(ms_lo + EPS), lax.rsqrt(ms_hi + EPS))
        parts.append(oj * inv * gnw[:, sl] * zg[:, sl])
    if final:
        yb, yc = yb_ref[...], yc_ref[...]
    else:
        is_ctx = pl.program_id(1) == 0
        yb = jnp.where(is_ctx, ybc_ref[...], yb_ref[...])
        yc = jnp.where(is_ctx, ycc_ref[...], yc_ref[...])
    parts.append(yb * zg[:, GLA_WIDTH:GLA_WIDTH + MLA_WIDTH])
    parts.append(yc * zg[:, GLA_WIDTH + MLA_WIDTH:])
    y = jnp.concatenate(parts, axis=-1).astype(BF16)
    gate = mod_ref[...]
    xn = x_ref[...] + gate * _dot(y, wout_ref[...])
    if final:
        xn = xn * lax.rsqrt(jnp.mean(xn * xn, axis=-1, keepdims=True) + EPS) * fnw_ref[...]
    o_ref[...] = xn


def _outproj_multi_kernel(*refs, final):
    per, gnw_ref, mod_ref, wout_ref, fnw_ref, o_ref = refs[:8], refs[8], refs[9], refs[10], refs[11], refs[12]
    for e in range(o_ref.shape[0]):
        _outproj_kernel(*[r.at[e] for r in per], gnw_ref, mod_ref.at[e], wout_ref, fnw_ref,
                        o_ref.at[e], final=final)


def _outproj(xs, o_f, o_b, y_b, y_c, zg, mod, lw, final_norm_w, final):
    bsz, t, _ = xs.shape
    off = 1 if final else 0
    nt = t // TM - off
    nb = OUT_BATCH if bsz % OUT_BATCH == 0 else 1
    tok = lambda w: pl.BlockSpec((nb, TM, w), lambda b, i: (b, i + off, 0))
    lat = lambda w: pl.BlockSpec((nb, TM, w), lambda b, i: (b, jnp.maximum(i + off - 1, 0), 0))
    ctx = lambda w: pl.BlockSpec((nb, CTX_LEN, w), lambda b, i: (b, 0, 0))
    full = lambda a: pl.BlockSpec(a.shape, lambda b, i: (0,) * a.ndim)
    return pl.pallas_call(
        functools.partial(_outproj_multi_kernel, final=final),
        grid=(bsz // nb, nt),
        in_specs=[tok(D_MODEL), tok(GLA_WIDTH), tok(GLA_WIDTH), lat(MLA_WIDTH), ctx(MLA_WIDTH),
                  lat(GQA_WIDTH), ctx(GQA_WIDTH), tok(D_MIX), full(lw["gla_norm"]),
                  pl.BlockSpec((nb, None, 1, D_MODEL),
                               lambda b, i: (b, jnp.minimum(i + off, 1), 0, 2)),
                  full(lw["w_out"]), full(final_norm_w)],
        out_specs=pl.BlockSpec((nb, TM, D_MODEL), lambda b, i: (b, i, 0)),
        out_shape=jax.ShapeDtypeStruct((bsz, nt * TM, D_MODEL), F32),
        compiler_params=_cparams(("parallel", "arbitrary")),
        name="outproj",
    )(xs, o_f, o_b, *y_b, *y_c, zg, lw["gla_norm"], mod, lw["w_out"], final_norm_w)


def _rot_src(d_rot):
    quarter = d_rot // 4
    i = np.arange(d_rot)
    return np.where(i % (2 * quarter) < quarter, i + quarter, i - quarter)


def _rope_tables(rows, d_rot, lanes):
    quarter = d_rot // 4
    n = rows * GRID_W
    row = jnp.repeat(jnp.arange(rows), GRID_W).astype(F32)
    col = jnp.tile(jnp.arange(GRID_W), rows).astype(F32)
    freqs = ROPE_THETA ** (-jnp.arange(quarter, dtype=F32) / quarter)
    ang = jnp.stack([row[:, None] * freqs, col[:, None] * freqs], axis=1)
    cos, sin = jnp.cos(ang), jnp.sin(ang)
    cos_l = jnp.stack([cos, cos], axis=2).reshape(n, d_rot)
    sin_l = jnp.stack([-sin, sin], axis=2).reshape(n, d_rot)
    cos_t = jnp.ones((n, LANES), F32)
    sin_t = jnp.zeros((n, LANES), F32)
    for lane0 in lanes:
        cos_t = cos_t.at[:, lane0:lane0 + d_rot].set(cos_l)
        sin_t = sin_t.at[:, lane0:lane0 + d_rot].set(sin_l)
    ident = jnp.concatenate([jnp.ones((CTX_LEN, LANES), F32), jnp.zeros((CTX_LEN, LANES), F32)], 1)
    return jnp.concatenate([ident, jnp.concatenate([cos_t, sin_t], axis=1)], axis=0)


def _take_cols(w, idx):
    w_ext = jnp.concatenate([w, jnp.zeros((w.shape[0], 1), w.dtype)], axis=1)
    return w_ext[:, np.asarray(idx)]


def _w_in_columns():
    c0 = 2 * GLA_QK + 2 * GLA_WIDTH
    m0 = c0 + 2 * GLA_GATE_RANK
    kr0 = m0 + MLA_Q_LORA + MLA_KV_LORA
    g0 = kr0 + MLA_ROPE + MLA_WIDTH
    gk0 = g0 + GQA_WIDTH
    gv0 = gk0 + GQA_KV_HEADS * HEAD_DIM
    gz0 = gv0 + GQA_KV_HEADS * HEAD_DIM
    pad = lambda n: [-1] * n
    seg_a = list(range(0, 2 * GLA_QK + GLA_WIDTH)) + list(range(c0, m0)) + pad(LANES - 2 * GLA_GATE_RANK)
    rs_m, rs_g = _rot_src(MLA_ROPE), _rot_src(HEAD_DIM)
    tail = pad(LANES - MLA_NOPE - MLA_ROPE)
    seg_b = (list(range(m0, kr0)) + pad(MLA_NOPE) + list(range(kr0, kr0 + MLA_ROPE)) + tail
             + pad(MLA_NOPE) + [kr0 + int(j) for j in rs_m] + tail)
    q_pad, q_rot = [], []
    for h in range(GQA_HEADS):
        base = g0 + h * HEAD_DIM
        own = list(range(base, base + HEAD_DIM))
        rot = [base + int(j) for j in rs_g]
        half = h // GQA_GROUP
        q_pad += own + pad(HEAD_DIM) if half == 0 else pad(HEAD_DIM) + own
        q_rot += rot + pad(HEAD_DIM) if half == 0 else pad(HEAD_DIM) + rot
    k_nat = list(range(gk0, gv0))
    k_rot = [gk0 + kv * HEAD_DIM + int(j) for kv in range(GQA_KV_HEADS) for j in rs_g]
    seg_c = q_pad + q_rot + k_nat + k_rot + list(range(gv0, gz0))
    seg_z = (list(range(2 * GLA_QK + GLA_WIDTH, c0)) + list(range(kr0 + MLA_ROPE, g0))
             + list(range(gz0, gz0 + GQA_WIDTH)))
    cols = seg_a + seg_b + seg_c + seg_z
    assert len(seg_a) == SEG_A and len(seg_b) == SEG_B and len(seg_c) == SEG_C and len(cols) == W_IN_PAD
    return cols


def _layer_weights(l, norm_w, w_in, gla_w_a_fwd, gla_b_a_fwd, gla_w_a_bwd, gla_b_a_bwd, gla_norm_w,
                   mla_q_norm_w, mla_w_uq, mla_kv_norm_w, mla_w_ukv, gqa_q_norm_w, gqa_k_norm_w,
                   w_out):
    w_pad = _take_cols(w_in[l], _w_in_columns()).astype(BF16)

    w_a = jnp.zeros((LANES, 2 * GLA_QK), F32)
    w_a = w_a.at[0:GLA_GATE_RANK, 0:GLA_QK].set(gla_w_a_fwd[l])
    w_a = w_a.at[GLA_GATE_RANK:2 * GLA_GATE_RANK, GLA_QK:].set(gla_w_a_bwd[l])
    b_a = jnp.concatenate([gla_b_a_fwd[l], gla_b_a_bwd[l]])[None, :]

    dq = MLA_NOPE + MLA_ROPE
    rs_m, rs_g = _rot_src(MLA_ROPE), _rot_src(HEAD_DIM)
    uq_cols, uq_rot_cols = [], []
    for h in range(MLA_HEADS):
        uq_cols += list(range(h * dq, (h + 1) * dq)) + [-1] * (LANES - dq)
        uq_rot_cols += ([-1] * MLA_NOPE + [h * dq + MLA_NOPE + int(j) for j in rs_m]
                        + [-1] * (LANES - dq))
    ukv = mla_w_ukv[l].reshape(MLA_KV_LORA, MLA_HEADS, MLA_NOPE + MLA_V)
    uk = jnp.pad(ukv[:, :, :MLA_NOPE], ((0, 0), (0, 0), (0, LANES - MLA_NOPE)))
    uk = uk.reshape(MLA_KV_LORA, MLA_HEADS * LANES)
    uv = ukv[:, :, MLA_NOPE:].reshape(MLA_KV_LORA, MLA_WIDTH)

    both = lambda v: jnp.tile(v, LANES // HEAD_DIM)[None, :]
    return {
        "norm_w": norm_w[l][None, :],
        "w_in": w_pad,
        "w_a": w_a.astype(BF16),
        "b_a": b_a,
        "mla_q_norm": mla_q_norm_w[l][None, :],
        "w_uq": _take_cols(mla_w_uq[l], uq_cols).astype(BF16),
        "w_uq_rot": _take_cols(mla_w_uq[l], uq_rot_cols).astype(BF16),
        "mla_kv_norm": mla_kv_norm_w[l][None, :],
        "w_uk": uk.astype(BF16),
        "w_uv": uv.astype(BF16),
        "gqa_q_norm": both(gqa_q_norm_w[l]),
        "gqa_q_norm_rot": both(gqa_q_norm_w[l][rs_g]),
        "gqa_k_norm": both(gqa_k_norm_w[l]),
        "gqa_k_norm_rot": both(gqa_k_norm_w[l][rs_g]),
        "gla_norm": jnp.tile(gla_norm_w[l], GLA_HEADS)[None, :],
        "w_out": w_out[l].astype(BF16),
    }


def kernel(x, c, ctx, c_ctx, norm_w, w_mod, b_mod, w_in, gla_w_a_fwd, gla_b_a_fwd, gla_w_a_bwd,
           gla_b_a_bwd, gla_norm_w, mla_q_norm_w, mla_w_uq, mla_kv_norm_w, mla_w_ukv,
           gqa_q_norm_w, gqa_k_norm_w, w_out, final_norm_w):
    bsz, n, _ = x.shape
    depth = w_in.shape[0]
    assert ctx.shape[1] == CTX_LEN and n % (2 * TM) == 0 and n % GRID_W == 0
    rows = n // GRID_W
    rope_m = _rope_tables(rows, MLA_ROPE, (MLA_NOPE,))
    rope_g = _rope_tables(rows, HEAD_DIM, (0, HEAD_DIM))

    mod_rows = -(-(bsz + 1) // 8) * 8
    cs = jnp.zeros((mod_rows, D_MODEL), F32).at[:bsz].set(c).at[bsz].set(c_ctx)
    mod_all = _modulation(cs, w_mod, b_mod)
    fnw = final_norm_w[None, :]

    xs = jnp.concatenate([ctx, x], axis=1)
    for l in range(depth):
        lw = _layer_weights(l, norm_w, w_in, gla_w_a_fwd, gla_b_a_fwd, gla_w_a_bwd, gla_b_a_bwd,
                            gla_norm_w, mla_q_norm_w, mla_w_uq, mla_kv_norm_w, mla_w_ukv,
                            gqa_q_norm_w, gqa_k_norm_w, w_out)
        m = mod_all[l]
        mod = jnp.stack([jnp.broadcast_to(m[bsz], (bsz, 3 * D_MODEL)), m[:bsz]], axis=1)
        mod = mod.reshape(bsz, 2, 1, 3 * D_MODEL)
        gq, gk, gv, gg, zg, mq, mk, mvt, cq, ck, cvt = _prep(xs, mod, lw, rope_m, rope_g)
        o_f, o_b = _gla(gq, gk, gv, gg)
        y_b = _attention(mq, mk, mvt, shared_kv=False)
        y_c = _attention(cq, ck, cvt, shared_kv=True)
        xs = _outproj(xs, o_f, o_b, y_b, y_c, zg, mod, lw, fnw, final=(l == depth - 1))
    return xs
```
